```python
import math
import jax, jax.numpy as jnp
from jax import lax
import numpy as np

D_MODEL = 2048
BATCH = 32
SEQ = 256
DEPTH = 4
DEC_BATCH = 4
DEC_SEQ = 4096
PAST_LEN = 512

GRID_W = 64
POOL_GROUPS = 4
POOL_WIDTH = 1024
POOL_GROUP_DIM = POOL_WIDTH // POOL_GROUPS
POOL_WINDOWS = (2, 4, 8, 16)
DIFF_HEADS = 8
DIFF_QK_DIM = 64
DIFF_V_DIM = 128
DIFF_QK_WIDTH = DIFF_HEADS * 2 * DIFF_QK_DIM
DIFF_WIDTH = DIFF_HEADS * DIFF_V_DIM
NA_HEADS = 8
NA_HEAD_DIM = 128
NA_WIDTH = NA_HEADS * NA_HEAD_DIM
NA_WIN_R = 8
NA_WIN_C = 16
NA_KEY_COLS = 2 * NA_WIN_C
SGU_GROUPS = 4
SGU_WIDTH = 1024
SGU_CHUNK = 128
MIX_WIDTH = POOL_WIDTH + DIFF_WIDTH
EVEN_SPLITS = (POOL_WIDTH, POOL_WIDTH, DIFF_QK_WIDTH, DIFF_QK_WIDTH, DIFF_WIDTH, DIFF_WIDTH)
ODD_SPLITS = (NA_WIDTH, NA_WIDTH, NA_WIDTH, NA_WIDTH, SGU_WIDTH, SGU_WIDTH, SGU_WIDTH)
EVEN_IN = sum(EVEN_SPLITS)
ODD_IN = sum(ODD_SPLITS)
QUERY_BLOCK = 128
ROPE_BASE = 10000.0
ROPE_AXIS_DIM = DIFF_QK_DIM // 2
LN_EPS = 1e-5
NEG_INF = -1e30
DEEPNORM_ALPHA = (2 * DEPTH) ** 0.25
DEEPNORM_BETA = (8 * DEPTH) ** -0.25

kernel_name = "hybrid_diffusion_pool_diffattn_natten_sgu_step"


def split_cols(x, sizes):
    idx = [int(i) for i in np.cumsum(sizes)[:-1]]
    return jnp.split(x, idx, axis=-1)


def layer_norm(x, g, b=None):
    xf = x.astype(jnp.float32)
    mu = jnp.mean(xf, -1, keepdims=True)
    var = jnp.mean(jnp.square(xf - mu), -1, keepdims=True)
    y = ((xf - mu) * lax.rsqrt(var + LN_EPS)).astype(x.dtype) * g
    return y if b is None else y + b


def rms_norm(x, g):
    xf = x.astype(jnp.float32)
    y = xf * lax.rsqrt(jnp.mean(jnp.square(xf), -1, keepdims=True) + LN_EPS)
    return y.astype(x.dtype) * g


def ada_params(cond, w_mod, b_mod):
    m = jax.nn.silu(cond) @ w_mod + b_mod
    return jnp.split(m, 3, axis=-1)


def _rotate(x, ang):
    cos = jnp.cos(ang).astype(x.dtype)
    sin = jnp.sin(ang).astype(x.dtype)
    x1, x2 = jnp.split(x, 2, axis=-1)
    return jnp.concatenate([x1 * cos - x2 * sin, x1 * sin + x2 * cos], axis=-1)


def rope_2d(x):
    L = x.shape[2]
    t = jnp.arange(L)
    row = (t // GRID_W).astype(jnp.float32)[:, None]
    col = (t % GRID_W).astype(jnp.float32)[:, None]
    inv = 1.0 / (ROPE_BASE ** (jnp.arange(0, ROPE_AXIS_DIM, 2, dtype=jnp.float32) / ROPE_AXIS_DIM))
    xr, xc = jnp.split(x, 2, axis=-1)
    return jnp.concatenate([_rotate(xr, row * inv), _rotate(xc, col * inv)], axis=-1)


def map_query_blocks(fn, *qs):
    b, h, L, _ = qs[0].shape
    nb = L // QUERY_BLOCK
    blocks = tuple(jnp.moveaxis(q.reshape(b, h, nb, QUERY_BLOCK, q.shape[-1]), 2, 0) for q in qs)
    out = lax.map(fn, blocks)
    return jnp.moveaxis(out, 0, 2).reshape(b, h, L, out.shape[-1])


def softmax_attention(q, k, v):
    scale = q.shape[-1] ** -0.5

    def block(qs):
        (qb,) = qs
        s = jnp.einsum('bhqd,bhkd->bhqk', qb, k).astype(jnp.float32) * scale
        p = jax.nn.softmax(s, axis=-1).astype(v.dtype)
        return jnp.einsum('bhqk,bhkd->bhqd', p, v)

    return map_query_blocks(block, q)


def diff_attention(q1, q2, k1, k2, v, lam):
    scale = DIFF_QK_DIM ** -0.5

    def block(qs):
        qb1, qb2 = qs
        s1 = jnp.einsum('bhqd,bhkd->bhqk', qb1, k1).astype(jnp.float32) * scale
        s2 = jnp.einsum('bhqd,bhkd->bhqk', qb2, k2).astype(jnp.float32) * scale
        p = jax.nn.softmax(s1, axis=-1) - lam * jax.nn.softmax(s2, axis=-1)
        return jnp.einsum('bhqk,bhkd->bhqd', p.astype(v.dtype), v)

    return map_query_blocks(block, q1, q2)


def _na_column_tables():
    ncb = GRID_W // NA_WIN_C
    blk = np.arange(ncb)
    qcol = blk[:, None] * NA_WIN_C + np.arange(NA_WIN_C)[None, :]
    kstart = np.clip(blk * NA_WIN_C - NA_WIN_C // 2, 0, GRID_W - NA_KEY_COLS)
    kcol = kstart[:, None] + np.arange(NA_KEY_COLS)[None, :]
    cstart = np.clip(qcol - NA_WIN_C // 2, 0, GRID_W - NA_WIN_C)
    valid = (kcol[:, None, :] >= cstart[:, :, None]) & (kcol[:, None, :] < cstart[:, :, None] + NA_WIN_C)
    dc = np.clip(kcol[:, None, :] - qcol[:, :, None] + NA_WIN_C - 1, 0, 2 * NA_WIN_C - 2)
    return kcol, valid, dc


def neighborhood_attention(q, k, v, ck, cv, rpb):
    B, H, L, d = q.shape
    rows = L // GRID_W
    win_r = min(NA_WIN_R, rows)
    ncb = GRID_W // NA_WIN_C
    kcol, valid, dc = _na_column_tables()
    nloc = win_r * NA_KEY_COLS
    kg = k.reshape(B, H, rows, GRID_W, d)
    vg = v.reshape(B, H, rows, GRID_W, v.shape[-1])
    qg = jnp.moveaxis(q.reshape(B, H, rows, ncb, NA_WIN_C, d), 2, 0)
    mask = jnp.asarray(np.broadcast_to(valid[:, :, None, :], (ncb, NA_WIN_C, win_r, NA_KEY_COLS)).reshape(ncb, NA_WIN_C, nloc))
    rpb_cols = rpb[:, :, dc]
    scale = d ** -0.5

    def row_step(args):
        r, qr = args
        rs = jnp.clip(r - win_r // 2, 0, rows - win_r)
        kr = lax.dynamic_slice_in_dim(kg, rs, win_r, axis=2)
        vr = lax.dynamic_slice_in_dim(vg, rs, win_r, axis=2)
        kb = jnp.moveaxis(kr[:, :, :, kcol, :], 2, 3).reshape(B, H, ncb, nloc, d)
        vb = jnp.moveaxis(vr[:, :, :, kcol, :], 2, 3).reshape(B, H, ncb, nloc, vr.shape[-1])
        dr = rs + jnp.arange(win_r) - r + NA_WIN_R - 1
        bias = jnp.transpose(rpb_cols[:, dr], (0, 2, 3, 1, 4)).reshape(H, ncb, NA_WIN_C, nloc)
        s_loc = jnp.einsum('bhnqd,bhnkd->bhnqk', qr, kb).astype(jnp.float32) * scale + bias.astype(jnp.float32)
        s_loc = jnp.where(mask, s_loc, NEG_INF)
        s_ctx = jnp.einsum('bhnqd,bhpd->bhnqp', qr, ck).astype(jnp.float32) * scale
        p = jax.nn.softmax(jnp.concatenate([s_loc, s_ctx], axis=-1), axis=-1).astype(v.dtype)
        return (jnp.einsum('bhnqk,bhnkd->bhnqd', p[..., :nloc], vb)
                + jnp.einsum('bhnqp,bhpd->bhnqd', p[..., nloc:], cv))

    out = lax.map(row_step, (jnp.arange(rows), qg))
    return jnp.moveaxis(out, 0, 2).reshape(B, H, L, out.shape[-1])


def multi_scale_pool(x, w_pool, pool_scale):
    B, L, C = x.shape
    xf = x.reshape(B, L, POOL_GROUPS, POOL_GROUP_DIM).astype(jnp.float32)
    cs = jnp.concatenate([jnp.zeros_like(xf[:, :1]), jnp.cumsum(xf, axis=1)], axis=1)
    t = jnp.arange(L)[:, None]
    half = jnp.asarray(POOL_WINDOWS)[None, :] // 2
    lo = jnp.clip(t - half, 0, L)
    hi = jnp.clip(t + half, 0, L)
    g = jnp.arange(POOL_GROUPS)[None, :]
    win_sum = cs[:, hi, g] - cs[:, lo, g]
    cnt = (hi - lo).astype(jnp.float32)[None, :, :, None]
    pooled = (win_sum / cnt - xf).astype(x.dtype)
    mixed = jnp.einsum('blgc,gcd->blgd', pooled, w_pool)
    return mixed.reshape(B, L, C) * pool_scale


def spatial_gating(u, v, ln_g, w_s, b_s):
    B, L, C = v.shape
    n = L // SGU_CHUNK
    vn = layer_norm(v, ln_g).reshape(B, n, SGU_CHUNK, SGU_GROUPS, C // SGU_GROUPS)
    s = jnp.einsum('gij,bnjgc->bnigc', w_s, vn) + jnp.transpose(b_s)[:, :, None]
    return u * s.reshape(B, L, C)


def even_branch(h, w_in, pool_w, pool_scale, diff_lam, diff_subln, layer_idx, ctx_k=None, ctx_v=None):
    B, L, _ = h.shape
    a_in, a_gate, q, k, v, b_gate = split_cols(h @ w_in, EVEN_SPLITS)
    ya = multi_scale_pool(a_in, pool_w, pool_scale) * jax.nn.silu(a_gate)
    q = q.reshape(B, L, DIFF_HEADS, 2, DIFF_QK_DIM).transpose(0, 2, 1, 3, 4)
    k = k.reshape(B, L, DIFF_HEADS, 2, DIFF_QK_DIM).transpose(0, 2, 1, 3, 4)
    v = v.reshape(B, L, DIFF_HEADS, DIFF_V_DIM).transpose(0, 2, 1, 3)
    q1, q2, k1, k2 = q[:, :, :, 0], q[:, :, :, 1], k[:, :, :, 0], k[:, :, :, 1]
    lam_init = 0.8 - 0.6 * math.exp(-0.3 * layer_idx)
    lam = (jnp.exp(jnp.sum(diff_lam[0] * diff_lam[1]).astype(jnp.float32))
           - jnp.exp(jnp.sum(diff_lam[2] * diff_lam[3]).astype(jnp.float32)) + lam_init)
    if ctx_k is None:
        new_k, new_v = jnp.concatenate([k1, k2], axis=-1), v
        kk1, kk2, vv = k1, k2, v
    else:
        new_k = new_v = None
        q1, q2 = rope_2d(q1), rope_2d(q2)
        kk1 = jnp.concatenate([rope_2d(k1), ctx_k[..., :DIFF_QK_DIM]], axis=2)
        kk2 = jnp.concatenate([rope_2d(k2), ctx_k[..., DIFF_QK_DIM:]], axis=2)
        vv = jnp.concatenate([v, ctx_v], axis=2)
    o = diff_attention(q1, q2, kk1, kk2, vv, lam)
    o = rms_norm(o, diff_subln) * (1.0 - lam_init)
    yb = o.transpose(0, 2, 1, 3).reshape(B, L, DIFF_WIDTH) * jax.nn.silu(b_gate)
    return jnp.concatenate([ya, yb], axis=-1), new_k, new_v


def odd_branch(h, w_in, rpb, sgu_ln, sgu_w, sgu_b, ctx_k=None, ctx_v=None):
    B, L, _ = h.shape
    q, k, v, c_gate, u, vs, d_gate = split_cols(h @ w_in, ODD_SPLITS)
    q, k, v = (t.reshape(B, L, NA_HEADS, NA_HEAD_DIM).transpose(0, 2, 1, 3) for t in (q, k, v))
    if ctx_k is None:
        new_k, new_v = k, v
        o = softmax_attention(q, k, v)
    else:
        new_k = new_v = None
        o = neighborhood_attention(q, k, v, ctx_k, ctx_v, rpb)
    yc = o.transpose(0, 2, 1, 3).reshape(B, L, NA_WIDTH) * jax.nn.silu(c_gate)
    yd = spatial_gating(u, vs, sgu_ln, sgu_w, sgu_b) * jax.nn.silu(d_gate)
    return jnp.concatenate([yc, yd], axis=-1), new_k, new_v


def setup_inputs(seed: int = 0) -> dict:
    key = jax.random.key(seed)
    ks = iter(jax.random.split(key, 64))

    def nrm(shape, s):
        return jax.random.normal(next(ks), shape, jnp.float32) * s

    d = {}
    d["x_prompt"] = nrm((BATCH, SEQ, D_MODEL), 1.0)
    d["x_sample"] = nrm((DEC_BATCH, DEC_SEQ, D_MODEL), 1.0)
    for l in range(DEPTH):
        if l % 2 == 0:
            d[f"cache_k_l{l}"] = nrm((DEC_BATCH, DIFF_HEADS, PAST_LEN, 2 * DIFF_QK_DIM), 1.0)
            d[f"cache_v_l{l}"] = nrm((DEC_BATCH, DIFF_HEADS, PAST_LEN, DIFF_V_DIM), 1.0)
        else:
            d[f"cache_k_l{l}"] = nrm((DEC_BATCH, NA_HEADS, PAST_LEN, NA_HEAD_DIM), 1.0)
            d[f"cache_v_l{l}"] = nrm((DEC_BATCH, NA_HEADS, PAST_LEN, NA_HEAD_DIM), 1.0)
    d["c"] = nrm((DEC_BATCH, D_MODEL), 1.0)
    d["c_ctx"] = nrm((D_MODEL,), 1.0)
    for l in range(DEPTH):
        d[f"w_mod_{l}"] = nrm((D_MODEL, 3 * D_MODEL), 0.5 * D_MODEL ** -0.5)
        d[f"b_mod_{l}"] = nrm((3 * D_MODEL,), 0.01)
        d[f"w_in_{l}"] = nrm((D_MODEL, EVEN_IN if l % 2 == 0 else ODD_IN), D_MODEL ** -0.5)
        d[f"w_out_{l}"] = nrm((MIX_WIDTH, D_MODEL), MIX_WIDTH ** -0.5 * DEEPNORM_BETA)
        d[f"ln_g_{l}"] = 1.0 + nrm((D_MODEL,), 0.01)
        d[f"ln_b_{l}"] = nrm((D_MODEL,), 0.01)
        if l % 2 == 0:
            d[f"pool_w_{l}"] = nrm((POOL_GROUPS, POOL_GROUP_DIM, POOL_GROUP_DIM), POOL_GROUP_DIM ** -0.5)
            d[f"pool_scale_{l}"] = 1.0 + nrm((POOL_WIDTH,), 0.1)
            d[f"diff_lam_{l}"] = nrm((4, DIFF_QK_DIM), 0.1)
            d[f"diff_subln_{l}"] = 1.0 + nrm((DIFF_V_DIM,), 0.01)
        else:
            d[f"rpb_{l}"] = nrm((NA_HEADS, 2 * NA_WIN_R - 1, 2 * NA_WIN_C - 1), 0.1)
            d[f"sgu_ln_{l}"] = 1.0 + nrm((SGU_WIDTH,), 0.01)
            d[f"sgu_w_{l}"] = nrm((SGU_GROUPS, SGU_CHUNK, SGU_CHUNK), SGU_CHUNK ** -0.5)
            d[f"sgu_b_{l}"] = 1.0 + nrm((SGU_GROUPS, SGU_CHUNK), 0.1)
    return d


def reference(x_prompt, x_sample,
              cache_k_l0, cache_v_l0, cache_k_l1, cache_v_l1,
              cache_k_l2, cache_v_l2, cache_k_l3, cache_v_l3,
              c, c_ctx,
              w_mod_0, b_mod_0, w_in_0, w_out_0, ln_g_0, ln_b_0, pool_w_0, pool_scale_0, diff_lam_0, diff_subln_0,
              w_mod_1, b_mod_1, w_in_1, w_out_1, ln_g_1, ln_b_1, rpb_1, sgu_ln_1, sgu_w_1, sgu_b_1,
              w_mod_2, b_mod_2, w_in_2, w_out_2, ln_g_2, ln_b_2, pool_w_2, pool_scale_2, diff_lam_2, diff_subln_2,
              w_mod_3, b_mod_3, w_in_3, w_out_3, ln_g_3, ln_b_3, rpb_3, sgu_ln_3, sgu_w_3, sgu_b_3):
    cache_k = [cache_k_l0, cache_k_l1, cache_k_l2, cache_k_l3]
    cache_v = [cache_v_l0, cache_v_l1, cache_v_l2, cache_v_l3]
    w_mod = [w_mod_0, w_mod_1, w_mod_2, w_mod_3]
    b_mod = [b_mod_0, b_mod_1, b_mod_2, b_mod_3]
    w_in = [w_in_0, w_in_1, w_in_2, w_in_3]
    w_out = [w_out_0, w_out_1, w_out_2, w_out_3]
    ln_g = [ln_g_0, ln_g_1, ln_g_2, ln_g_3]
    ln_b = [ln_b_0, ln_b_1, ln_b_2, ln_b_3]
    even_p = {0: (pool_w_0, pool_scale_0, diff_lam_0, diff_subln_0),
              2: (pool_w_2, pool_scale_2, diff_lam_2, diff_subln_2)}
    odd_p = {1: (rpb_1, sgu_ln_1, sgu_w_1, sgu_b_1),
             3: (rpb_3, sgu_ln_3, sgu_w_3, sgu_b_3)}

    xp, xs = x_prompt, x_sample
    ks, vs = [], []
    for l in range(DEPTH):
        sh_p, sc_p, g_p = ada_params(c_ctx, w_mod[l], b_mod[l])
        sh_s, sc_s, g_s = (m[:, None, :] for m in ada_params(c, w_mod[l], b_mod[l]))
        hp = xp * (1.0 + sc_p) + sh_p
        hs = xs * (1.0 + sc_s) + sh_s
        if l % 2 == 0:
            yp, nk, nv = even_branch(hp, w_in[l], *even_p[l], l)
            ys, _, _ = even_branch(hs, w_in[l], *even_p[l], l, cache_k[l], cache_v[l])
        else:
            yp, nk, nv = odd_branch(hp, w_in[l], *odd_p[l])
            ys, _, _ = odd_branch(hs, w_in[l], *odd_p[l], cache_k[l], cache_v[l])
        xp = layer_norm(DEEPNORM_ALPHA * xp + g_p * (yp @ w_out[l]), ln_g[l], ln_b[l])
        xs = layer_norm(DEEPNORM_ALPHA * xs + g_s * (ys @ w_out[l]), ln_g[l], ln_b[l])
        ks.append(nk)
        vs.append(nv)
    return (xp, xs, ks[0], vs[0], ks[1], vs[1], ks[2], vs[2], ks[3], vs[3])
```

```python
import functools
import math

import numpy as np
import jax
import jax.numpy as jnp
from jax import lax
from jax.experimental import pallas as pl
from jax.experimental.pallas import tpu as pltpu

F32 = jnp.float32
BF16 = jnp.bfloat16

D_MODEL = 2048
DEPTH = 4
GRID_W = 64
GRID_ROWS = 64
HEADS = 8
HEAD_DIM = 128
QK_HALF = 64
POOL_WINDOWS = (2, 4, 8, 16)
POOL_GROUP_DIM = 256
SGU_CHUNK = 128
SGU_GROUP_DIM = 256
NA_WIN_R = 8
NA_WIN_C = 16
NA_Q_ROWS = 8
NA_K_ROWS = 16
ROPE_BASE = 10000.0
LN_EPS = 1e-5
NEG_INF = -1e30
DEEPNORM_ALPHA = (2 * DEPTH) ** 0.25

VMEM_LIMIT_BYTES = 56 * 1024 * 1024
COND_ROWS = 8
CTX_ROW = 4

PROJ_TM = 1024
PROJ_TN = 1024
OUT_TM = 512
ADA_TN = 512
POOL_ROWS = 256
POOL_HALO = 16
SGU_TM = 512
DIFF_TQ = 256
PREP_ROWS = 512


def _params(*sem):
    return pltpu.CompilerParams(dimension_semantics=sem, vmem_limit_bytes=VMEM_LIMIT_BYTES)


def _silu(x):
    return x * jax.nn.sigmoid(x)


def _dot_nt(a, b):
    return lax.dot_general(a, b, (((1,), (1,)), ((), ())), preferred_element_type=F32)


def _ada_kernel(c_ref, w_ref, b_ref, o_ref):
    a = _silu(c_ref[...]).astype(BF16)
    o_ref[...] = jnp.dot(a, w_ref[...].astype(BF16), preferred_element_type=F32) + b_ref[...]


def ada_params(cond, w_mod, b_mod):
    n = w_mod.shape[1]
    return pl.pallas_call(
        _ada_kernel,
        out_shape=jax.ShapeDtypeStruct((COND_ROWS, n), F32),
        grid=(n // ADA_TN,),
        in_specs=[pl.BlockSpec((COND_ROWS, D_MODEL), lambda j: (0, 0)),
                  pl.BlockSpec((D_MODEL, ADA_TN), lambda j: (0, j)),
                  pl.BlockSpec((1, ADA_TN), lambda j: (0, j))],
        out_specs=pl.BlockSpec((COND_ROWS, ADA_TN), lambda j: (0, j)),
        compiler_params=_params("parallel"),
        name="ada_params",
    )(cond, w_mod, b_mod.reshape(1, n))


def _inproj_kernel(x_ref, sh_ref, sc_ref, w_ref, o_ref, h_scr):
    @pl.when(pl.program_id(1) == 0)
    def _():
        h = x_ref[...] * (1.0 + sc_ref[0]) + sh_ref[0]
        h_scr[...] = h.astype(BF16)

    o_ref[...] = jnp.dot(h_scr[...], w_ref[...], preferred_element_type=F32).astype(o_ref.dtype)


def in_projection(x2d, mod3, w_bf16, rows_per_cond, out_dtype):
    m = x2d.shape[0]
    n = w_bf16.shape[1]
    if rows_per_cond is None:
        cond_row = lambda i: CTX_ROW
    else:
        blocks_per_cond = rows_per_cond // PROJ_TM
        cond_row = lambda i: i // blocks_per_cond
    return pl.pallas_call(
        _inproj_kernel,
        out_shape=jax.ShapeDtypeStruct((m, n), out_dtype),
        grid=(m // PROJ_TM, n // PROJ_TN),
        in_specs=[pl.BlockSpec((PROJ_TM, D_MODEL), lambda i, j: (i, 0)),
                  pl.BlockSpec((1, 1, D_MODEL), lambda i, j: (cond_row(i), 0, 0)),
                  pl.BlockSpec((1, 1, D_MODEL), lambda i, j: (cond_row(i), 0, 1)),
                  pl.BlockSpec((D_MODEL, PROJ_TN), lambda i, j: (0, j))],
        out_specs=pl.BlockSpec((PROJ_TM, PROJ_TN), lambda i, j: (i, j)),
        scratch_shapes=[pltpu.VMEM((PROJ_TM, D_MODEL), BF16)],
        compiler_params=_params("parallel", "arbitrary"),
        name="in_projection",
    )(x2d, mod3, mod3, w_bf16)


def _outproj_kernel(ya_ref, yb_ref, x_ref, g_ref, w_ref, lng_ref, lnb_ref, o_ref):
    half = ya_ref.shape[1]
    acc = jnp.dot(ya_ref[...], w_ref[0:half, :], preferred_element_type=F32)
    acc = acc + jnp.dot(yb_ref[...], w_ref[half:, :], preferred_element_type=F32)
    z = DEEPNORM_ALPHA * x_ref[...] + g_ref[0] * acc
    mu = jnp.mean(z, axis=-1, keepdims=True)
    zc = z - mu
    var = jnp.mean(zc * zc, axis=-1, keepdims=True)
    o_ref[...] = zc * lax.rsqrt(var + LN_EPS) * lng_ref[...] + lnb_ref[...]


def out_projection(ya, yb, x2d, mod3, w_bf16, ln_g, ln_b, rows_per_cond):
    m = x2d.shape[0]
    half = ya.shape[1]
    if rows_per_cond is None:
        cond_row = lambda i: CTX_ROW
    else:
        blocks_per_cond = rows_per_cond // OUT_TM
        cond_row = lambda i: i // blocks_per_cond
    return pl.pallas_call(
        _outproj_kernel,
        out_shape=jax.ShapeDtypeStruct((m, D_MODEL), F32),
        grid=(m // OUT_TM,),
        in_specs=[pl.BlockSpec((OUT_TM, half), lambda i: (i, 0)),
                  pl.BlockSpec((OUT_TM, half), lambda i: (i, 0)),
                  pl.BlockSpec((OUT_TM, D_MODEL), lambda i: (i, 0)),
                  pl.BlockSpec((1, 1, D_MODEL), lambda i: (cond_row(i), 0, 2)),
                  pl.BlockSpec((2 * half, D_MODEL), lambda i: (0, 0)),
                  pl.BlockSpec((1, D_MODEL), lambda i: (0, 0)),
                  pl.BlockSpec((1, D_MODEL), lambda i: (0, 0))],
        out_specs=pl.BlockSpec((OUT_TM, D_MODEL), lambda i: (i, 0)),
        compiler_params=_params("parallel"),
        name="out_projection",
    )(ya, yb, x2d, mod3, w_bf16, ln_g.reshape(1, D_MODEL), ln_b.reshape(1, D_MODEL))


def _pool_kernel(a_ref, g_ref, w_ref, s_ref, o_ref, ext_scr, *, seq):
    n_chunks = seq // POOL_ROWS
    group = pl.program_id(1)

    def run(half):
        def chunk(ci, carry):
            r0 = pl.multiple_of(ci * POOL_ROWS, POOL_ROWS)
            main = a_ref[0, pl.ds(r0, POOL_ROWS), :].astype(F32)
            if n_chunks > 1:
                p0 = pl.multiple_of(jnp.maximum(r0 - POOL_HALO, 0), POOL_HALO)
                n0 = pl.multiple_of(jnp.minimum(r0 + POOL_ROWS, seq - POOL_HALO), POOL_HALO)
                prev = a_ref[0, pl.ds(p0, POOL_HALO), :].astype(F32)
                nxt = a_ref[0, pl.ds(n0, POOL_HALO), :].astype(F32)
                prev = jnp.where(ci > 0, prev, 0.0)
                nxt = jnp.where(ci < n_chunks - 1, nxt, 0.0)
            else:
                prev = jnp.zeros((POOL_HALO, POOL_GROUP_DIM), F32)
                nxt = prev
            ext_scr[0:POOL_HALO, :] = prev
            ext_scr[POOL_HALO:POOL_HALO + POOL_ROWS, :] = main
            ext_scr[POOL_HALO + POOL_ROWS:, :] = nxt
            win = ext_scr[POOL_HALO - half:POOL_HALO - half + POOL_ROWS, :]
            for j in range(-half + 1, half):
                win = win + ext_scr[POOL_HALO + j:POOL_HALO + j + POOL_ROWS, :]
            t = r0 + lax.broadcasted_iota(jnp.int32, (POOL_ROWS, 1), 0)
            cnt = (jnp.minimum(t + half, seq) - jnp.maximum(t - half, 0)).astype(F32)
            pooled = win / cnt - main
            mixed = jnp.dot(pooled.astype(BF16), w_ref[0], preferred_element_type=F32)
            gate = g_ref[0, pl.ds(r0, POOL_ROWS), :].astype(F32)
            y = mixed * s_ref[...] * _silu(gate)
            o_ref[0, pl.ds(r0, POOL_ROWS), :] = y.astype(o_ref.dtype)
            return carry

        lax.fori_loop(0, n_chunks, chunk, 0)

    for gi, window in enumerate(POOL_WINDOWS):
        pl.when(group == gi)(functools.partial(run, window // 2))


def pool_mixer(p3, w_pool_bf16, pool_scale):
    b, seq, _ = p3.shape
    groups = len(POOL_WINDOWS)
    return pl.pallas_call(
        functools.partial(_pool_kernel, seq=seq),
        out_shape=jax.ShapeDtypeStruct((b, seq, groups * POOL_GROUP_DIM), BF16),
        grid=(b, groups),
        in_specs=[pl.BlockSpec((1, seq, POOL_GROUP_DIM), lambda i, g: (i, 0, g)),
                  pl.BlockSpec((1, seq, POOL_GROUP_DIM), lambda i, g: (i, 0, groups + g)),
                  pl.BlockSpec((1, POOL_GROUP_DIM, POOL_GROUP_DIM), lambda i, g: (g, 0, 0)),
                  pl.BlockSpec((1, POOL_GROUP_DIM), lambda i, g: (0, g))],
        out_specs=pl.BlockSpec((1, seq, POOL_GROUP_DIM), lambda i, g: (i, 0, g)),
        scratch_shapes=[pltpu.VMEM((POOL_ROWS + 2 * POOL_HALO, POOL_GROUP_DIM), F32)],
        compiler_params=_params("parallel", "parallel"),
        name="pool_mixer",
    )(p3, p3, w_pool_bf16, pool_scale.reshape(1, groups * POOL_GROUP_DIM))


def _rope_tables():
    t = np.arange(GRID_ROWS * GRID_W)
    row = (t // GRID_W).astype(np.float64)[:, None]
    col = (t % GRID_W).astype(np.float64)[:, None]
    axis_dim = QK_HALF // 2
    inv = 1.0 / (ROPE_BASE ** (np.arange(0, axis_dim, 2, dtype=np.float64) / axis_dim))
    lane = np.arange(HEAD_DIM)
    within = lane % QK_HALF
    pos = np.where((within < axis_dim)[None, :], row, col)
    ang = pos * inv[within % (axis_dim // 2)][None, :]
    is_a = ((within % axis_dim) < axis_dim // 2)[None, :]
    cos = np.cos(ang)
    sin = np.sin(ang)
    return (cos.astype(np.float32),
            np.where(is_a, -sin, 0.0).astype(np.float32),
            np.where(is_a, 0.0, sin).astype(np.float32))


def _rotate_lanes(x, cos, sin_a, sin_b):
    shift = QK_HALF // 4
    return (x * cos + pltpu.roll(x, HEAD_DIM - shift, 1) * sin_a
            + pltpu.roll(x, shift, 1) * sin_b)


def _diff_attn_kernel(*refs, seq, tq, ctx_len, rope, emit_cache, lam_init):
    it = iter(refs)
    q_ref, k_ref, v_ref, g_ref = next(it), next(it), next(it), next(it)
    if ctx_len:
        ck_ref, cv_ref = next(it), next(it)
    if rope:
        cos_ref, sa_ref, sb_ref = next(it), next(it), next(it)
    lam_ref, sub_ref = next(it), next(it)
    o_ref = next(it)
    if emit_cache:
        kc_ref, vc_ref = next(it), next(it)
    k_scr, v_scr = next(it), next(it)

    qi = pl.program_id(2)
    prep_rows = min(PREP_ROWS, seq)

    @pl.when(qi == 0)
    def _prep():
        def chunk(ci, carry):
            r0 = pl.multiple_of(ci * prep_rows, prep_rows)
            rows = pl.ds(r0, prep_rows)
            k = k_ref[0, rows, :]
            v = v_ref[0, rows, :]
            if emit_cache:
                kc_ref[0, 0, rows, :] = k.astype(F32)
                vc_ref[0, 0, rows, :] = v.astype(F32)
            if rope:
                k = _rotate_lanes(k.astype(F32), cos_ref[rows, :], sa_ref[rows, :], sb_ref[rows, :])
            k_scr[rows, :] = k.astype(BF16)
            v_scr[rows, :] = v.astype(BF16)
            return carry

        lax.fori_loop(0, seq // prep_rows, chunk, 0)
        if ctx_len:
            k_scr[seq:, :] = ck_ref[0, 0].astype(BF16)
            v_scr[seq:, :] = cv_ref[0, 0].astype(BF16)

    dl = lam_ref[...]
    lam = (jnp.exp(jnp.sum(dl[0:1] * dl[1:2], axis=-1, keepdims=True))
           - jnp.exp(jnp.sum(dl[2:3] * dl[3:4], axis=-1, keepdims=True)) + lam_init)

    q = q_ref[0].astype(F32)
    if rope:
        rows = pl.ds(pl.multiple_of(qi * tq, tq), tq)
        q = _rotate_lanes(q, cos_ref[rows, :], sa_ref[rows, :], sb_ref[rows, :])
    q = q * (QK_HALF ** -0.5)
    lane = lax.broadcasted_iota(jnp.int32, (1, HEAD_DIM), 1)
    q1 = jnp.where(lane < QK_HALF, q, 0.0).astype(BF16)
    q2 = jnp.where(lane >= QK_HALF, q, 0.0).astype(BF16)
    keys = k_scr[...]
    s1 = _dot_nt(q1, keys)
    s2 = _dot_nt(q2, keys)
    e1 = jnp.exp(s1 - jnp.max(s1, axis=-1, keepdims=True))
    e2 = jnp.exp(s2 - jnp.max(s2, axis=-1, keepdims=True))
    w1 = 1.0 / jnp.sum(e1, axis=-1, keepdims=True)
    w2 = lam / jnp.sum(e2, axis=-1, keepdims=True)
    p = (e1 * w1 - e2 * w2).astype(BF16)
    o = jnp.dot(p, v_scr[...], preferred_element_type=F32)
    o = o * lax.rsqrt(jnp.mean(o * o, axis=-1, keepdims=True) + LN_EPS)
    o = o * sub_ref[...] * (1.0 - lam_init)
    o_ref[0] = (o * _silu(g_ref[0].astype(F32))).astype(o_ref.dtype)


def diff_attention(p3, diff_lam, diff_subln, layer_idx, ctx_k=None, ctx_v=None):
    b, seq, _ = p3.shape
    latent = ctx_k is not None
    ctx_len = ctx_k.shape[2] if latent else 0
    tq = min(DIFF_TQ, seq)
    lam_init = 0.8 - 0.6 * math.exp(-0.3 * layer_idx)
    q0, k0, v0, g0 = 2 * HEADS, 3 * HEADS, 4 * HEADS, 5 * HEADS

    in_specs = [pl.BlockSpec((1, tq, HEAD_DIM), lambda i, h, t: (i, t, q0 + h)),
                pl.BlockSpec((1, seq, HEAD_DIM), lambda i, h, t: (i, 0, k0 + h)),
                pl.BlockSpec((1, seq, HEAD_DIM), lambda i, h, t: (i, 0, v0 + h)),
                pl.BlockSpec((1, tq, HEAD_DIM), lambda i, h, t: (i, t, g0 + h))]
    args = [p3, p3, p3, p3]
    if latent:
        ctx_spec = pl.BlockSpec((1, 1, ctx_len, HEAD_DIM), lambda i, h, t: (i, h, 0, 0))
        in_specs += [ctx_spec, ctx_spec]
        args += [ctx_k, ctx_v]
        tab_spec = pl.BlockSpec((seq, HEAD_DIM), lambda i, h, t: (0, 0))
        in_specs += [tab_spec] * 3
        args += [jnp.asarray(tab) for tab in _rope_tables()]
    in_specs += [pl.BlockSpec(diff_lam.shape, lambda i, h, t: (0, 0)),
                 pl.BlockSpec((1, HEAD_DIM), lambda i, h, t: (0, 0))]
    args += [diff_lam, diff_subln.reshape(1, HEAD_DIM)]

    out_shape = [jax.ShapeDtypeStruct((b, seq, HEADS * HEAD_DIM), BF16)]
    out_specs = [pl.BlockSpec((1, tq, HEAD_DIM), lambda i, h, t: (i, t, h))]
    if not latent:
        cache_spec = pl.BlockSpec((1, 1, seq, HEAD_DIM), lambda i, h, t: (i, h, 0, 0))
        out_shape += [jax.ShapeDtypeStruct((b, HEADS, seq, HEAD_DIM), F32)] * 2
        out_specs += [cache_spec, cache_spec]

    outs = pl.pallas_call(
        functools.partial(_diff_attn_kernel, seq=seq, tq=tq, ctx_len=ctx_len, rope=latent,
                          emit_cache=not latent, lam_init=lam_init),
        out_shape=out_shape,
        grid=(b, HEADS, seq // tq),
        in_specs=in_specs,
        out_specs=out_specs,
        scratch_shapes=[pltpu.VMEM((seq + ctx_len, HEAD_DIM), BF16),
                        pltpu.VMEM((seq + ctx_len, HEAD_DIM), BF16)],
        compiler_params=_params("parallel", "parallel", "arbitrary"),
        name="diff_attention_latent" if latent else "diff_attention_context",
    )(*args)
    return outs if not latent else (outs[0], None, None)


def _ctx_attn_kernel(q_ref, k_ref, v_ref, g_ref, o_ref, kc_ref, vc_ref):
    k = k_ref[0]
    v = v_ref[0]
    kc_ref[0, 0] = k.astype(F32)
    vc_ref[0, 0] = v.astype(F32)
    q = (q_ref[0].astype(F32) * (HEAD_DIM ** -0.5)).astype(BF16)
    s = _dot_nt(q, k.astype(BF16))
    e = jnp.exp(s - jnp.max(s, axis=-1, keepdims=True))
    o = jnp.dot(e.astype(BF16), v.astype(BF16), preferred_element_type=F32)
    o = o / jnp.sum(e, axis=-1, keepdims=True)
    o_ref[0] = (o * _silu(g_ref[0].astype(F32))).astype(o_ref.dtype)


def context_attention(p3):
    b, seq, _ = p3.shape
    blk = lambda off: pl.BlockSpec((1, seq, HEAD_DIM), lambda i, h: (i, 0, off + h))
    cache_spec = pl.BlockSpec((1, 1, seq, HEAD_DIM), lambda i, h: (i, h, 0, 0))
    return pl.pallas_call(
        _ctx_attn_kernel,
        out_shape=[jax.ShapeDtypeStruct((b, seq, HEADS * HEAD_DIM), BF16),
                   jax.ShapeDtypeStruct((b, HEADS, seq, HEAD_DIM), F32),
                   jax.ShapeDtypeStruct((b, HEADS, seq, HEAD_DIM), F32)],
        grid=(b, HEADS),
        in_specs=[blk(0), blk(HEADS), blk(2 * HEADS), blk(3 * HEADS)],
        out_specs=[pl.BlockSpec((1, seq, HEAD_DIM), lambda i, h: (i, 0, h)), cache_spec, cache_spec],
        compiler_params=_params("parallel", "parallel"),
        name="context_attention",
    )(p3, p3, p3, p3)


NA_TILE_Q = NA_Q_ROWS * GRID_W
NA_TILE_K = NA_K_ROWS * GRID_W
NA_TILES = GRID_ROWS // NA_Q_ROWS
NA_CASES = 3


def _na_key_row_start(tile):
    return int(np.clip(NA_Q_ROWS * tile - NA_WIN_R // 2, 0, GRID_ROWS - NA_K_ROWS))


def _na_bias_tables(rpb):
    qc = np.arange(GRID_W)[:, None]
    kc = np.arange(GRID_W)[None, :]
    cstart = np.clip(qc - NA_WIN_C // 2, 0, GRID_W - NA_WIN_C)
    col_valid = (kc >= cstart) & (kc < cstart + NA_WIN_C)
    dc = np.clip(kc - qc + NA_WIN_C - 1, 0, 2 * NA_WIN_C - 2)
    per_dr = jnp.where(jnp.asarray(col_valid), rpb[:, :, dc], NEG_INF)
    n_dr = 2 * NA_WIN_R - 1
    per_dr = jnp.concatenate([per_dr, jnp.full_like(per_dr[:, :1], NEG_INF)], axis=1)
    dr_idx = np.full((NA_CASES, NA_Q_ROWS, NA_K_ROWS), n_dr, np.int32)
    for case, tile in enumerate((0, 1, NA_TILES - 1)):
        ks = _na_key_row_start(tile)
        for qr_local in range(NA_Q_ROWS):
            qr = NA_Q_ROWS * tile + qr_local
            rs = int(np.clip(qr - NA_WIN_R // 2, 0, GRID_ROWS - NA_WIN_R))
            for kr_local in range(NA_K_ROWS):
                kr = ks + kr_local
                if rs <= kr < rs + NA_WIN_R:
                    dr_idx[case, qr_local, kr_local] = kr - qr + NA_WIN_R - 1
    big = per_dr[:, dr_idx]
    big = jnp.transpose(big, (0, 1, 2, 4, 3, 5))
    return big.reshape(HEADS, NA_CASES, NA_TILE_Q, NA_TILE_K)


def _na_kernel(q_ref, k_ref, v_ref, g_ref, ck_ref, cv_ref, bias_ref, o_ref):
    tile = pl.program_id(2)
    case = jnp.where(tile == 0, 0, jnp.where(tile == NA_TILES - 1, 2, 1))
    key_row = jnp.clip(NA_Q_ROWS * tile - NA_WIN_R // 2, 0, GRID_ROWS - NA_K_ROWS)
    keys = pl.ds(pl.multiple_of(key_row * GRID_W, 256), NA_TILE_K)
    q = (q_ref[0].astype(F32) * (HEAD_DIM ** -0.5)).astype(BF16)
    s_loc = _dot_nt(q, k_ref[0, keys, :].astype(BF16)) + bias_ref[0, case]
    s_ctx = _dot_nt(q, ck_ref[0, 0].astype(BF16))
    m = jnp.maximum(jnp.max(s_loc, axis=-1, keepdims=True), jnp.max(s_ctx, axis=-1, keepdims=True))
    e_loc = jnp.exp(s_loc - m)
    e_ctx = jnp.exp(s_ctx - m)
    denom = jnp.sum(e_loc, axis=-1, keepdims=True) + jnp.sum(e_ctx, axis=-1, keepdims=True)
    o = jnp.dot(e_loc.astype(BF16), v_ref[0, keys, :].astype(BF16), preferred_element_type=F32)
    o = o + jnp.dot(e_ctx.astype(BF16), cv_ref[0, 0].astype(BF16), preferred_element_type=F32)
    o = o / denom
    o_ref[0] = (o * _silu(g_ref[0].astype(F32))).astype(o_ref.dtype)


def neighborhood_attention(p3, ctx_k, ctx_v, rpb):
    b, seq, _ = p3.shape
    ctx_len = ctx_k.shape[2]
    bias = _na_bias_tables(rpb)
    ctx_spec = pl.BlockSpec((1, 1, ctx_len, HEAD_DIM), lambda h, i, t: (i, h, 0, 0))
    return pl.pallas_call(
        _na_kernel,
        out_shape=jax.ShapeDtypeStruct((b, seq, HEADS * HEAD_DIM), BF16),
        grid=(HEADS, b, NA_TILES),
        in_specs=[pl.BlockSpec((1, NA_TILE_Q, HEAD_DIM), lambda h, i, t: (i, t, h)),
                  pl.BlockSpec((1, seq, HEAD_DIM), lambda h, i, t: (i, 0, HEADS + h)),
                  pl.BlockSpec((1, seq, HEAD_DIM), lambda h, i, t: (i, 0, 2 * HEADS + h)),
                  pl.BlockSpec((1, NA_TILE_Q, HEAD_DIM), lambda h, i, t: (i, t, 3 * HEADS + h)),
                  ctx_spec, ctx_spec,
                  pl.BlockSpec((1, NA_CASES, NA_TILE_Q, NA_TILE_K), lambda h, i, t: (h, 0, 0, 0))],
        out_specs=pl.BlockSpec((1, NA_TILE_Q, HEAD_DIM), lambda h, i, t: (i, t, h)),
        compiler_params=_params("parallel", "parallel", "arbitrary"),
        name="neighborhood_attention",
    )(p3, p3, p3, p3, ctx_k, ctx_v, bias)


def _sgu_kernel(u_ref, v_ref, g_ref, ln_ref, w_ref, b_ref, o_ref):
    v = v_ref[...].astype(F32)
    mu = jnp.mean(v, axis=-1, keepdims=True)
    vc = v - mu
    var = jnp.mean(vc * vc, axis=-1, keepdims=True)
    vn = (vc * lax.rsqrt(var + LN_EPS) * ln_ref[...]).astype(BF16)
    groups = w_ref.shape[0]
    for n in range(v.shape[0] // SGU_CHUNK):
        rows = slice(n * SGU_CHUNK, (n + 1) * SGU_CHUNK)
        for g in range(groups):
            cols = slice(g * SGU_GROUP_DIM, (g + 1) * SGU_GROUP_DIM)
            s = jnp.dot(w_ref[g], vn[rows, cols], preferred_element_type=F32) + b_ref[:, g:g + 1]
            y = u_ref[rows, cols].astype(F32) * s * _silu(g_ref[rows, cols].astype(F32))
            o_ref[rows, cols] = y.astype(o_ref.dtype)


def spatial_gating(p2, sgu_ln, w_s_bf16, b_s):
    m = p2.shape[0]
    groups = w_s_bf16.shape[0]
    width = groups * SGU_GROUP_DIM
    blk = lambda c: pl.BlockSpec((SGU_TM, width), lambda i: (i, c))
    return pl.pallas_call(
        _sgu_kernel,
        out_shape=jax.ShapeDtypeStruct((m, width), BF16),
        grid=(m // SGU_TM,),
        in_specs=[blk(4), blk(5), blk(6),
                  pl.BlockSpec((1, width), lambda i: (0, 0)),
                  pl.BlockSpec((groups, SGU_CHUNK, SGU_CHUNK), lambda i: (0, 0, 0)),
                  pl.BlockSpec((SGU_CHUNK, groups), lambda i: (0, 0))],
        out_specs=pl.BlockSpec((SGU_TM, width), lambda i: (i, 0)),
        compiler_params=_params("parallel"),
        name="spatial_gating",
    )(p2, p2, p2, sgu_ln.reshape(1, width), w_s_bf16, jnp.transpose(b_s))


def kernel(x_prompt, x_sample, cache_k_l0, cache_v_l0, cache_k_l1, cache_v_l1, cache_k_l2, cache_v_l2, cache_k_l3, cache_v_l3, c, c_ctx, w_mod_0, b_mod_0, w_in_0, w_out_0, ln_g_0, ln_b_0, pool_w_0, pool_scale_0, diff_lam_0, diff_subln_0, w_mod_1, b_mod_1, w_in_1, w_out_1, ln_g_1, ln_b_1, rpb_1, sgu_ln_1, sgu_w_1, sgu_b_1, w_mod_2, b_mod_2, w_in_2, w_out_2, ln_g_2, ln_b_2, pool_w_2, pool_scale_2, diff_lam_2, diff_subln_2, w_mod_3, b_mod_3, w_in_3, w_out_3, ln_g_3, ln_b_3, rpb_3, sgu_ln_3, sgu_w_3, sgu_b_3):
    cache_k = [cache_k_l0, cache_k_l1, cache_k_l2, cache_k_l3]
    cache_v = [cache_v_l0, cache_v_l1, cache_v_l2, cache_v_l3]
    w_mod = [w_mod_0, w_mod_1, w_mod_2, w_mod_3]
    b_mod = [b_mod_0, b_mod_1, b_mod_2, b_mod_3]
    w_in = [w_in_0, w_in_1, w_in_2, w_in_3]
    w_out = [w_out_0, w_out_1, w_out_2, w_out_3]
    ln_g = [ln_g_0, ln_g_1, ln_g_2, ln_g_3]
    ln_b = [ln_b_0, ln_b_1, ln_b_2, ln_b_3]
    even_p = {0: (pool_w_0, pool_scale_0, diff_lam_0, diff_subln_0),
              2: (pool_w_2, pool_scale_2, diff_lam_2, diff_subln_2)}
    odd_p = {1: (rpb_1, sgu_ln_1, sgu_w_1, sgu_b_1),
             3: (rpb_3, sgu_ln_3, sgu_w_3, sgu_b_3)}

    bp, lp, _ = x_prompt.shape
    bs, ls, _ = x_sample.shape
    xp = x_prompt.reshape(bp * lp, D_MODEL)
    xs = x_sample.reshape(bs * ls, D_MODEL)
    cond = jnp.concatenate([c, c_ctx[None, :], jnp.zeros((COND_ROWS - bs - 1, D_MODEL), F32)], axis=0)

    new_k, new_v = [], []
    for l in range(DEPTH):
        mod3 = ada_params(cond, w_mod[l], b_mod[l]).reshape(COND_ROWS, 1, 3 * D_MODEL)
        w_in_l = w_in[l].astype(BF16)
        n_in = w_in_l.shape[1]
        pp = in_projection(xp, mod3, w_in_l, None, F32)
        ps = in_projection(xs, mod3, w_in_l, ls, BF16)
        pp3 = pp.reshape(bp, lp, n_in)
        ps3 = ps.reshape(bs, ls, n_in)
        if l % 2 == 0:
            pool_w, pool_scale, diff_lam, diff_subln = even_p[l]
            pool_w = pool_w.astype(BF16)
            ya_p = pool_mixer(pp3, pool_w, pool_scale)
            ya_s = pool_mixer(ps3, pool_w, pool_scale)
            yb_p, nk, nv = diff_attention(pp3, diff_lam, diff_subln, l)
            yb_s, _, _ = diff_attention(ps3, diff_lam, diff_subln, l, cache_k[l], cache_v[l])
        else:
            rpb, sgu_ln, sgu_w, sgu_b = odd_p[l]
            sgu_w = sgu_w.astype(BF16)
            ya_p, nk, nv = context_attention(pp3)
            ya_s = neighborhood_attention(ps3, cache_k[l], cache_v[l], rpb)
            yb_p = spatial_gating(pp, sgu_ln, sgu_w, sgu_b)
            yb_s = spatial_gating(ps, sgu_ln, sgu_w, sgu_b)
        half = ya_p.shape[-1]
        w_out_l = w_out[l].astype(BF16)
        xp = out_projection(ya_p.reshape(bp * lp, half), yb_p.reshape(bp * lp, half), xp, mod3,
                            w_out_l, ln_g[l], ln_b[l], None)
        xs = out_projection(ya_s.reshape(bs * ls, half), yb_s.reshape(bs * ls, half), xs, mod3,
                            w_out_l, ln_g[l], ln_b[l], ls)
        new_k.append(nk)
        new_v.append(nv)

    return (xp.reshape(bp, lp, D_MODEL), xs.reshape(bs, ls, D_MODEL),
            new_k[0], new_v[0], new_k[1], new_v[1], new_k[2], new_v[2], new_k[3], new_v[3])
```

```python
import functools
import math

import numpy as np
import jax
import jax.numpy as jnp
from jax import lax
from jax.experimental import pallas as pl
from jax.experimental.pallas import tpu as pltpu

F32 = jnp.float32
BF16 = jnp.bfloat16

D_MODEL = 2048
DEPTH = 4
GRID_W = 64
GRID_ROWS = 64
HEADS = 8
HEAD_DIM = 128
QK_HALF = 64
POOL_WINDOWS = (2, 4, 8, 16)
POOL_GROUP_DIM = 256
SGU_CHUNK = 128
SGU_GROUP_DIM = 256
NA_WIN_R = 8
NA_WIN_C = 16
NA_Q_ROWS = 8
NA_K_ROWS = 16
ROPE_BASE = 10000.0
LN_EPS = 1e-5
NEG_INF = -1e30
DEEPNORM_ALPHA = (2 * DEPTH) ** 0.25
LOG2E = math.log2(math.e)

VMEM_LIMIT_BYTES = 56 * 1024 * 1024
COND_ROWS = 8
CTX_ROW = 4

PROJ_TM = 1024
PROJ_TN = 1024
OUT_TM = 512
ADA_TN = 512
POOL_ROWS = 256
POOL_HALO = 16
SGU_TM = 512
DIFF_TQ = 256
DIFF_KEY_CHUNK = 512
PREP_ROWS = 512


def _params(*sem):
    return pltpu.CompilerParams(dimension_semantics=sem, vmem_limit_bytes=VMEM_LIMIT_BYTES)


def _silu(x):
    return x * jax.nn.sigmoid(x)


def _dot_nt(a, b):
    return lax.dot_general(a, b, (((1,), (1,)), ((), ())), preferred_element_type=F32)


def _ada_kernel(c_ref, w_ref, b_ref, o_ref):
    a = _silu(c_ref[...]).astype(BF16)
    o_ref[...] = jnp.dot(a, w_ref[...].astype(BF16), preferred_element_type=F32) + b_ref[...]


def ada_params(cond, w_mod, b_mod):
    n = w_mod.shape[1]
    return pl.pallas_call(
        _ada_kernel,
        out_shape=jax.ShapeDtypeStruct((COND_ROWS, n), F32),
        grid=(n // ADA_TN,),
        in_specs=[pl.BlockSpec((COND_ROWS, D_MODEL), lambda j: (0, 0)),
                  pl.BlockSpec((D_MODEL, ADA_TN), lambda j: (0, j)),
                  pl.BlockSpec((1, ADA_TN), lambda j: (0, j))],
        out_specs=pl.BlockSpec((COND_ROWS, ADA_TN), lambda j: (0, j)),
        compiler_params=_params("parallel"),
        name="ada_params",
    )(cond, w_mod, b_mod.reshape(1, n))


def _inproj_kernel(x_ref, sh_ref, sc_ref, w_ref, o_ref, h_scr):
    @pl.when(pl.program_id(1) == 0)
    def _():
        h = x_ref[...] * (1.0 + sc_ref[0]) + sh_ref[0]
        h_scr[...] = h.astype(BF16)

    o_ref[...] = jnp.dot(h_scr[...], w_ref[...], preferred_element_type=F32).astype(o_ref.dtype)


def in_projection(x2d, mod3, w_bf16, rows_per_cond, out_dtype):
    m = x2d.shape[0]
    n = w_bf16.shape[1]
    if rows_per_cond is None:
        cond_row = lambda i: CTX_ROW
    else:
        blocks_per_cond = rows_per_cond // PROJ_TM
        cond_row = lambda i: i // blocks_per_cond
    return pl.pallas_call(
        _inproj_kernel,
        out_shape=jax.ShapeDtypeStruct((m, n), out_dtype),
        grid=(m // PROJ_TM, n // PROJ_TN),
        in_specs=[pl.BlockSpec((PROJ_TM, D_MODEL), lambda i, j: (i, 0)),
                  pl.BlockSpec((1, 1, D_MODEL), lambda i, j: (cond_row(i), 0, 0)),
                  pl.BlockSpec((1, 1, D_MODEL), lambda i, j: (cond_row(i), 0, 1)),
                  pl.BlockSpec((D_MODEL, PROJ_TN), lambda i, j: (0, j))],
        out_specs=pl.BlockSpec((PROJ_TM, PROJ_TN), lambda i, j: (i, j)),
        scratch_shapes=[pltpu.VMEM((PROJ_TM, D_MODEL), BF16)],
        compiler_params=_params("parallel", "arbitrary"),
        name="in_projection",
    )(x2d, mod3, mod3, w_bf16)


def _outproj_kernel(ya_ref, yb_ref, x_ref, g_ref, w_ref, lng_ref, lnb_ref, o_ref):
    half = ya_ref.shape[1]
    acc = jnp.dot(ya_ref[...], w_ref[0:half, :], preferred_element_type=F32)
    acc = acc + jnp.dot(yb_ref[...], w_ref[half:, :], preferred_element_type=F32)
    z = DEEPNORM_ALPHA * x_ref[...] + g_ref[0] * acc
    mu = jnp.mean(z, axis=-1, keepdims=True)
    zc = z - mu
    var = jnp.mean(zc * zc, axis=-1, keepdims=True)
    o_ref[...] = zc * lax.rsqrt(var + LN_EPS) * lng_ref[...] + lnb_ref[...]


def out_projection(ya, yb, x2d, mod3, w_bf16, ln_g, ln_b, rows_per_cond):
    m = x2d.shape[0]
    half = ya.shape[1]
    if rows_per_cond is None:
        cond_row = lambda i: CTX_ROW
    else:
        blocks_per_cond = rows_per_cond // OUT_TM
        cond_row = lambda i: i // blocks_per_cond
    return pl.pallas_call(
        _outproj_kernel,
        out_shape=jax.ShapeDtypeStruct((m, D_MODEL), F32),
        grid=(m // OUT_TM,),
        in_specs=[pl.BlockSpec((OUT_TM, half), lambda i: (i, 0)),
                  pl.BlockSpec((OUT_TM, half), lambda i: (i, 0)),
                  pl.BlockSpec((OUT_TM, D_MODEL), lambda i: (i, 0)),
                  pl.BlockSpec((1, 1, D_MODEL), lambda i: (cond_row(i), 0, 2)),
                  pl.BlockSpec((2 * half, D_MODEL), lambda i: (0, 0)),
                  pl.BlockSpec((1, D_MODEL), lambda i: (0, 0)),
                  pl.BlockSpec((1, D_MODEL), lambda i: (0, 0))],
        out_specs=pl.BlockSpec((OUT_TM, D_MODEL), lambda i: (i, 0)),
        compiler_params=_params("parallel"),
        name="out_projection",
    )(ya, yb, x2d, mod3, w_bf16, ln_g.reshape(1, D_MODEL), ln_b.reshape(1, D_MODEL))


def _pool_kernel(a_ref, g_ref, w_ref, s_ref, o_ref, ext_scr, *, seq):
    n_chunks = seq // POOL_ROWS
    group = pl.program_id(1)

    def run(half):
        def chunk(ci, carry):
            r0 = pl.multiple_of(ci * POOL_ROWS, POOL_ROWS)
            main = a_ref[0, pl.ds(r0, POOL_ROWS), :].astype(F32)
            if n_chunks > 1:
                p0 = pl.multiple_of(jnp.maximum(r0 - POOL_HALO, 0), POOL_HALO)
                n0 = pl.multiple_of(jnp.minimum(r0 + POOL_ROWS, seq - POOL_HALO), POOL_HALO)
                prev = a_ref[0, pl.ds(p0, POOL_HALO), :].astype(F32)
                nxt = a_ref[0, pl.ds(n0, POOL_HALO), :].astype(F32)
                prev = jnp.where(ci > 0, prev, 0.0)
                nxt = jnp.where(ci < n_chunks - 1, nxt, 0.0)
            else:
                prev = jnp.zeros((POOL_HALO, POOL_GROUP_DIM), F32)
                nxt = prev
            ext_scr[0:POOL_HALO, :] = prev
            ext_scr[POOL_HALO:POOL_HALO + POOL_ROWS, :] = main
            ext_scr[POOL_HALO + POOL_ROWS:, :] = nxt
            win = ext_scr[POOL_HALO - half:POOL_HALO - half + POOL_ROWS, :]
            for j in range(-half + 1, half):
                win = win + ext_scr[POOL_HALO + j:POOL_HALO + j + POOL_ROWS, :]
            t = r0 + lax.broadcasted_iota(jnp.int32, (POOL_ROWS, 1), 0)
            cnt = (jnp.minimum(t + half, seq) - jnp.maximum(t - half, 0)).astype(F32)
            pooled = win / cnt - main
            mixed = jnp.dot(pooled.astype(BF16), w_ref[0], preferred_element_type=F32)
            gate = g_ref[0, pl.ds(r0, POOL_ROWS), :].astype(F32)
            y = mixed * s_ref[...] * _silu(gate)
            o_ref[0, pl.ds(r0, POOL_ROWS), :] = y.astype(o_ref.dtype)
            return carry

        lax.fori_loop(0, n_chunks, chunk, 0)

    for gi, window in enumerate(POOL_WINDOWS):
        pl.when(group == gi)(functools.partial(run, window // 2))


def pool_mixer(p3, w_pool_bf16, pool_scale):
    b, seq, _ = p3.shape
    groups = len(POOL_WINDOWS)
    return pl.pallas_call(
        functools.partial(_pool_kernel, seq=seq),
        out_shape=jax.ShapeDtypeStruct((b, seq, groups * POOL_GROUP_DIM), BF16),
        grid=(b, groups),
        in_specs=[pl.BlockSpec((1, seq, POOL_GROUP_DIM), lambda i, g: (i, 0, g)),
                  pl.BlockSpec((1, seq, POOL_GROUP_DIM), lambda i, g: (i, 0, groups + g)),
                  pl.BlockSpec((1, POOL_GROUP_DIM, POOL_GROUP_DIM), lambda i, g: (g, 0, 0)),
                  pl.BlockSpec((1, POOL_GROUP_DIM), lambda i, g: (0, g))],
        out_specs=pl.BlockSpec((1, seq, POOL_GROUP_DIM), lambda i, g: (i, 0, g)),
        scratch_shapes=[pltpu.VMEM((POOL_ROWS + 2 * POOL_HALO, POOL_GROUP_DIM), F32)],
        compiler_params=_params("parallel", "parallel"),
        name="pool_mixer",
    )(p3, p3, w_pool_bf16, pool_scale.reshape(1, groups * POOL_GROUP_DIM))


def _rope_tables():
    t = np.arange(GRID_ROWS * GRID_W)
    row = (t // GRID_W).astype(np.float64)[:, None]
    col = (t % GRID_W).astype(np.float64)[:, None]
    axis_dim = QK_HALF // 2
    inv = 1.0 / (ROPE_BASE ** (np.arange(0, axis_dim, 2, dtype=np.float64) / axis_dim))
    lane = np.arange(HEAD_DIM)
    within = lane % QK_HALF
    pos = np.where((within < axis_dim)[None, :], row, col)
    ang = pos * inv[within % (axis_dim // 2)][None, :]
    is_a = ((within % axis_dim) < axis_dim // 2)[None, :]
    cos = np.cos(ang)
    sin = np.sin(ang)
    return (cos.astype(np.float32),
            np.where(is_a, -sin, 0.0).astype(np.float32),
            np.where(is_a, 0.0, sin).astype(np.float32))


def _rotate_lanes(x, cos, sin_a, sin_b):
    shift = QK_HALF // 4
    return (x * cos + pltpu.roll(x, HEAD_DIM - shift, 1) * sin_a
            + pltpu.roll(x, shift, 1) * sin_b)


def _diff_lambda(lam_ref, lam_init):
    dl = lam_ref[...]
    return (jnp.exp(jnp.sum(dl[0:1] * dl[1:2], axis=-1, keepdims=True))
            - jnp.exp(jnp.sum(dl[2:3] * dl[3:4], axis=-1, keepdims=True)) + lam_init)


def _split_components(q):
    lane = lax.broadcasted_iota(jnp.int32, (1, HEAD_DIM), 1)
    return (jnp.where(lane < QK_HALF, q, 0.0).astype(BF16),
            jnp.where(lane >= QK_HALF, q, 0.0).astype(BF16))


def _diff_combine(e1, e2, lam, v, sub, lam_init, gate):
    w1 = 1.0 / jnp.sum(e1, axis=-1, keepdims=True)
    w2 = lam / jnp.sum(e2, axis=-1, keepdims=True)
    p = (e1 * w1 - e2 * w2).astype(BF16)
    o = jnp.dot(p, v, preferred_element_type=F32)
    o = o * lax.rsqrt(jnp.mean(o * o, axis=-1, keepdims=True) + LN_EPS)
    o = o * sub * (1.0 - lam_init)
    return o * _silu(gate)


def _diff_latent_kernel(q_ref, k_ref, v_ref, g_ref, ck_ref, cv_ref, cos_ref, sa_ref, sb_ref,
                        lam_ref, sub_ref, o_ref, kt_scr, v_scr, s_even, s_odd, m_even, m_odd,
                        *, seq, tq, lam_init):
    t = pl.program_id(2)
    nq = seq // tq

    def prep():
        def chunk(ci, carry):
            rows = pl.ds(pl.multiple_of(ci * PREP_ROWS, PREP_ROWS), PREP_ROWS)
            k = _rotate_lanes(k_ref[0, rows, :].astype(F32), cos_ref[rows, :], sa_ref[rows, :], sb_ref[rows, :])
            kt_scr[:, rows] = jnp.transpose(k).astype(BF16)
            v_scr[rows, :] = v_ref[0, rows, :].astype(BF16)
            return carry

        lax.fori_loop(0, seq // PREP_ROWS, chunk, 0)
        kt_scr[:, seq:] = jnp.transpose(ck_ref[0, 0]).astype(BF16)
        v_scr[seq:, :] = cv_ref[0, 0].astype(BF16)

    nk = kt_scr.shape[1]
    slabs = DIFF_KEY_CHUNK // HEAD_DIM

    def step(new, old):
        if new is not None:
            s_new, m_new = new
            rows = pl.ds(pl.multiple_of(jnp.minimum(t, nq - 1) * tq, tq), tq)
            q = _rotate_lanes(q_ref[0].astype(F32), cos_ref[rows, :], sa_ref[rows, :], sb_ref[rows, :])
            qs = _split_components(q * (QK_HALF ** -0.5 * LOG2E))
            run_max = [None, None]
        if old is not None:
            s_old, m_old = old
            run_sum = [None, None]
        for c in range(nk // DIFF_KEY_CHUNK):
            c0 = c * DIFF_KEY_CHUNK
            for comp in range(2):
                if new is not None:
                    s = jnp.dot(qs[comp], kt_scr[:, c0:c0 + DIFF_KEY_CHUNK], preferred_element_type=F32)
                    s_new[comp, :, c0:c0 + DIFF_KEY_CHUNK] = s
                    for j in range(slabs):
                        slab = s[:, j * HEAD_DIM:(j + 1) * HEAD_DIM]
                        run_max[comp] = slab if run_max[comp] is None else jnp.maximum(run_max[comp], slab)
                if old is not None:
                    for j in range(slabs):
                        cols = slice(c0 + j * HEAD_DIM, c0 + (j + 1) * HEAD_DIM)
                        e = jnp.exp2(s_old[comp, :, cols] - m_old[comp])
                        s_old[comp, :, cols] = e
                        run_sum[comp] = e if run_sum[comp] is None else run_sum[comp] + e
        if new is not None:
            for comp in range(2):
                m_new[comp] = jnp.broadcast_to(jnp.max(run_max[comp], axis=-1, keepdims=True), (tq, HEAD_DIM))
        if old is not None:
            lam = _diff_lambda(lam_ref, lam_init)
            w1 = jnp.broadcast_to(1.0 / jnp.sum(run_sum[0], axis=-1, keepdims=True), (tq, HEAD_DIM))
            w2 = jnp.broadcast_to(lam / jnp.sum(run_sum[1], axis=-1, keepdims=True), (tq, HEAD_DIM))
            o = None
            for c in range(nk // DIFF_KEY_CHUNK):
                c0 = c * DIFF_KEY_CHUNK
                p = jnp.concatenate(
                    [(s_old[0, :, c0 + j * HEAD_DIM:c0 + (j + 1) * HEAD_DIM] * w1
                      - s_old[1, :, c0 + j * HEAD_DIM:c0 + (j + 1) * HEAD_DIM] * w2).astype(BF16)
                     for j in range(slabs)], axis=-1)
                part = jnp.dot(p, v_scr[c0:c0 + DIFF_KEY_CHUNK, :], preferred_element_type=F32)
                o = part if o is None else o + part
            o = o * lax.rsqrt(jnp.mean(o * o, axis=-1, keepdims=True) + LN_EPS)
            o = o * sub_ref[...] * (1.0 - lam_init)
            o_ref[0] = (o * _silu(g_ref[0].astype(F32))).astype(o_ref.dtype)

    even = (s_even, m_even)
    odd = (s_odd, m_odd)

    @pl.when(t == 0)
    def _first():
        prep()
        step(even, None)

    @pl.when(jnp.logical_and(t > 0, jnp.logical_and(t < nq, t % 2 == 1)))
    def _odd():
        step(odd, even)

    @pl.when(jnp.logical_and(t > 0, jnp.logical_and(t < nq, t % 2 == 0)))
    def _even():
        step(even, odd)

    @pl.when(t == nq)
    def _last():
        step(None, odd if nq % 2 == 0 else even)


def diff_attention_latent(p3, diff_lam, diff_subln, layer_idx, ctx_k, ctx_v):
    b, seq, _ = p3.shape
    ctx_len = ctx_k.shape[2]
    tq = DIFF_TQ
    nq = seq // tq
    nk = seq + ctx_len
    lam_init = 0.8 - 0.6 * math.exp(-0.3 * layer_idx)
    q0, k0, v0, g0 = 2 * HEADS, 3 * HEADS, 4 * HEADS, 5 * HEADS
    const = lambda i, h, t: (0, 0)
    ctx_spec = pl.BlockSpec((1, 1, ctx_len, HEAD_DIM), lambda i, h, t: (i, h, 0, 0))
    tab_spec = pl.BlockSpec((seq, HEAD_DIM), const, pipeline_mode=pl.Buffered(1))
    score_scr = pltpu.VMEM((2, tq, nk), F32)
    max_scr = pltpu.VMEM((2, tq, HEAD_DIM), F32)
    return pl.pallas_call(
        functools.partial(_diff_latent_kernel, seq=seq, tq=tq, lam_init=lam_init),
        out_shape=jax.ShapeDtypeStruct((b, seq, HEADS * HEAD_DIM), BF16),
        grid=(b, HEADS, nq + 1),
        in_specs=[pl.BlockSpec((1, tq, HEAD_DIM), lambda i, h, t: (i, jnp.minimum(t, nq - 1), q0 + h)),
                  pl.BlockSpec((1, seq, HEAD_DIM), lambda i, h, t: (i, 0, k0 + h)),
                  pl.BlockSpec((1, seq, HEAD_DIM), lambda i, h, t: (i, 0, v0 + h)),
                  pl.BlockSpec((1, tq, HEAD_DIM), lambda i, h, t: (i, jnp.maximum(t - 1, 0), g0 + h)),
                  ctx_spec, ctx_spec, tab_spec, tab_spec, tab_spec,
                  pl.BlockSpec(diff_lam.shape, const),
                  pl.BlockSpec((1, HEAD_DIM), const)],
        out_specs=pl.BlockSpec((1, tq, HEAD_DIM), lambda i, h, t: (i, jnp.maximum(t - 1, 0), h)),
        scratch_shapes=[pltpu.VMEM((HEAD_DIM, nk), BF16), pltpu.VMEM((nk, HEAD_DIM), BF16),
                        score_scr, score_scr, max_scr, max_scr],
        compiler_params=_params("arbitrary", "arbitrary", "arbitrary"),
        name="diff_attention_latent",
    )(p3, p3, p3, p3, ctx_k, ctx_v, *[jnp.asarray(tab) for tab in _rope_tables()],
      diff_lam, diff_subln.reshape(1, HEAD_DIM))


def _diff_context_kernel(q_ref, k_ref, v_ref, g_ref, lam_ref, sub_ref, o_ref, kc_ref, vc_ref, *, lam_init):
    lam = _diff_lambda(lam_ref, lam_init)
    for h in range(HEADS):
        cols = slice(h * HEAD_DIM, (h + 1) * HEAD_DIM)
        k = k_ref[0, :, cols]
        v = v_ref[0, :, cols]
        kc_ref[0, h] = k
        vc_ref[0, h] = v
        q1, q2 = _split_components(q_ref[0, :, cols] * (QK_HALF ** -0.5 * LOG2E))
        kb = k.astype(BF16)
        s1 = _dot_nt(q1, kb)
        s2 = _dot_nt(q2, kb)
        e1 = jnp.exp2(s1 - jnp.max(s1, axis=-1, keepdims=True))
        e2 = jnp.exp2(s2 - jnp.max(s2, axis=-1, keepdims=True))
        y = _diff_combine(e1, e2, lam, v.astype(BF16), sub_ref[...], lam_init, g_ref[0, :, cols])
        o_ref[0, :, cols] = y.astype(o_ref.dtype)


def diff_attention_context(p3, diff_lam, diff_subln, layer_idx):
    b, seq, _ = p3.shape
    width = HEADS * HEAD_DIM
    lam_init = 0.8 - 0.6 * math.exp(-0.3 * layer_idx)
    blk = lambda c: pl.BlockSpec((1, seq, width), lambda i: (i, 0, c))
    cache_spec = pl.BlockSpec((1, HEADS, seq, HEAD_DIM), lambda i: (i, 0, 0, 0))
    return pl.pallas_call(
        functools.partial(_diff_context_kernel, lam_init=lam_init),
        out_shape=[jax.ShapeDtypeStruct((b, seq, width), BF16),
                   jax.ShapeDtypeStruct((b, HEADS, seq, HEAD_DIM), F32),
                   jax.ShapeDtypeStruct((b, HEADS, seq, HEAD_DIM), F32)],
        grid=(b,),
        in_specs=[blk(2), blk(3), blk(4), blk(5),
                  pl.BlockSpec(diff_lam.shape, lambda i: (0, 0)),
                  pl.BlockSpec((1, HEAD_DIM), lambda i: (0, 0))],
        out_specs=[pl.BlockSpec((1, seq, width), lambda i: (i, 0, 0)), cache_spec, cache_spec],
        compiler_params=_params("parallel"),
        name="diff_attention_context",
    )(p3, p3, p3, p3, diff_lam, diff_subln.reshape(1, HEAD_DIM))


def _ctx_attn_kernel(q_ref, k_ref, v_ref, g_ref, o_ref, kc_ref, vc_ref):
    for h in range(HEADS):
        cols = slice(h * HEAD_DIM, (h + 1) * HEAD_DIM)
        k = k_ref[0, :, cols]
        v = v_ref[0, :, cols]
        kc_ref[0, h] = k
        vc_ref[0, h] = v
        q = (q_ref[0, :, cols] * (HEAD_DIM ** -0.5)).astype(BF16)
        s = _dot_nt(q, k.astype(BF16))
        e = jnp.exp(s - jnp.max(s, axis=-1, keepdims=True))
        o = jnp.dot(e.astype(BF16), v.astype(BF16), preferred_element_type=F32)
        o = o / jnp.sum(e, axis=-1, keepdims=True)
        o_ref[0, :, cols] = (o * _silu(g_ref[0, :, cols])).astype(o_ref.dtype)


def context_attention(p3):
    b, seq, _ = p3.shape
    width = HEADS * HEAD_DIM
    blk = lambda c: pl.BlockSpec((1, seq, width), lambda i: (i, 0, c))
    cache_spec = pl.BlockSpec((1, HEADS, seq, HEAD_DIM), lambda i: (i, 0, 0, 0))
    return pl.pallas_call(
        _ctx_attn_kernel,
        out_shape=[jax.ShapeDtypeStruct((b, seq, width), BF16),
                   jax.ShapeDtypeStruct((b, HEADS, seq, HEAD_DIM), F32),
                   jax.ShapeDtypeStruct((b, HEADS, seq, HEAD_DIM), F32)],
        grid=(b,),
        in_specs=[blk(0), blk(1), blk(2), blk(3)],
        out_specs=[pl.BlockSpec((1, seq, width), lambda i: (i, 0, 0)), cache_spec, cache_spec],
        compiler_params=_params("parallel"),
        name="context_attention",
    )(p3, p3, p3, p3)


NA_TILE_Q = NA_Q_ROWS * GRID_W
NA_TILE_K = NA_K_ROWS * GRID_W
NA_TILES = GRID_ROWS // NA_Q_ROWS
NA_CASES = 3
NA_DR = 2 * NA_WIN_R - 1


def _na_key_row_start(tile):
    return int(np.clip(NA_Q_ROWS * tile - NA_WIN_R // 2, 0, GRID_ROWS - NA_K_ROWS))


def _na_bias_layout():
    dr = np.full((NA_CASES, NA_Q_ROWS, NA_K_ROWS), NA_DR, np.int32)
    for case, tile in enumerate((0, 1, NA_TILES - 1)):
        ks = _na_key_row_start(tile)
        for qr_local in range(NA_Q_ROWS):
            qr = NA_Q_ROWS * tile + qr_local
            rs = int(np.clip(qr - NA_WIN_R // 2, 0, GRID_ROWS - NA_WIN_R))
            for kr_local in range(NA_K_ROWS):
                kr = ks + kr_local
                if rs <= kr < rs + NA_WIN_R:
                    dr[case, qr_local, kr_local] = kr - qr + NA_WIN_R - 1
    pairs = sorted({(int(a), int(b)) for a, b in zip(dr[..., 0::2].ravel(), dr[..., 1::2].ravel())})
    index = {p: i for i, p in enumerate(pairs)}
    block = np.array([[[index[(int(dr[c, q, 2 * p]), int(dr[c, q, 2 * p + 1]))]
                        for p in range(NA_K_ROWS // 2)] for q in range(NA_Q_ROWS)] for c in range(NA_CASES)])
    return pairs, block


def _na_bias_blocks(rpb):
    qc = np.arange(GRID_W)[:, None]
    kc = np.arange(GRID_W)[None, :]
    cstart = np.clip(qc - NA_WIN_C // 2, 0, GRID_W - NA_WIN_C)
    col_valid = (kc >= cstart) & (kc < cstart + NA_WIN_C)
    dc = np.clip(kc - qc + NA_WIN_C - 1, 0, 2 * NA_WIN_C - 2)
    per_dr = jnp.where(jnp.asarray(col_valid), rpb[:, :, dc], NEG_INF)
    per_dr = jnp.concatenate([per_dr, jnp.full_like(per_dr[:, :1], NEG_INF)], axis=1)
    pairs, _ = _na_bias_layout()
    left = np.array([p[0] for p in pairs])
    right = np.array([p[1] for p in pairs])
    return jnp.concatenate([per_dr[:, left], per_dr[:, right]], axis=-1)


def _na_kernel(q_ref, k_ref, v_ref, g_ref, ck_ref, cv_ref, tab_ref, o_ref,
               bias_scr, ck_scr, cv_scr, sl_even, sl_odd, sc_even, sc_odd, m_even, m_odd):
    b = pl.program_id(1)
    t = pl.program_id(2)

    def key_rows(tile):
        key_row = jnp.clip(NA_Q_ROWS * tile - NA_WIN_R // 2, 0, GRID_ROWS - NA_K_ROWS)
        return pl.ds(pl.multiple_of(key_row * GRID_W, 256), NA_TILE_K)

    @pl.when(jnp.logical_and(b == 0, t == 0))
    def _build_bias():
        _, block = _na_bias_layout()
        for case in range(NA_CASES):
            for qr in range(NA_Q_ROWS):
                for p in range(NA_K_ROWS // 2):
                    bias_scr[case, qr * GRID_W:(qr + 1) * GRID_W, p * 128:(p + 1) * 128] = (
                        tab_ref[0, int(block[case, qr, p])])

    def scores(sl_scr, sc_scr, m_scr):
        tile = jnp.minimum(t, NA_TILES - 1)
        case = jnp.where(tile == 0, 0, jnp.where(tile == NA_TILES - 1, 2, 1))
        q = (q_ref[0].astype(F32) * (HEAD_DIM ** -0.5)).astype(BF16)
        s_loc = _dot_nt(q, k_ref[0, key_rows(tile), :]) + bias_scr[case]
        s_ctx = _dot_nt(q, ck_scr[...])
        sl_scr[...] = s_loc
        sc_scr[...] = s_ctx
        m_scr[...] = jnp.maximum(jnp.max(s_loc, axis=-1, keepdims=True), jnp.max(s_ctx, axis=-1, keepdims=True))

    def finish(sl_scr, sc_scr, m_scr):
        m = m_scr[...]
        e_loc = jnp.exp(sl_scr[...] - m)
        e_ctx = jnp.exp(sc_scr[...] - m)
        denom = jnp.sum(e_loc, axis=-1, keepdims=True) + jnp.sum(e_ctx, axis=-1, keepdims=True)
        o = jnp.dot(e_loc.astype(BF16), v_ref[0, key_rows(t - 1), :], preferred_element_type=F32)
        o = o + jnp.dot(e_ctx.astype(BF16), cv_scr[...], preferred_element_type=F32)
        o_ref[0] = (o / denom * _silu(g_ref[0].astype(F32))).astype(o_ref.dtype)

    @pl.when(t == 0)
    def _first():
        ck_scr[...] = ck_ref[0, 0].astype(BF16)
        cv_scr[...] = cv_ref[0, 0].astype(BF16)
        scores(sl_even, sc_even, m_even)

    @pl.when(jnp.logical_and(t > 0, jnp.logical_and(t < NA_TILES, t % 2 == 1)))
    def _odd():
        scores(sl_odd, sc_odd, m_odd)
        finish(sl_even, sc_even, m_even)

    @pl.when(jnp.logical_and(t > 0, jnp.logical_and(t < NA_TILES, t % 2 == 0)))
    def _even():
        scores(sl_even, sc_even, m_even)
        finish(sl_odd, sc_odd, m_odd)

    @pl.when(t == NA_TILES)
    def _last():
        if NA_TILES % 2 == 0:
            finish(sl_odd, sc_odd, m_odd)
        else:
            finish(sl_even, sc_even, m_even)


def neighborhood_attention(p3, ctx_k, ctx_v, rpb):
    b, seq, _ = p3.shape
    ctx_len = ctx_k.shape[2]
    tab = _na_bias_blocks(rpb)
    n_pairs = tab.shape[1]
    ctx_spec = pl.BlockSpec((1, 1, ctx_len, HEAD_DIM), lambda h, i, t: (i, h, 0, 0))
    q_tile = lambda h, i, t: (i, jnp.minimum(t, NA_TILES - 1), h)
    done_tile = lambda t: jnp.maximum(t - 1, 0)
    return pl.pallas_call(
        _na_kernel,
        out_shape=jax.ShapeDtypeStruct((b, seq, HEADS * HEAD_DIM), BF16),
        grid=(HEADS, b, NA_TILES + 1),
        in_specs=[pl.BlockSpec((1, NA_TILE_Q, HEAD_DIM), q_tile),
                  pl.BlockSpec((1, seq, HEAD_DIM), lambda h, i, t: (i, 0, HEADS + h)),
                  pl.BlockSpec((1, seq, HEAD_DIM), lambda h, i, t: (i, 0, 2 * HEADS + h)),
                  pl.BlockSpec((1, NA_TILE_Q, HEAD_DIM), lambda h, i, t: (i, done_tile(t), 3 * HEADS + h)),
                  ctx_spec, ctx_spec,
                  pl.BlockSpec((1, n_pairs, GRID_W, 2 * GRID_W), lambda h, i, t: (h, 0, 0, 0))],
        out_specs=pl.BlockSpec((1, NA_TILE_Q, HEAD_DIM), lambda h, i, t: (i, done_tile(t), h)),
        scratch_shapes=[pltpu.VMEM((NA_CASES, NA_TILE_Q, NA_TILE_K), F32),
                        pltpu.VMEM((ctx_len, HEAD_DIM), BF16), pltpu.VMEM((ctx_len, HEAD_DIM), BF16),
                        pltpu.VMEM((NA_TILE_Q, NA_TILE_K), F32), pltpu.VMEM((NA_TILE_Q, NA_TILE_K), F32),
                        pltpu.VMEM((NA_TILE_Q, ctx_len), F32), pltpu.VMEM((NA_TILE_Q, ctx_len), F32),
                        pltpu.VMEM((NA_TILE_Q, 1), F32), pltpu.VMEM((NA_TILE_Q, 1), F32)],
        compiler_params=_params("arbitrary", "arbitrary", "arbitrary"),
        name="neighborhood_attention",
    )(p3, p3, p3, p3, ctx_k, ctx_v, tab)


def _sgu_kernel(u_ref, v_ref, g_ref, ln_ref, w_ref, b_ref, o_ref):
    v = v_ref[...].astype(F32)
    mu = jnp.mean(v, axis=-1, keepdims=True)
    vc = v - mu
    var = jnp.mean(vc * vc, axis=-1, keepdims=True)
    vn = (vc * lax.rsqrt(var + LN_EPS) * ln_ref[...]).astype(BF16)
    groups = w_ref.shape[0]
    for n in range(v.shape[0] // SGU_CHUNK):
        rows = slice(n * SGU_CHUNK, (n + 1) * SGU_CHUNK)
        for g in range(groups):
            cols = slice(g * SGU_GROUP_DIM, (g + 1) * SGU_GROUP_DIM)
            s = jnp.dot(w_ref[g], vn[rows, cols], preferred_element_type=F32) + b_ref[:, g:g + 1]
            y = u_ref[rows, cols].astype(F32) * s * _silu(g_ref[rows, cols].astype(F32))
            o_ref[rows, cols] = y.astype(o_ref.dtype)


def spatial_gating(p2, sgu_ln, w_s_bf16, b_s):
    m = p2.shape[0]
    groups = w_s_bf16.shape[0]
    width = groups * SGU_GROUP_DIM
    blk = lambda c: pl.BlockSpec((SGU_TM, width), lambda i: (i, c))
    return pl.pallas_call(
        _sgu_kernel,
        out_shape=jax.ShapeDtypeStruct((m, width), BF16),
        grid=(m // SGU_TM,),
        in_specs=[blk(4), blk(5), blk(6),
                  pl.BlockSpec((1, width), lambda i: (0, 0)),
                  pl.BlockSpec((groups, SGU_CHUNK, SGU_CHUNK), lambda i: (0, 0, 0)),
                  pl.BlockSpec((SGU_CHUNK, groups), lambda i: (0, 0))],
        out_specs=pl.BlockSpec((SGU_TM, width), lambda i: (i, 0)),
        compiler_params=_params("parallel"),
        name="spatial_gating",
    )(p2, p2, p2, sgu_ln.reshape(1, width), w_s_bf16, jnp.transpose(b_s))


def kernel(x_prompt, x_sample, cache_k_l0, cache_v_l0, cache_k_l1, cache_v_l1, cache_k_l2, cache_v_l2, cache_k_l3, cache_v_l3, c, c_ctx, w_mod_0, b_mod_0, w_in_0, w_out_0, ln_g_0, ln_b_0, pool_w_0, pool_scale_0, diff_lam_0, diff_subln_0, w_mod_1, b_mod_1, w_in_1, w_out_1, ln_g_1, ln_b_1, rpb_1, sgu_ln_1, sgu_w_1, sgu_b_1, w_mod_2, b_mod_2, w_in_2, w_out_2, ln_g_2, ln_b_2, pool_w_2, pool_scale_2, diff_lam_2, diff_subln_2, w_mod_3, b_mod_3, w_in_3, w_out_3, ln_g_3, ln_b_3, rpb_3, sgu_ln_3, sgu_w_3, sgu_b_3):
    cache_k = [cache_k_l0, cache_k_l1, cache_k_l2, cache_k_l3]
    cache_v = [cache_v_l0, cache_v_l1, cache_v_l2, cache_v_l3]
    w_mod = [w_mod_0, w_mod_1, w_mod_2, w_mod_3]
    b_mod = [b_mod_0, b_mod_1, b_mod_2, b_mod_3]
    w_in = [w_in_0, w_in_1, w_in_2, w_in_3]
    w_out = [w_out_0, w_out_1, w_out_2, w_out_3]
    ln_g = [ln_g_0, ln_g_1, ln_g_2, ln_g_3]
    ln_b = [ln_b_0, ln_b_1, ln_b_2, ln_b_3]
    even_p = {0: (pool_w_0, pool_scale_0, diff_lam_0, diff_subln_0),
              2: (pool_w_2, pool_scale_2, diff_lam_2, diff_subln_2)}
    odd_p = {1: (rpb_1, sgu_ln_1, sgu_w_1, sgu_b_1),
             3: (rpb_3, sgu_ln_3, sgu_w_3, sgu_b_3)}

    bp, lp, _ = x_prompt.shape
    bs, ls, _ = x_sample.shape
    xp = x_prompt.reshape(bp * lp, D_MODEL)
    xs = x_sample.reshape(bs * ls, D_MODEL)
    cond = jnp.concatenate([c, c_ctx[None, :], jnp.zeros((COND_ROWS - bs - 1, D_MODEL), F32)], axis=0)

    new_k, new_v = [], []
    for l in range(DEPTH):
        mod3 = ada_params(cond, w_mod[l], b_mod[l]).reshape(COND_ROWS, 1, 3 * D_MODEL)
        w_in_l = w_in[l].astype(BF16)
        n_in = w_in_l.shape[1]
        pp = in_projection(xp, mod3, w_in_l, None, F32)
        ps = in_projection(xs, mod3, w_in_l, ls, BF16)
        pp3 = pp.reshape(bp, lp, n_in)
        ps3 = ps.reshape(bs, ls, n_in)
        if l % 2 == 0:
            pool_w, pool_scale, diff_lam, diff_subln = even_p[l]
            pool_w = pool_w.astype(BF16)
            ya_p = pool_mixer(pp3, pool_w, pool_scale)
            ya_s = pool_mixer(ps3, pool_w, pool_scale)
            yb_p, nk, nv = diff_attention_context(pp3, diff_lam, diff_subln, l)
            yb_s = diff_attention_latent(ps3, diff_lam, diff_subln, l, cache_k[l], cache_v[l])
        else:
            rpb, sgu_ln, sgu_w, sgu_b = odd_p[l]
            sgu_w = sgu_w.astype(BF16)
            ya_p, nk, nv = context_attention(pp3)
            ya_s = neighborhood_attention(ps3, cache_k[l], cache_v[l], rpb)
            yb_p = spatial_gating(pp, sgu_ln, sgu_w, sgu_b)
            yb_s = spatial_gating(ps, sgu_ln, sgu_w, sgu_b)
        half = ya_p.shape[-1]
        w_out_l = w_out[l].astype(BF16)
        xp = out_projection(ya_p.reshape(bp * lp, half), yb_p.reshape(bp * lp, half), xp, mod3,
                            w_out_l, ln_g[l], ln_b[l], None)
        xs = out_projection(ya_s.reshape(bs * ls, half), yb_s.reshape(bs * ls, half), xs, mod3,
                            w_out_l, ln_g[l], ln_b[l], ls)
        new_k.append(nk)
        new_v.append(nv)

    return (xp.reshape(bp, lp, D_MODEL), xs.reshape(bs, ls, D_MODEL),
            new_k[0], new_v[0], new_k[1], new_v[1], new_k[2], new_v[2], new_k[3], new_v[3])
```

```python
import functools
import math

import numpy as np
import jax
import jax.numpy as jnp
from jax import lax
from jax.experimental import pallas as pl
from jax.experimental.pallas import tpu as pltpu

F32 = jnp.float32
BF16 = jnp.bfloat16

D_MODEL = 2048
DEPTH = 4
GRID_W = 64
GRID_ROWS = 64
HEADS = 8
HEAD_DIM = 128
QK_HALF = 64
POOL_WINDOWS = (2, 4, 8, 16)
POOL_GROUP_DIM = 256
SGU_CHUNK = 128
SGU_GROUP_DIM = 256
NA_WIN_R = 8
NA_WIN_C = 16
NA_Q_ROWS = 8
NA_K_ROWS = 16
ROPE_BASE = 10000.0
LN_EPS = 1e-5
NEG_INF = -1e30
DEEPNORM_ALPHA = (2 * DEPTH) ** 0.25
LOG2E = math.log2(math.e)

VMEM_LIMIT_BYTES = 56 * 1024 * 1024
COND_ROWS = 8
CTX_ROW = 4

PROJ_TM = 1024
PROJ_TN = 1024
OUT_TM = 512
ADA_TN = 512
POOL_ROWS = 256
POOL_HALO = 16
SGU_TM = 512
DIFF_TQ = 512
PREP_ROWS = 512


def _params(*sem):
    return pltpu.CompilerParams(dimension_semantics=sem, vmem_limit_bytes=VMEM_LIMIT_BYTES)


def _silu(x):
    return x * jax.nn.sigmoid(x)


def _dot_nt(a, b):
    return lax.dot_general(a, b, (((1,), (1,)), ((), ())), preferred_element_type=F32)


def _ada_kernel(c_ref, w_ref, b_ref, o_ref):
    a = _silu(c_ref[...]).astype(BF16)
    o_ref[...] = jnp.dot(a, w_ref[...].astype(BF16), preferred_element_type=F32) + b_ref[...]


def ada_params(cond, w_mod, b_mod):
    n = w_mod.shape[1]
    return pl.pallas_call(
        _ada_kernel,
        out_shape=jax.ShapeDtypeStruct((COND_ROWS, n), F32),
        grid=(n // ADA_TN,),
        in_specs=[pl.BlockSpec((COND_ROWS, D_MODEL), lambda j: (0, 0)),
                  pl.BlockSpec((D_MODEL, ADA_TN), lambda j: (0, j)),
                  pl.BlockSpec((1, ADA_TN), lambda j: (0, j))],
        out_specs=pl.BlockSpec((COND_ROWS, ADA_TN), lambda j: (0, j)),
        compiler_params=_params("parallel"),
        name="ada_params",
    )(cond, w_mod, b_mod.reshape(1, n))


def _inproj_kernel(x_ref, sh_ref, sc_ref, w_ref, o_ref, h_scr):
    @pl.when(pl.program_id(1) == 0)
    def _():
        h = x_ref[...] * (1.0 + sc_ref[0]) + sh_ref[0]
        h_scr[...] = h.astype(BF16)

    o_ref[...] = jnp.dot(h_scr[...], w_ref[...], preferred_element_type=F32).astype(o_ref.dtype)


def in_projection(x2d, mod3, w_bf16, rows_per_cond, out_dtype):
    m = x2d.shape[0]
    n = w_bf16.shape[1]
    if rows_per_cond is None:
        cond_row = lambda i: CTX_ROW
    else:
        blocks_per_cond = rows_per_cond // PROJ_TM
        cond_row = lambda i: i // blocks_per_cond
    return pl.pallas_call(
        _inproj_kernel,
        out_shape=jax.ShapeDtypeStruct((m, n), out_dtype),
        grid=(m // PROJ_TM, n // PROJ_TN),
        in_specs=[pl.BlockSpec((PROJ_TM, D_MODEL), lambda i, j: (i, 0)),
                  pl.BlockSpec((1, 1, D_MODEL), lambda i, j: (cond_row(i), 0, 0)),
                  pl.BlockSpec((1, 1, D_MODEL), lambda i, j: (cond_row(i), 0, 1)),
                  pl.BlockSpec((D_MODEL, PROJ_TN), lambda i, j: (0, j))],
        out_specs=pl.BlockSpec((PROJ_TM, PROJ_TN), lambda i, j: (i, j)),
        scratch_shapes=[pltpu.VMEM((PROJ_TM, D_MODEL), BF16)],
        compiler_params=_params("parallel", "arbitrary"),
        name="in_projection",
    )(x2d, mod3, mod3, w_bf16)


def _outproj_kernel(ya_ref, yb_ref, x_ref, g_ref, w_ref, lng_ref, lnb_ref, o_ref):
    half = ya_ref.shape[1]
    acc = jnp.dot(ya_ref[...], w_ref[0:half, :], preferred_element_type=F32)
    acc = acc + jnp.dot(yb_ref[...], w_ref[half:, :], preferred_element_type=F32)
    z = DEEPNORM_ALPHA * x_ref[...] + g_ref[0] * acc
    mu = jnp.mean(z, axis=-1, keepdims=True)
    zc = z - mu
    var = jnp.mean(zc * zc, axis=-1, keepdims=True)
    o_ref[...] = zc * lax.rsqrt(var + LN_EPS) * lng_ref[...] + lnb_ref[...]


def out_projection(ya, yb, x2d, mod3, w_bf16, ln_g, ln_b, rows_per_cond):
    m = x2d.shape[0]
    half = ya.shape[1]
    if rows_per_cond is None:
        cond_row = lambda i: CTX_ROW
    else:
        blocks_per_cond = rows_per_cond // OUT_TM
        cond_row = lambda i: i // blocks_per_cond
    return pl.pallas_call(
        _outproj_kernel,
        out_shape=jax.ShapeDtypeStruct((m, D_MODEL), F32),
        grid=(m // OUT_TM,),
        in_specs=[pl.BlockSpec((OUT_TM, half), lambda i: (i, 0)),
                  pl.BlockSpec((OUT_TM, half), lambda i: (i, 0)),
                  pl.BlockSpec((OUT_TM, D_MODEL), lambda i: (i, 0)),
                  pl.BlockSpec((1, 1, D_MODEL), lambda i: (cond_row(i), 0, 2)),
                  pl.BlockSpec((2 * half, D_MODEL), lambda i: (0, 0)),
                  pl.BlockSpec((1, D_MODEL), lambda i: (0, 0)),
                  pl.BlockSpec((1, D_MODEL), lambda i: (0, 0))],
        out_specs=pl.BlockSpec((OUT_TM, D_MODEL), lambda i: (i, 0)),
        compiler_params=_params("parallel"),
        name="out_projection",
    )(ya, yb, x2d, mod3, w_bf16, ln_g.reshape(1, D_MODEL), ln_b.reshape(1, D_MODEL))


def _pool_kernel(a_ref, g_ref, w_ref, s_ref, o_ref, ext_scr, *, seq):
    n_chunks = seq // POOL_ROWS
    group = pl.program_id(1)

    def run(half):
        def chunk(ci, carry):
            r0 = pl.multiple_of(ci * POOL_ROWS, POOL_ROWS)
            main = a_ref[0, pl.ds(r0, POOL_ROWS), :].astype(F32)
            if n_chunks > 1:
                p0 = pl.multiple_of(jnp.maximum(r0 - POOL_HALO, 0), POOL_HALO)
                n0 = pl.multiple_of(jnp.minimum(r0 + POOL_ROWS, seq - POOL_HALO), POOL_HALO)
                prev = a_ref[0, pl.ds(p0, POOL_HALO), :].astype(F32)
                nxt = a_ref[0, pl.ds(n0, POOL_HALO), :].astype(F32)
                prev = jnp.where(ci > 0, prev, 0.0)
                nxt = jnp.where(ci < n_chunks - 1, nxt, 0.0)
            else:
                prev = jnp.zeros((POOL_HALO, POOL_GROUP_DIM), F32)
                nxt = prev
            ext_scr[0:POOL_HALO, :] = prev
            ext_scr[POOL_HALO:POOL_HALO + POOL_ROWS, :] = main
            ext_scr[POOL_HALO + POOL_ROWS:, :] = nxt
            win = ext_scr[POOL_HALO - half:POOL_HALO - half + POOL_ROWS, :]
            for j in range(-half + 1, half):
                win = win + ext_scr[POOL_HALO + j:POOL_HALO + j + POOL_ROWS, :]
            t = r0 + lax.broadcasted_iota(jnp.int32, (POOL_ROWS, 1), 0)
            cnt = (jnp.minimum(t + half, seq) - jnp.maximum(t - half, 0)).astype(F32)
            pooled = win / cnt - main
            mixed = jnp.dot(pooled.astype(BF16), w_ref[0], preferred_element_type=F32)
            gate = g_ref[0, pl.ds(r0, POOL_ROWS), :].astype(F32)
            y = mixed * s_ref[...] * _silu(gate)
            o_ref[0, pl.ds(r0, POOL_ROWS), :] = y.astype(o_ref.dtype)
            return carry

        lax.fori_loop(0, n_chunks, chunk, 0)

    for gi, window in enumerate(POOL_WINDOWS):
        pl.when(group == gi)(functools.partial(run, window // 2))


def pool_mixer(p3, w_pool_bf16, pool_scale):
    b, seq, _ = p3.shape
    groups = len(POOL_WINDOWS)
    return pl.pallas_call(
        functools.partial(_pool_kernel, seq=seq),
        out_shape=jax.ShapeDtypeStruct((b, seq, groups * POOL_GROUP_DIM), BF16),
        grid=(b, groups),
        in_specs=[pl.BlockSpec((1, seq, POOL_GROUP_DIM), lambda i, g: (i, 0, g)),
                  pl.BlockSpec((1, seq, POOL_GROUP_DIM), lambda i, g: (i, 0, groups + g)),
                  pl.BlockSpec((1, POOL_GROUP_DIM, POOL_GROUP_DIM), lambda i, g: (g, 0, 0)),
                  pl.BlockSpec((1, POOL_GROUP_DIM), lambda i, g: (0, g))],
        out_specs=pl.BlockSpec((1, seq, POOL_GROUP_DIM), lambda i, g: (i, 0, g)),
        scratch_shapes=[pltpu.VMEM((POOL_ROWS + 2 * POOL_HALO, POOL_GROUP_DIM), F32)],
        compiler_params=_params("parallel", "parallel"),
        name="pool_mixer",
    )(p3, p3, w_pool_bf16, pool_scale.reshape(1, groups * POOL_GROUP_DIM))


def _rope_tables():
    t = np.arange(GRID_ROWS * GRID_W)
    row = (t // GRID_W).astype(np.float64)[:, None]
    col = (t % GRID_W).astype(np.float64)[:, None]
    axis_dim = QK_HALF // 2
    inv = 1.0 / (ROPE_BASE ** (np.arange(0, axis_dim, 2, dtype=np.float64) / axis_dim))
    lane = np.arange(HEAD_DIM)
    within = lane % QK_HALF
    pos = np.where((within < axis_dim)[None, :], row, col)
    ang = pos * inv[within % (axis_dim // 2)][None, :]
    is_a = ((within % axis_dim) < axis_dim // 2)[None, :]
    cos = np.cos(ang)
    sin = np.sin(ang)
    return (cos.astype(np.float32),
            np.where(is_a, -sin, 0.0).astype(np.float32),
            np.where(is_a, 0.0, sin).astype(np.float32))


def _rotate_lanes(x, cos, sin_a, sin_b):
    shift = QK_HALF // 4
    return (x * cos + pltpu.roll(x, HEAD_DIM - shift, 1) * sin_a
            + pltpu.roll(x, shift, 1) * sin_b)


def _diff_lambda(lam_ref, lam_init):
    dl = lam_ref[...]
    return (jnp.exp(jnp.sum(dl[0:1] * dl[1:2], axis=-1, keepdims=True))
            - jnp.exp(jnp.sum(dl[2:3] * dl[3:4], axis=-1, keepdims=True)) + lam_init)


def _split_components(q):
    lane = lax.broadcasted_iota(jnp.int32, (1, HEAD_DIM), 1)
    return (jnp.where(lane < QK_HALF, q, 0.0).astype(BF16),
            jnp.where(lane >= QK_HALF, q, 0.0).astype(BF16))


def _diff_combine(e1, e2, lam, v, sub, lam_init, gate):
    w1 = 1.0 / jnp.sum(e1, axis=-1, keepdims=True)
    w2 = lam / jnp.sum(e2, axis=-1, keepdims=True)
    p = (e1 * w1 - e2 * w2).astype(BF16)
    o = jnp.dot(p, v, preferred_element_type=F32)
    o = o * lax.rsqrt(jnp.mean(o * o, axis=-1, keepdims=True) + LN_EPS)
    o = o * sub * (1.0 - lam_init)
    return o * _silu(gate)


def _diff_latent_kernel(q_ref, k_ref, v_ref, g_ref, ck_ref, cv_ref, cos_ref, sa_ref, sb_ref,
                        lam_ref, sub_ref, o_ref, kt_scr, v_scr, *, seq, tq, lam_init):
    t = pl.program_id(2)

    @pl.when(t == 0)
    def _prep():
        lane = lax.broadcasted_iota(jnp.int32, (1, HEAD_DIM), 1)
        ones_col = jnp.broadcast_to(jnp.where(lane == 0, 1.0, 0.0).astype(BF16), (PREP_ROWS, HEAD_DIM))

        def chunk(ci, carry):
            rows = pl.ds(pl.multiple_of(ci * PREP_ROWS, PREP_ROWS), PREP_ROWS)
            k = _rotate_lanes(k_ref[0, rows, :].astype(F32), cos_ref[rows, :], sa_ref[rows, :], sb_ref[rows, :])
            kt_scr[:, rows] = jnp.transpose(k).astype(BF16)
            v_scr[rows, 0:HEAD_DIM] = v_ref[0, rows, :].astype(BF16)
            v_scr[rows, HEAD_DIM:] = ones_col
            return carry

        lax.fori_loop(0, seq // PREP_ROWS, chunk, 0)
        kt_scr[:, seq:] = jnp.transpose(ck_ref[0, 0]).astype(BF16)
        v_scr[seq:, 0:HEAD_DIM] = cv_ref[0, 0].astype(BF16)
        v_scr[seq:, HEAD_DIM:] = ones_col[0:v_scr.shape[0] - seq]

    rows = pl.ds(pl.multiple_of(t * tq, tq), tq)
    q = _rotate_lanes(q_ref[0].astype(F32), cos_ref[rows, :], sa_ref[rows, :], sb_ref[rows, :])
    q1, q2 = _split_components(q * (QK_HALF ** -0.5 * LOG2E))
    keys_t = kt_scr[...]
    values = v_scr[...]
    s = [jnp.dot(qc, keys_t, preferred_element_type=F32) for qc in (q1, q2)]
    e = [jnp.exp2(sc - jnp.max(sc, axis=-1, keepdims=True)).astype(BF16) for sc in s]
    r = [jnp.dot(ec, values, preferred_element_type=F32) for ec in e]
    w1 = 1.0 / r[0][:, HEAD_DIM:HEAD_DIM + 1]
    w2 = _diff_lambda(lam_ref, lam_init) / r[1][:, HEAD_DIM:HEAD_DIM + 1]
    o = r[0][:, 0:HEAD_DIM] * w1 - r[1][:, 0:HEAD_DIM] * w2
    o = o * lax.rsqrt(jnp.mean(o * o, axis=-1, keepdims=True) + LN_EPS)
    o = o * sub_ref[...] * (1.0 - lam_init)
    o_ref[0] = (o * _silu(g_ref[0].astype(F32))).astype(o_ref.dtype)


def diff_attention_latent(p3, diff_lam, diff_subln, layer_idx, ctx_k, ctx_v):
    b, seq, _ = p3.shape
    ctx_len = ctx_k.shape[2]
    tq = DIFF_TQ
    nk = seq + ctx_len
    lam_init = 0.8 - 0.6 * math.exp(-0.3 * layer_idx)
    q0, k0, v0, g0 = 2 * HEADS, 3 * HEADS, 4 * HEADS, 5 * HEADS
    const = lambda i, h, t: (0, 0)
    ctx_spec = pl.BlockSpec((1, 1, ctx_len, HEAD_DIM), lambda i, h, t: (i, h, 0, 0))
    tab_spec = pl.BlockSpec((seq, HEAD_DIM), const, pipeline_mode=pl.Buffered(1))
    return pl.pallas_call(
        functools.partial(_diff_latent_kernel, seq=seq, tq=tq, lam_init=lam_init),
        out_shape=jax.ShapeDtypeStruct((b, seq, HEADS * HEAD_DIM), BF16),
        grid=(b, HEADS, seq // tq),
        in_specs=[pl.BlockSpec((1, tq, HEAD_DIM), lambda i, h, t: (i, t, q0 + h)),
                  pl.BlockSpec((1, seq, HEAD_DIM), lambda i, h, t: (i, 0, k0 + h)),
                  pl.BlockSpec((1, seq, HEAD_DIM), lambda i, h, t: (i, 0, v0 + h)),
                  pl.BlockSpec((1, tq, HEAD_DIM), lambda i, h, t: (i, t, g0 + h)),
                  ctx_spec, ctx_spec, tab_spec, tab_spec, tab_spec,
                  pl.BlockSpec(diff_lam.shape, const),
                  pl.BlockSpec((1, HEAD_DIM), const)],
        out_specs=pl.BlockSpec((1, tq, HEAD_DIM), lambda i, h, t: (i, t, h)),
        scratch_shapes=[pltpu.VMEM((HEAD_DIM, nk), BF16), pltpu.VMEM((nk, 2 * HEAD_DIM), BF16)],
        compiler_params=_params("parallel", "parallel", "arbitrary"),
        name="diff_attention_latent",
    )(p3, p3, p3, p3, ctx_k, ctx_v, *[jnp.asarray(tab) for tab in _rope_tables()],
      diff_lam, diff_subln.reshape(1, HEAD_DIM))


def _diff_context_kernel(q_ref, k_ref, v_ref, g_ref, lam_ref, sub_ref, o_ref, kc_ref, vc_ref, *, lam_init):
    lam = _diff_lambda(lam_ref, lam_init)
    for h in range(HEADS):
        cols = slice(h * HEAD_DIM, (h + 1) * HEAD_DIM)
        k = k_ref[0, :, cols]
        v = v_ref[0, :, cols]
        kc_ref[0, h] = k
        vc_ref[0, h] = v
        q1, q2 = _split_components(q_ref[0, :, cols] * (QK_HALF ** -0.5 * LOG2E))
        kb = k.astype(BF16)
        s1 = _dot_nt(q1, kb)
        s2 = _dot_nt(q2, kb)
        e1 = jnp.exp2(s1 - jnp.max(s1, axis=-1, keepdims=True))
        e2 = jnp.exp2(s2 - jnp.max(s2, axis=-1, keepdims=True))
        y = _diff_combine(e1, e2, lam, v.astype(BF16), sub_ref[...], lam_init, g_ref[0, :, cols])
        o_ref[0, :, cols] = y.astype(o_ref.dtype)


def diff_attention_context(p3, diff_lam, diff_subln, layer_idx):
    b, seq, _ = p3.shape
    width = HEADS * HEAD_DIM
    lam_init = 0.8 - 0.6 * math.exp(-0.3 * layer_idx)
    blk = lambda c: pl.BlockSpec((1, seq, width), lambda i: (i, 0, c))
    cache_spec = pl.BlockSpec((1, HEADS, seq, HEAD_DIM), lambda i: (i, 0, 0, 0))
    return pl.pallas_call(
        functools.partial(_diff_context_kernel, lam_init=lam_init),
        out_shape=[jax.ShapeDtypeStruct((b, seq, width), BF16),
                   jax.ShapeDtypeStruct((b, HEADS, seq, HEAD_DIM), F32),
                   jax.ShapeDtypeStruct((b, HEADS, seq, HEAD_DIM), F32)],
        grid=(b,),
        in_specs=[blk(2), blk(3), blk(4), blk(5),
                  pl.BlockSpec(diff_lam.shape, lambda i: (0, 0)),
                  pl.BlockSpec((1, HEAD_DIM), lambda i: (0, 0))],
        out_specs=[pl.BlockSpec((1, seq, width), lambda i: (i, 0, 0)), cache_spec, cache_spec],
        compiler_params=_params("parallel"),
        name="diff_attention_context",
    )(p3, p3, p3, p3, diff_lam, diff_subln.reshape(1, HEAD_DIM))


def _ctx_attn_kernel(q_ref, k_ref, v_ref, g_ref, o_ref, kc_ref, vc_ref):
    for h in range(HEADS):
        cols = slice(h * HEAD_DIM, (h + 1) * HEAD_DIM)
        k = k_ref[0, :, cols]
        v = v_ref[0, :, cols]
        kc_ref[0, h] = k
        vc_ref[0, h] = v
        q = (q_ref[0, :, cols] * (HEAD_DIM ** -0.5)).astype(BF16)
        s = _dot_nt(q, k.astype(BF16))
        e = jnp.exp(s - jnp.max(s, axis=-1, keepdims=True))
        o = jnp.dot(e.astype(BF16), v.astype(BF16), preferred_element_type=F32)
        o = o / jnp.sum(e, axis=-1, keepdims=True)
        o_ref[0, :, cols] = (o * _silu(g_ref[0, :, cols])).astype(o_ref.dtype)


def context_attention(p3):
    b, seq, _ = p3.shape
    width = HEADS * HEAD_DIM
    blk = lambda c: pl.BlockSpec((1, seq, width), lambda i: (i, 0, c))
    cache_spec = pl.BlockSpec((1, HEADS, seq, HEAD_DIM), lambda i: (i, 0, 0, 0))
    return pl.pallas_call(
        _ctx_attn_kernel,
        out_shape=[jax.ShapeDtypeStruct((b, seq, width), BF16),
                   jax.ShapeDtypeStruct((b, HEADS, seq, HEAD_DIM), F32),
                   jax.ShapeDtypeStruct((b, HEADS, seq, HEAD_DIM), F32)],
        grid=(b,),
        in_specs=[blk(0), blk(1), blk(2), blk(3)],
        out_specs=[pl.BlockSpec((1, seq, width), lambda i: (i, 0, 0)), cache_spec, cache_spec],
        compiler_params=_params("parallel"),
        name="context_attention",
    )(p3, p3, p3, p3)


NA_TILE_Q = NA_Q_ROWS * GRID_W
NA_TILE_K = NA_K_ROWS * GRID_W
NA_TILES = GRID_ROWS // NA_Q_ROWS
NA_CASES = 3
NA_DR = 2 * NA_WIN_R - 1


def _na_key_row_start(tile):
    return int(np.clip(NA_Q_ROWS * tile - NA_WIN_R // 2, 0, GRID_ROWS - NA_K_ROWS))


def _na_bias_layout():
    dr = np.full((NA_CASES, NA_Q_ROWS, NA_K_ROWS), NA_DR, np.int32)
    for case, tile in enumerate((0, 1, NA_TILES - 1)):
        ks = _na_key_row_start(tile)
        for qr_local in range(NA_Q_ROWS):
            qr = NA_Q_ROWS * tile + qr_local
            rs = int(np.clip(qr - NA_WIN_R // 2, 0, GRID_ROWS - NA_WIN_R))
            for kr_local in range(NA_K_ROWS):
                kr = ks + kr_local
                if rs <= kr < rs + NA_WIN_R:
                    dr[case, qr_local, kr_local] = kr - qr + NA_WIN_R - 1
    pairs = sorted({(int(a), int(b)) for a, b in zip(dr[..., 0::2].ravel(), dr[..., 1::2].ravel())})
    index = {p: i for i, p in enumerate(pairs)}
    block = np.array([[[index[(int(dr[c, q, 2 * p]), int(dr[c, q, 2 * p + 1]))]
                        for p in range(NA_K_ROWS // 2)] for q in range(NA_Q_ROWS)] for c in range(NA_CASES)])
    return pairs, block


def _na_bias_blocks(rpb):
    qc = np.arange(GRID_W)[:, None]
    kc = np.arange(GRID_W)[None, :]
    cstart = np.clip(qc - NA_WIN_C // 2, 0, GRID_W - NA_WIN_C)
    col_valid = (kc >= cstart) & (kc < cstart + NA_WIN_C)
    dc = np.clip(kc - qc + NA_WIN_C - 1, 0, 2 * NA_WIN_C - 2)
    per_dr = jnp.where(jnp.asarray(col_valid), rpb[:, :, dc], NEG_INF)
    per_dr = jnp.concatenate([per_dr, jnp.full_like(per_dr[:, :1], NEG_INF)], axis=1)
    pairs, _ = _na_bias_layout()
    left = np.array([p[0] for p in pairs])
    right = np.array([p[1] for p in pairs])
    return jnp.concatenate([per_dr[:, left], per_dr[:, right]], axis=-1)


def _na_kernel(q_ref, k_ref, v_ref, g_ref, ck_ref, cv_ref, tab_ref, o_ref,
               bias_scr, ck_scr, cv_scr, v_scr):
    b = pl.program_id(1)
    t = pl.program_id(2)
    seq = v_ref.shape[1]

    @pl.when(jnp.logical_and(b == 0, t == 0))
    def _build_bias():
        _, block = _na_bias_layout()
        for case in range(NA_CASES):
            for qr in range(NA_Q_ROWS):
                for p in range(NA_K_ROWS // 2):
                    bias_scr[case, qr * GRID_W:(qr + 1) * GRID_W, p * 128:(p + 1) * 128] = (
                        tab_ref[0, int(block[case, qr, p])] * LOG2E)

    @pl.when(t == 0)
    def _stage():
        lane = lax.broadcasted_iota(jnp.int32, (1, HEAD_DIM), 1)
        ones_col = jnp.broadcast_to(jnp.where(lane == 0, 1.0, 0.0).astype(BF16), (PREP_ROWS, HEAD_DIM))
        ck_scr[...] = ck_ref[0, 0].astype(BF16)
        cv_scr[:, 0:HEAD_DIM] = cv_ref[0, 0].astype(BF16)
        cv_scr[:, HEAD_DIM:] = ones_col[0:cv_scr.shape[0]]

        def chunk(ci, carry):
            rows = pl.ds(pl.multiple_of(ci * PREP_ROWS, PREP_ROWS), PREP_ROWS)
            v_scr[rows, 0:HEAD_DIM] = v_ref[0, rows, :]
            v_scr[rows, HEAD_DIM:] = ones_col
            return carry

        lax.fori_loop(0, seq // PREP_ROWS, chunk, 0)

    case = jnp.where(t == 0, 0, jnp.where(t == NA_TILES - 1, 2, 1))
    key_row = jnp.clip(NA_Q_ROWS * t - NA_WIN_R // 2, 0, GRID_ROWS - NA_K_ROWS)
    keys = pl.ds(pl.multiple_of(key_row * GRID_W, 256), NA_TILE_K)
    q = (q_ref[0].astype(F32) * (HEAD_DIM ** -0.5 * LOG2E)).astype(BF16)
    k_loc = k_ref[0, keys, :]
    k_ctx = ck_scr[...]
    halves = [slice(i * NA_TILE_Q // 2, (i + 1) * NA_TILE_Q // 2) for i in range(2)]
    s_loc = [_dot_nt(q[h], k_loc) + bias_scr[case, h, :] for h in halves]
    s_ctx = [_dot_nt(q[h], k_ctx) for h in halves]
    m = [jnp.maximum(jnp.max(sl, axis=-1, keepdims=True), jnp.max(sc, axis=-1, keepdims=True))
         for sl, sc in zip(s_loc, s_ctx)]
    e_loc = [jnp.exp2(sl - mh).astype(BF16) for sl, mh in zip(s_loc, m)]
    e_ctx = [jnp.exp2(sc - mh).astype(BF16) for sc, mh in zip(s_ctx, m)]
    v_loc = v_scr[keys, :]
    v_ctx = cv_scr[...]
    r = [jnp.dot(el, v_loc, preferred_element_type=F32) + jnp.dot(ec, v_ctx, preferred_element_type=F32)
         for el, ec in zip(e_loc, e_ctx)]
    for h, rh in zip(halves, r):
        o = rh[:, 0:HEAD_DIM] / rh[:, HEAD_DIM:HEAD_DIM + 1]
        o_ref[0, h, :] = (o * _silu(g_ref[0, h, :].astype(F32))).astype(o_ref.dtype)


def neighborhood_attention(p3, ctx_k, ctx_v, rpb):
    b, seq, _ = p3.shape
    ctx_len = ctx_k.shape[2]
    tab = _na_bias_blocks(rpb)
    n_pairs = tab.shape[1]
    ctx_spec = pl.BlockSpec((1, 1, ctx_len, HEAD_DIM), lambda h, i, t: (i, h, 0, 0))
    return pl.pallas_call(
        _na_kernel,
        out_shape=jax.ShapeDtypeStruct((b, seq, HEADS * HEAD_DIM), BF16),
        grid=(HEADS, b, NA_TILES),
        in_specs=[pl.BlockSpec((1, NA_TILE_Q, HEAD_DIM), lambda h, i, t: (i, t, h)),
                  pl.BlockSpec((1, seq, HEAD_DIM), lambda h, i, t: (i, 0, HEADS + h)),
                  pl.BlockSpec((1, seq, HEAD_DIM), lambda h, i, t: (i, 0, 2 * HEADS + h)),
                  pl.BlockSpec((1, NA_TILE_Q, HEAD_DIM), lambda h, i, t: (i, t, 3 * HEADS + h)),
                  ctx_spec, ctx_spec,
                  pl.BlockSpec((1, n_pairs, GRID_W, 2 * GRID_W), lambda h, i, t: (h, 0, 0, 0))],
        out_specs=pl.BlockSpec((1, NA_TILE_Q, HEAD_DIM), lambda h, i, t: (i, t, h)),
        scratch_shapes=[pltpu.VMEM((NA_CASES, NA_TILE_Q, NA_TILE_K), F32),
                        pltpu.VMEM((ctx_len, HEAD_DIM), BF16), pltpu.VMEM((ctx_len, 2 * HEAD_DIM), BF16),
                        pltpu.VMEM((seq, 2 * HEAD_DIM), BF16)],
        compiler_params=_params("arbitrary", "arbitrary", "arbitrary"),
        name="neighborhood_attention",
    )(p3, p3, p3, p3, ctx_k, ctx_v, tab)


def _sgu_kernel(u_ref, v_ref, g_ref, ln_ref, w_ref, b_ref, o_ref):
    v = v_ref[...].astype(F32)
    mu = jnp.mean(v, axis=-1, keepdims=True)
    vc = v - mu
    var = jnp.mean(vc * vc, axis=-1, keepdims=True)
    vn = (vc * lax.rsqrt(var + LN_EPS) * ln_ref[...]).astype(BF16)
    groups = w_ref.shape[0]
    for n in range(v.shape[0] // SGU_CHUNK):
        rows = slice(n * SGU_CHUNK, (n + 1) * SGU_CHUNK)
        for g in range(groups):
            cols = slice(g * SGU_GROUP_DIM, (g + 1) * SGU_GROUP_DIM)
            s = jnp.dot(w_ref[g], vn[rows, cols], preferred_element_type=F32) + b_ref[:, g:g + 1]
            y = u_ref[rows, cols].astype(F32) * s * _silu(g_ref[rows, cols].astype(F32))
            o_ref[rows, cols] = y.astype(o_ref.dtype)


def spatial_gating(p2, sgu_ln, w_s_bf16, b_s):
    m = p2.shape[0]
    groups = w_s_bf16.shape[0]
    width = groups * SGU_GROUP_DIM
    blk = lambda c: pl.BlockSpec((SGU_TM, width), lambda i: (i, c))
    return pl.pallas_call(
        _sgu_kernel,
        out_shape=jax.ShapeDtypeStruct((m, width), BF16),
        grid=(m // SGU_TM,),
        in_specs=[blk(4), blk(5), blk(6),
                  pl.BlockSpec((1, width), lambda i: (0, 0)),
                  pl.BlockSpec((groups, SGU_CHUNK, SGU_CHUNK), lambda i: (0, 0, 0)),
                  pl.BlockSpec((SGU_CHUNK, groups), lambda i: (0, 0))],
        out_specs=pl.BlockSpec((SGU_TM, width), lambda i: (i, 0)),
        compiler_params=_params("parallel"),
        name="spatial_gating",
    )(p2, p2, p2, sgu_ln.reshape(1, width), w_s_bf16, jnp.transpose(b_s))


def kernel(x_prompt, x_sample, cache_k_l0, cache_v_l0, cache_k_l1, cache_v_l1, cache_k_l2, cache_v_l2, cache_k_l3, cache_v_l3, c, c_ctx, w_mod_0, b_mod_0, w_in_0, w_out_0, ln_g_0, ln_b_0, pool_w_0, pool_scale_0, diff_lam_0, diff_subln_0, w_mod_1, b_mod_1, w_in_1, w_out_1, ln_g_1, ln_b_1, rpb_1, sgu_ln_1, sgu_w_1, sgu_b_1, w_mod_2, b_mod_2, w_in_2, w_out_2, ln_g_2, ln_b_2, pool_w_2, pool_scale_2, diff_lam_2, diff_subln_2, w_mod_3, b_mod_3, w_in_3, w_out_3, ln_g_3, ln_b_3, rpb_3, sgu_ln_3, sgu_w_3, sgu_b_3):
    cache_k = [cache_k_l0, cache_k_l1, cache_k_l2, cache_k_l3]
    cache_v = [cache_v_l0, cache_v_l1, cache_v_l2, cache_v_l3]
    w_mod = [w_mod_0, w_mod_1, w_mod_2, w_mod_3]
    b_mod = [b_mod_0, b_mod_1, b_mod_2, b_mod_3]
    w_in = [w_in_0, w_in_1, w_in_2, w_in_3]
    w_out = [w_out_0, w_out_1, w_out_2, w_out_3]
    ln_g = [ln_g_0, ln_g_1, ln_g_2, ln_g_3]
    ln_b = [ln_b_0, ln_b_1, ln_b_2, ln_b_3]
    even_p = {0: (pool_w_0, pool_scale_0, diff_lam_0, diff_subln_0),
              2: (pool_w_2, pool_scale_2, diff_lam_2, diff_subln_2)}
    odd_p = {1: (rpb_1, sgu_ln_1, sgu_w_1, sgu_b_1),
             3: (rpb_3, sgu_ln_3, sgu_w_3, sgu_b_3)}

    bp, lp, _ = x_prompt.shape
    bs, ls, _ = x_sample.shape
    xp = x_prompt.reshape(bp * lp, D_MODEL)
    xs = x_sample.reshape(bs * ls, D_MODEL)
    cond = jnp.concatenate([c, c_ctx[None, :], jnp.zeros((COND_ROWS - bs - 1, D_MODEL), F32)], axis=0)

    new_k, new_v = [], []
    for l in range(DEPTH):
        mod3 = ada_params(cond, w_mod[l], b_mod[l]).reshape(COND_ROWS, 1, 3 * D_MODEL)
        w_in_l = w_in[l].astype(BF16)
        n_in = w_in_l.shape[1]
        pp = in_projection(xp, mod3, w_in_l, None, F32)
        ps = in_projection(xs, mod3, w_in_l, ls, BF16)
        pp3 = pp.reshape(bp, lp, n_in)
        ps3 = ps.reshape(bs, ls, n_in)
        if l % 2 == 0:
            pool_w, pool_scale, diff_lam, diff_subln = even_p[l]
            pool_w = pool_w.astype(BF16)
            ya_p = pool_mixer(pp3, pool_w, pool_scale)
            ya_s = pool_mixer(ps3, pool_w, pool_scale)
            yb_p, nk, nv = diff_attention_context(pp3, diff_lam, diff_subln, l)
            yb_s = diff_attention_latent(ps3, diff_lam, diff_subln, l, cache_k[l], cache_v[l])
        else:
            rpb, sgu_ln, sgu_w, sgu_b = odd_p[l]
            sgu_w = sgu_w.astype(BF16)
            ya_p, nk, nv = context_attention(pp3)
            ya_s = neighborhood_attention(ps3, cache_k[l], cache_v[l], rpb)
            yb_p = spatial_gating(pp, sgu_ln, sgu_w, sgu_b)
            yb_s = spatial_gating(ps, sgu_ln, sgu_w, sgu_b)
        half = ya_p.shape[-1]
        w_out_l = w_out[l].astype(BF16)
        xp = out_projection(ya_p.reshape(bp * lp, half), yb_p.reshape(bp * lp, half), xp, mod3,
                            w_out_l, ln_g[l], ln_b[l], None)
        xs = out_projection(ya_s.reshape(bs * ls, half), yb_s.reshape(bs * ls, half), xs, mod3,
                            w_out_l, ln_g[l], ln_b[l], ls)
        new_k.append(nk)
        new_v.append(nv)

    return (xp.reshape(bp, lp, D_MODEL), xs.reshape(bs, ls, D_MODEL),
            new_k[0], new_v[0], new_k[1], new_v[1], new_k[2], new_v[2], new_k[3], new_v[3])
```

```python
import functools
import math

import numpy as np
import jax
import jax.numpy as jnp
from jax import lax
from jax.experimental import pallas as pl
from jax.experimental.pallas import tpu as pltpu

F32 = jnp.float32
BF16 = jnp.bfloat16

D_MODEL = 2048
DEPTH = 4
GRID_W = 64
GRID_ROWS = 64
HEADS = 8
HEAD_DIM = 128
QK_HALF = 64
POOL_WINDOWS = (2, 4, 8, 16)
POOL_GROUP_DIM = 256
SGU_CHUNK = 128
SGU_GROUP_DIM = 256
NA_WIN_R = 8
NA_WIN_C = 16
NA_Q_ROWS = 8
NA_K_ROWS = 16
ROPE_BASE = 10000.0
LN_EPS = 1e-5
NEG_INF = -1e30
DEEPNORM_ALPHA = (2 * DEPTH) ** 0.25
LOG2E = math.log2(math.e)

VMEM_LIMIT_BYTES = 56 * 1024 * 1024
COND_ROWS = 8
CTX_ROW = 4

PROJ_TM = 1024
PROJ_TN = 1024
OUT_TM = 512
OUT_ROW_GROUP = 256
ADA_TN = 512
POOL_ROWS = 256
POOL_HALO = 16
POOL_CHUNKS_PER_ITER = 8
SGU_TM = 512
DIFF_TQ = 512
PREP_ROWS = 512


def _params(*sem):
    return pltpu.CompilerParams(dimension_semantics=sem, vmem_limit_bytes=VMEM_LIMIT_BYTES)


def _silu(x):
    return x * jax.nn.sigmoid(x)


def _dot_nt(a, b):
    return lax.dot_general(a, b, (((1,), (1,)), ((), ())), preferred_element_type=F32)


def _ada_kernel(c_ref, w_ref, b_ref, o_ref):
    a = _silu(c_ref[...]).astype(BF16)
    o_ref[...] = jnp.dot(a, w_ref[...].astype(BF16), preferred_element_type=F32) + b_ref[...]


def ada_params(cond, w_mod, b_mod):
    n = w_mod.shape[1]
    return pl.pallas_call(
        _ada_kernel,
        out_shape=jax.ShapeDtypeStruct((COND_ROWS, n), F32),
        grid=(n // ADA_TN,),
        in_specs=[pl.BlockSpec((COND_ROWS, D_MODEL), lambda j: (0, 0)),
                  pl.BlockSpec((D_MODEL, ADA_TN), lambda j: (0, j)),
                  pl.BlockSpec((1, ADA_TN), lambda j: (0, j))],
        out_specs=pl.BlockSpec((COND_ROWS, ADA_TN), lambda j: (0, j)),
        compiler_params=_params("parallel"),
        name="ada_params",
    )(cond, w_mod, b_mod.reshape(1, n))


def _inproj_kernel(x_ref, sh_ref, sc_ref, w_ref, o_ref, h_scr):
    @pl.when(pl.program_id(1) == 0)
    def _():
        h = x_ref[...] * (1.0 + sc_ref[0]) + sh_ref[0]
        h_scr[...] = h.astype(BF16)

    o_ref[...] = jnp.dot(h_scr[...], w_ref[...], preferred_element_type=F32).astype(o_ref.dtype)


def in_projection(x2d, mod3, w_bf16, rows_per_cond, out_dtype):
    m = x2d.shape[0]
    n = w_bf16.shape[1]
    if rows_per_cond is None:
        cond_row = lambda i: CTX_ROW
    else:
        blocks_per_cond = rows_per_cond // PROJ_TM
        cond_row = lambda i: i // blocks_per_cond
    return pl.pallas_call(
        _inproj_kernel,
        out_shape=jax.ShapeDtypeStruct((m, n), out_dtype),
        grid=(m // PROJ_TM, n // PROJ_TN),
        in_specs=[pl.BlockSpec((PROJ_TM, D_MODEL), lambda i, j: (i, 0)),
                  pl.BlockSpec((1, 1, D_MODEL), lambda i, j: (cond_row(i), 0, 0)),
                  pl.BlockSpec((1, 1, D_MODEL), lambda i, j: (cond_row(i), 0, 1)),
                  pl.BlockSpec((D_MODEL, PROJ_TN), lambda i, j: (0, j))],
        out_specs=pl.BlockSpec((PROJ_TM, PROJ_TN), lambda i, j: (i, j)),
        scratch_shapes=[pltpu.VMEM((PROJ_TM, D_MODEL), BF16)],
        compiler_params=_params("parallel", "arbitrary"),
        name="in_projection",
    )(x2d, mod3, mod3, w_bf16)


def _outproj_kernel(ya_ref, yb_ref, x_ref, g_ref, w_ref, lng_ref, lnb_ref, o_ref):
    half = ya_ref.shape[1]
    for r in range(OUT_TM // OUT_ROW_GROUP):
        rows = slice(r * OUT_ROW_GROUP, (r + 1) * OUT_ROW_GROUP)
        acc = jnp.dot(ya_ref[rows, :], w_ref[0:half, :], preferred_element_type=F32)
        acc = acc + jnp.dot(yb_ref[rows, :], w_ref[half:, :], preferred_element_type=F32)
        z = DEEPNORM_ALPHA * x_ref[rows, :] + g_ref[0] * acc
        mu = jnp.mean(z, axis=-1, keepdims=True)
        zc = z - mu
        var = jnp.mean(zc * zc, axis=-1, keepdims=True)
        o_ref[rows, :] = zc * lax.rsqrt(var + LN_EPS) * lng_ref[...] + lnb_ref[...]


def out_projection(ya, yb, x2d, mod3, w_bf16, ln_g, ln_b, rows_per_cond):
    m = x2d.shape[0]
    half = ya.shape[1]
    if rows_per_cond is None:
        cond_row = lambda i: CTX_ROW
    else:
        blocks_per_cond = rows_per_cond // OUT_TM
        cond_row = lambda i: i // blocks_per_cond
    return pl.pallas_call(
        _outproj_kernel,
        out_shape=jax.ShapeDtypeStruct((m, D_MODEL), F32),
        grid=(m // OUT_TM,),
        in_specs=[pl.BlockSpec((OUT_TM, half), lambda i: (i, 0)),
                  pl.BlockSpec((OUT_TM, half), lambda i: (i, 0)),
                  pl.BlockSpec((OUT_TM, D_MODEL), lambda i: (i, 0)),
                  pl.BlockSpec((1, 1, D_MODEL), lambda i: (cond_row(i), 0, 2)),
                  pl.BlockSpec((2 * half, D_MODEL), lambda i: (0, 0)),
                  pl.BlockSpec((1, D_MODEL), lambda i: (0, 0)),
                  pl.BlockSpec((1, D_MODEL), lambda i: (0, 0))],
        out_specs=pl.BlockSpec((OUT_TM, D_MODEL), lambda i: (i, 0)),
        compiler_params=_params("parallel"),
        name="out_projection",
    )(ya, yb, x2d, mod3, w_bf16, ln_g.reshape(1, D_MODEL), ln_b.reshape(1, D_MODEL))


def _pool_bands():
    assert max(POOL_WINDOWS) // 2 <= POOL_HALO
    i = np.arange(POOL_ROWS)[:, None]
    own = np.arange(POOL_ROWS)[None, :]
    mains = [(own >= i - w // 2) & (own < i + w // 2) for w in POOL_WINDOWS]
    top_i = np.arange(POOL_HALO)[:, None]
    bot_i = POOL_ROWS - POOL_HALO + top_i
    before = np.arange(POOL_HALO)[None, :] - POOL_HALO
    after = POOL_ROWS + np.arange(POOL_HALO)[None, :]
    halos = [np.stack([before >= top_i - w // 2, after < bot_i + w // 2]) for w in POOL_WINDOWS]
    return np.stack(mains).astype(np.float32), np.stack(halos).astype(np.float32)


def _pool_kernel(a_ref, g_ref, w_ref, s_ref, bm_ref, bh_ref, o_ref, *, seq):
    n_chunks = seq // POOL_ROWS
    group = pl.program_id(1)
    half = jnp.int32(0)
    for gi, window in enumerate(POOL_WINDOWS):
        half = jnp.where(group == gi, window // 2, half)

    def chunk(bi, ci):
        r0 = pl.multiple_of(ci * POOL_ROWS, POOL_ROWS)
        main = a_ref[bi, pl.ds(r0, POOL_ROWS), :]
        win = jnp.dot(bm_ref[0], main.astype(BF16), preferred_element_type=F32)
        if n_chunks > 1:
            p0 = pl.multiple_of(jnp.maximum(r0 - POOL_HALO, 0), POOL_HALO)
            n0 = pl.multiple_of(jnp.minimum(r0 + POOL_ROWS, seq - POOL_HALO), POOL_HALO)
            prev = jnp.where(ci > 0, a_ref[bi, pl.ds(p0, POOL_HALO), :].astype(F32), 0.0).astype(BF16)
            nxt = jnp.where(ci < n_chunks - 1, a_ref[bi, pl.ds(n0, POOL_HALO), :].astype(F32), 0.0).astype(BF16)
            top = win[0:POOL_HALO] + jnp.dot(bh_ref[0, 0], prev, preferred_element_type=F32)
            bot = win[POOL_ROWS - POOL_HALO:] + jnp.dot(bh_ref[0, 1], nxt, preferred_element_type=F32)
            win = jnp.concatenate([top, win[POOL_HALO:POOL_ROWS - POOL_HALO], bot], axis=0)
        t = r0 + lax.broadcasted_iota(jnp.int32, (POOL_ROWS, 1), 0)
        cnt = (jnp.minimum(t + half, seq) - jnp.maximum(t - half, 0)).astype(F32)
        pooled = win / cnt - main.astype(F32)
        mixed = jnp.dot(pooled.astype(BF16), w_ref[0], preferred_element_type=F32)
        gate = g_ref[bi, pl.ds(r0, POOL_ROWS), :].astype(F32)
        y = mixed * s_ref[...] * _silu(gate)
        o_ref[bi, pl.ds(r0, POOL_ROWS), :] = y.astype(o_ref.dtype)

    def group_of_chunks(it, carry):
        for u in range(POOL_CHUNKS_PER_ITER):
            idx = it * POOL_CHUNKS_PER_ITER + u
            chunk(idx // n_chunks, idx % n_chunks)
        return carry

    lax.fori_loop(0, a_ref.shape[0] * n_chunks // POOL_CHUNKS_PER_ITER, group_of_chunks, 0)


def pool_mixer(p3, w_pool_bf16, pool_scale):
    b, seq, _ = p3.shape
    groups = len(POOL_WINDOWS)
    band_main, band_halo = (jnp.asarray(m, BF16) for m in _pool_bands())
    bb = max(1, POOL_CHUNKS_PER_ITER * POOL_ROWS // seq)
    return pl.pallas_call(
        functools.partial(_pool_kernel, seq=seq),
        out_shape=jax.ShapeDtypeStruct((b, seq, groups * POOL_GROUP_DIM), BF16),
        grid=(b // bb, groups),
        in_specs=[pl.BlockSpec((bb, seq, POOL_GROUP_DIM), lambda i, g: (i, 0, g)),
                  pl.BlockSpec((bb, seq, POOL_GROUP_DIM), lambda i, g: (i, 0, groups + g)),
                  pl.BlockSpec((1, POOL_GROUP_DIM, POOL_GROUP_DIM), lambda i, g: (g, 0, 0)),
                  pl.BlockSpec((1, POOL_GROUP_DIM), lambda i, g: (0, g)),
                  pl.BlockSpec((1, POOL_ROWS, POOL_ROWS), lambda i, g: (g, 0, 0)),
                  pl.BlockSpec((1, 2, POOL_HALO, POOL_HALO), lambda i, g: (g, 0, 0, 0))],
        out_specs=pl.BlockSpec((bb, seq, POOL_GROUP_DIM), lambda i, g: (i, 0, g)),
        compiler_params=_params("parallel", "parallel"),
        name="pool_mixer",
    )(p3, p3, w_pool_bf16, pool_scale.reshape(1, groups * POOL_GROUP_DIM), band_main, band_halo)


def _rope_tables():
    t = np.arange(GRID_ROWS * GRID_W)
    row = (t // GRID_W).astype(np.float64)[:, None]
    col = (t % GRID_W).astype(np.float64)[:, None]
    axis_dim = QK_HALF // 2
    inv = 1.0 / (ROPE_BASE ** (np.arange(0, axis_dim, 2, dtype=np.float64) / axis_dim))
    lane = np.arange(HEAD_DIM)
    within = lane % QK_HALF
    pos = np.where((within < axis_dim)[None, :], row, col)
    ang = pos * inv[within % (axis_dim // 2)][None, :]
    is_a = ((within % axis_dim) < axis_dim // 2)[None, :]
    cos = np.cos(ang)
    sin = np.sin(ang)
    return (cos.astype(np.float32),
            np.where(is_a, -sin, 0.0).astype(np.float32),
            np.where(is_a, 0.0, sin).astype(np.float32))


def _rotate_lanes(x, cos, sin_a, sin_b):
    shift = QK_HALF // 4
    return (x * cos + pltpu.roll(x, HEAD_DIM - shift, 1) * sin_a
            + pltpu.roll(x, shift, 1) * sin_b)


def _diff_lambda(lam_ref, lam_init):
    dl = lam_ref[...]
    return (jnp.exp(jnp.sum(dl[0:1] * dl[1:2], axis=-1, keepdims=True))
            - jnp.exp(jnp.sum(dl[2:3] * dl[3:4], axis=-1, keepdims=True)) + lam_init)


def _split_components(q):
    lane = lax.broadcasted_iota(jnp.int32, (1, HEAD_DIM), 1)
    return (jnp.where(lane < QK_HALF, q, 0.0).astype(BF16),
            jnp.where(lane >= QK_HALF, q, 0.0).astype(BF16))


def _diff_combine(e1, e2, lam, v, sub, lam_init, gate):
    w1 = 1.0 / jnp.sum(e1, axis=-1, keepdims=True)
    w2 = lam / jnp.sum(e2, axis=-1, keepdims=True)
    p = (e1 * w1 - e2 * w2).astype(BF16)
    o = jnp.dot(p, v, preferred_element_type=F32)
    o = o * lax.rsqrt(jnp.mean(o * o, axis=-1, keepdims=True) + LN_EPS)
    o = o * sub * (1.0 - lam_init)
    return o * _silu(gate)


def _diff_latent_kernel(q_ref, k_ref, v_ref, g_ref, ck_ref, cv_ref, cos_ref, sa_ref, sb_ref,
                        lam_ref, sub_ref, o_ref, kt_scr, v_scr, *, seq, tq, lam_init):
    lane = lax.broadcasted_iota(jnp.int32, (1, HEAD_DIM), 1)
    ones_col = jnp.broadcast_to(jnp.where(lane == 0, 1.0, 0.0).astype(BF16), (PREP_ROWS, HEAD_DIM))

    def stage(ci, carry):
        rows = pl.ds(pl.multiple_of(ci * PREP_ROWS, PREP_ROWS), PREP_ROWS)
        k = _rotate_lanes(k_ref[0, rows, :].astype(F32), cos_ref[rows, :], sa_ref[rows, :], sb_ref[rows, :])
        kt_scr[:, rows] = jnp.transpose(k).astype(BF16)
        v_scr[rows, 0:HEAD_DIM] = v_ref[0, rows, :].astype(BF16)
        v_scr[rows, HEAD_DIM:] = ones_col
        return carry

    lax.fori_loop(0, seq // PREP_ROWS, stage, 0)
    kt_scr[:, seq:] = jnp.transpose(ck_ref[0, 0]).astype(BF16)
    v_scr[seq:, 0:HEAD_DIM] = cv_ref[0, 0].astype(BF16)
    v_scr[seq:, HEAD_DIM:] = ones_col[0:v_scr.shape[0] - seq]
    lam = _diff_lambda(lam_ref, lam_init)

    def tile(t, carry):
        rows = pl.ds(pl.multiple_of(t * tq, tq), tq)
        q = _rotate_lanes(q_ref[0, rows, :].astype(F32), cos_ref[rows, :], sa_ref[rows, :], sb_ref[rows, :])
        q1, q2 = _split_components(q * (QK_HALF ** -0.5 * LOG2E))
        keys_t = kt_scr[...]
        values = v_scr[...]
        s = [jnp.dot(qc, keys_t, preferred_element_type=F32) for qc in (q1, q2)]
        e = [jnp.exp2(sc - jnp.max(sc, axis=-1, keepdims=True)).astype(BF16) for sc in s]
        r = [jnp.dot(ec, values, preferred_element_type=F32) for ec in e]
        w1 = 1.0 / r[0][:, HEAD_DIM:HEAD_DIM + 1]
        w2 = lam / r[1][:, HEAD_DIM:HEAD_DIM + 1]
        o = r[0][:, 0:HEAD_DIM] * w1 - r[1][:, 0:HEAD_DIM] * w2
        o = o * lax.rsqrt(jnp.mean(o * o, axis=-1, keepdims=True) + LN_EPS)
        o = o * sub_ref[...] * (1.0 - lam_init)
        o_ref[0, rows, :] = (o * _silu(g_ref[0, rows, :].astype(F32))).astype(o_ref.dtype)
        return carry

    lax.fori_loop(0, seq // tq, tile, 0)


def diff_attention_latent(p3, diff_lam, diff_subln, layer_idx, ctx_k, ctx_v):
    b, seq, _ = p3.shape
    ctx_len = ctx_k.shape[2]
    tq = DIFF_TQ
    nk = seq + ctx_len
    lam_init = 0.8 - 0.6 * math.exp(-0.3 * layer_idx)
    q0, k0, v0, g0 = 2 * HEADS, 3 * HEADS, 4 * HEADS, 5 * HEADS
    const = lambda i, h: (0, 0)
    head_blk = lambda off: pl.BlockSpec((1, seq, HEAD_DIM), lambda i, h: (i, 0, off + h))
    ctx_spec = pl.BlockSpec((1, 1, ctx_len, HEAD_DIM), lambda i, h: (i, h, 0, 0))
    tab_spec = pl.BlockSpec((seq, HEAD_DIM), const, pipeline_mode=pl.Buffered(1))
    return pl.pallas_call(
        functools.partial(_diff_latent_kernel, seq=seq, tq=tq, lam_init=lam_init),
        out_shape=jax.ShapeDtypeStruct((b, seq, HEADS * HEAD_DIM), BF16),
        grid=(b, HEADS),
        in_specs=[head_blk(q0), head_blk(k0), head_blk(v0), head_blk(g0),
                  ctx_spec, ctx_spec, tab_spec, tab_spec, tab_spec,
                  pl.BlockSpec(diff_lam.shape, const),
                  pl.BlockSpec((1, HEAD_DIM), const)],
        out_specs=pl.BlockSpec((1, seq, HEAD_DIM), lambda i, h: (i, 0, h)),
        scratch_shapes=[pltpu.VMEM((HEAD_DIM, nk), BF16), pltpu.VMEM((nk, 2 * HEAD_DIM), BF16)],
        compiler_params=_params("parallel", "parallel"),
        name="diff_attention_latent",
    )(p3, p3, p3, p3, ctx_k, ctx_v, *[jnp.asarray(tab) for tab in _rope_tables()],
      diff_lam, diff_subln.reshape(1, HEAD_DIM))


def _diff_context_kernel(q_ref, k_ref, v_ref, g_ref, lam_ref, sub_ref, o_ref, kc_ref, vc_ref, *, lam_init):
    lam = _diff_lambda(lam_ref, lam_init)
    for h in range(HEADS):
        cols = slice(h * HEAD_DIM, (h + 1) * HEAD_DIM)
        k = k_ref[0, :, cols]
        v = v_ref[0, :, cols]
        kc_ref[0, h] = k
        vc_ref[0, h] = v
        q1, q2 = _split_components(q_ref[0, :, cols] * (QK_HALF ** -0.5 * LOG2E))
        kb = k.astype(BF16)
        s1 = _dot_nt(q1, kb)
        s2 = _dot_nt(q2, kb)
        e1 = jnp.exp2(s1 - jnp.max(s1, axis=-1, keepdims=True))
        e2 = jnp.exp2(s2 - jnp.max(s2, axis=-1, keepdims=True))
        y = _diff_combine(e1, e2, lam, v.astype(BF16), sub_ref[...], lam_init, g_ref[0, :, cols])
        o_ref[0, :, cols] = y.astype(o_ref.dtype)


def diff_attention_context(p3, diff_lam, diff_subln, layer_idx):
    b, seq, _ = p3.shape
    width = HEADS * HEAD_DIM
    lam_init = 0.8 - 0.6 * math.exp(-0.3 * layer_idx)
    blk = lambda c: pl.BlockSpec((1, seq, width), lambda i: (i, 0, c))
    cache_spec = pl.BlockSpec((1, HEADS, seq, HEAD_DIM), lambda i: (i, 0, 0, 0))
    return pl.pallas_call(
        functools.partial(_diff_context_kernel, lam_init=lam_init),
        out_shape=[jax.ShapeDtypeStruct((b, seq, width), BF16),
                   jax.ShapeDtypeStruct((b, HEADS, seq, HEAD_DIM), F32),
                   jax.ShapeDtypeStruct((b, HEADS, seq, HEAD_DIM), F32)],
        grid=(b,),
        in_specs=[blk(2), blk(3), blk(4), blk(5),
                  pl.BlockSpec(diff_lam.shape, lambda i: (0, 0)),
                  pl.BlockSpec((1, HEAD_DIM), lambda i: (0, 0))],
        out_specs=[pl.BlockSpec((1, seq, width), lambda i: (i, 0, 0)), cache_spec, cache_spec],
        compiler_params=_params("parallel"),
        name="diff_attention_context",
    )(p3, p3, p3, p3, diff_lam, diff_subln.reshape(1, HEAD_DIM))


def _ctx_attn_kernel(q_ref, k_ref, v_ref, g_ref, o_ref, kc_ref, vc_ref):
    for h in range(HEADS):
        cols = slice(h * HEAD_DIM, (h + 1) * HEAD_DIM)
        k = k_ref[0, :, cols]
        v = v_ref[0, :, cols]
        kc_ref[0, h] = k
        vc_ref[0, h] = v
        q = (q_ref[0, :, cols] * (HEAD_DIM ** -0.5)).astype(BF16)
        s = _dot_nt(q, k.astype(BF16))
        e = jnp.exp(s - jnp.max(s, axis=-1, keepdims=True))
        o = jnp.dot(e.astype(BF16), v.astype(BF16), preferred_element_type=F32)
        o = o / jnp.sum(e, axis=-1, keepdims=True)
        o_ref[0, :, cols] = (o * _silu(g_ref[0, :, cols])).astype(o_ref.dtype)


def context_attention(p3):
    b, seq, _ = p3.shape
    width = HEADS * HEAD_DIM
    blk = lambda c: pl.BlockSpec((1, seq, width), lambda i: (i, 0, c))
    cache_spec = pl.BlockSpec((1, HEADS, seq, HEAD_DIM), lambda i: (i, 0, 0, 0))
    return pl.pallas_call(
        _ctx_attn_kernel,
        out_shape=[jax.ShapeDtypeStruct((b, seq, width), BF16),
                   jax.ShapeDtypeStruct((b, HEADS, seq, HEAD_DIM), F32),
                   jax.ShapeDtypeStruct((b, HEADS, seq, HEAD_DIM), F32)],
        grid=(b,),
        in_specs=[blk(0), blk(1), blk(2), blk(3)],
        out_specs=[pl.BlockSpec((1, seq, width), lambda i: (i, 0, 0)), cache_spec, cache_spec],
        compiler_params=_params("parallel"),
        name="context_attention",
    )(p3, p3, p3, p3)


NA_TILE_Q = NA_Q_ROWS * GRID_W
NA_TILE_K = NA_K_ROWS * GRID_W
NA_TILES = GRID_ROWS // NA_Q_ROWS
NA_CASES = 3
NA_DR = 2 * NA_WIN_R - 1


def _na_key_row_start(tile):
    return int(np.clip(NA_Q_ROWS * tile - NA_WIN_R // 2, 0, GRID_ROWS - NA_K_ROWS))


def _na_bias_layout():
    dr = np.full((NA_CASES, NA_Q_ROWS, NA_K_ROWS), NA_DR, np.int32)
    for case, tile in enumerate((0, 1, NA_TILES - 1)):
        ks = _na_key_row_start(tile)
        for qr_local in range(NA_Q_ROWS):
            qr = NA_Q_ROWS * tile + qr_local
            rs = int(np.clip(qr - NA_WIN_R // 2, 0, GRID_ROWS - NA_WIN_R))
            for kr_local in range(NA_K_ROWS):
                kr = ks + kr_local
                if rs <= kr < rs + NA_WIN_R:
                    dr[case, qr_local, kr_local] = kr - qr + NA_WIN_R - 1
    pairs = sorted({(int(a), int(b)) for a, b in zip(dr[..., 0::2].ravel(), dr[..., 1::2].ravel())})
    index = {p: i for i, p in enumerate(pairs)}
    block = np.array([[[index[(int(dr[c, q, 2 * p]), int(dr[c, q, 2 * p + 1]))]
                        for p in range(NA_K_ROWS // 2)] for q in range(NA_Q_ROWS)] for c in range(NA_CASES)])
    return pairs, block


def _na_bias_blocks(rpb):
    qc = np.arange(GRID_W)[:, None]
    kc = np.arange(GRID_W)[None, :]
    cstart = np.clip(qc - NA_WIN_C // 2, 0, GRID_W - NA_WIN_C)
    col_valid = (kc >= cstart) & (kc < cstart + NA_WIN_C)
    dc = np.clip(kc - qc + NA_WIN_C - 1, 0, 2 * NA_WIN_C - 2)
    per_dr = jnp.where(jnp.asarray(col_valid), rpb[:, :, dc], NEG_INF)
    per_dr = jnp.concatenate([per_dr, jnp.full_like(per_dr[:, :1], NEG_INF)], axis=1)
    pairs, _ = _na_bias_layout()
    left = np.array([p[0] for p in pairs])
    right = np.array([p[1] for p in pairs])
    return jnp.concatenate([per_dr[:, left], per_dr[:, right]], axis=-1)


def _na_kernel(q_ref, k_ref, v_ref, g_ref, ck_ref, cv_ref, tab_ref, o_ref,
               bias_scr, ck_scr, cv_scr, v_scr):
    b = pl.program_id(1)
    seq = v_ref.shape[1]

    @pl.when(b == 0)
    def _build_bias():
        _, block = _na_bias_layout()
        for case in range(NA_CASES):
            for qr in range(NA_Q_ROWS):
                for p in range(NA_K_ROWS // 2):
                    bias_scr[case, qr * GRID_W:(qr + 1) * GRID_W, p * 128:(p + 1) * 128] = (
                        tab_ref[0, int(block[case, qr, p])] * LOG2E)

    lane = lax.broadcasted_iota(jnp.int32, (1, HEAD_DIM), 1)
    ones_col = jnp.broadcast_to(jnp.where(lane == 0, 1.0, 0.0).astype(BF16), (PREP_ROWS, HEAD_DIM))
    ck_scr[...] = ck_ref[0, 0].astype(BF16)
    cv_scr[:, 0:HEAD_DIM] = cv_ref[0, 0].astype(BF16)
    cv_scr[:, HEAD_DIM:] = ones_col[0:cv_scr.shape[0]]

    def stage(ci, carry):
        rows = pl.ds(pl.multiple_of(ci * PREP_ROWS, PREP_ROWS), PREP_ROWS)
        v_scr[rows, 0:HEAD_DIM] = v_ref[0, rows, :]
        v_scr[rows, HEAD_DIM:] = ones_col
        return carry

    lax.fori_loop(0, seq // PREP_ROWS, stage, 0)
    half_rows = NA_TILE_Q // 2

    def tile(t, carry):
        case = jnp.where(t == 0, 0, jnp.where(t == NA_TILES - 1, 2, 1))
        key_row = jnp.clip(NA_Q_ROWS * t - NA_WIN_R // 2, 0, GRID_ROWS - NA_K_ROWS)
        keys = pl.ds(pl.multiple_of(key_row * GRID_W, 256), NA_TILE_K)
        k_loc = k_ref[0, keys, :]
        k_ctx = ck_scr[...]
        rows = [pl.ds(pl.multiple_of(t * NA_TILE_Q + i * half_rows, half_rows), half_rows) for i in range(2)]
        local = [slice(i * half_rows, (i + 1) * half_rows) for i in range(2)]
        q = [(q_ref[0, r, :].astype(F32) * (HEAD_DIM ** -0.5 * LOG2E)).astype(BF16) for r in rows]
        s_loc = [_dot_nt(qh, k_loc) + bias_scr[case, lr, :] for qh, lr in zip(q, local)]
        s_ctx = [_dot_nt(qh, k_ctx) for qh in q]
        m = [jnp.maximum(jnp.max(sl, axis=-1, keepdims=True), jnp.max(sc, axis=-1, keepdims=True))
             for sl, sc in zip(s_loc, s_ctx)]
        e_loc = [jnp.exp2(sl - mh).astype(BF16) for sl, mh in zip(s_loc, m)]
        e_ctx = [jnp.exp2(sc - mh).astype(BF16) for sc, mh in zip(s_ctx, m)]
        v_loc = v_scr[keys, :]
        v_ctx = cv_scr[...]
        res = [jnp.dot(el, v_loc, preferred_element_type=F32) + jnp.dot(ec, v_ctx, preferred_element_type=F32)
               for el, ec in zip(e_loc, e_ctx)]
        for r, rh in zip(rows, res):
            o = rh[:, 0:HEAD_DIM] / rh[:, HEAD_DIM:HEAD_DIM + 1]
            o_ref[0, r, :] = (o * _silu(g_ref[0, r, :].astype(F32))).astype(o_ref.dtype)
        return carry

    lax.fori_loop(0, NA_TILES, tile, 0)


def neighborhood_attention(p3, ctx_k, ctx_v, rpb):
    b, seq, _ = p3.shape
    ctx_len = ctx_k.shape[2]
    tab = _na_bias_blocks(rpb)
    n_pairs = tab.shape[1]
    ctx_spec = pl.BlockSpec((1, 1, ctx_len, HEAD_DIM), lambda h, i: (i, h, 0, 0))
    head_blk = lambda off: pl.BlockSpec((1, seq, HEAD_DIM), lambda h, i: (i, 0, off + h))
    return pl.pallas_call(
        _na_kernel,
        out_shape=jax.ShapeDtypeStruct((b, seq, HEADS * HEAD_DIM), BF16),
        grid=(HEADS, b),
        in_specs=[head_blk(0), head_blk(HEADS), head_blk(2 * HEADS), head_blk(3 * HEADS),
                  ctx_spec, ctx_spec,
                  pl.BlockSpec((1, n_pairs, GRID_W, 2 * GRID_W), lambda h, i: (h, 0, 0, 0))],
        out_specs=pl.BlockSpec((1, seq, HEAD_DIM), lambda h, i: (i, 0, h)),
        scratch_shapes=[pltpu.VMEM((NA_CASES, NA_TILE_Q, NA_TILE_K), F32),
                        pltpu.VMEM((ctx_len, HEAD_DIM), BF16), pltpu.VMEM((ctx_len, 2 * HEAD_DIM), BF16),
                        pltpu.VMEM((seq, 2 * HEAD_DIM), BF16)],
        compiler_params=_params("arbitrary", "arbitrary"),
        name="neighborhood_attention",
    )(p3, p3, p3, p3, ctx_k, ctx_v, tab)


def _sgu_kernel(u_ref, v_ref, g_ref, ln_ref, w_ref, b_ref, o_ref):
    v = v_ref[...].astype(F32)
    mu = jnp.mean(v, axis=-1, keepdims=True)
    vc = v - mu
    var = jnp.mean(vc * vc, axis=-1, keepdims=True)
    vn = (vc * lax.rsqrt(var + LN_EPS) * ln_ref[...]).astype(BF16)
    groups = w_ref.shape[0]
    for n in range(v.shape[0] // SGU_CHUNK):
        rows = slice(n * SGU_CHUNK, (n + 1) * SGU_CHUNK)
        for g in range(groups):
            cols = slice(g * SGU_GROUP_DIM, (g + 1) * SGU_GROUP_DIM)
            s = jnp.dot(w_ref[g], vn[rows, cols], preferred_element_type=F32) + b_ref[:, g:g + 1]
            y = u_ref[rows, cols].astype(F32) * s * _silu(g_ref[rows, cols].astype(F32))
            o_ref[rows, cols] = y.astype(o_ref.dtype)


def spatial_gating(p2, sgu_ln, w_s_bf16, b_s):
    m = p2.shape[0]
    groups = w_s_bf16.shape[0]
    width = groups * SGU_GROUP_DIM
    blk = lambda c: pl.BlockSpec((SGU_TM, width), lambda i: (i, c))
    return pl.pallas_call(
        _sgu_kernel,
        out_shape=jax.ShapeDtypeStruct((m, width), BF16),
        grid=(m // SGU_TM,),
        in_specs=[blk(4), blk(5), blk(6),
                  pl.BlockSpec((1, width), lambda i: (0, 0)),
                  pl.BlockSpec((groups, SGU_CHUNK, SGU_CHUNK), lambda i: (0, 0, 0)),
                  pl.BlockSpec((SGU_CHUNK, groups), lambda i: (0, 0))],
        out_specs=pl.BlockSpec((SGU_TM, width), lambda i: (i, 0)),
        compiler_params=_params("parallel"),
        name="spatial_gating",
    )(p2, p2, p2, sgu_ln.reshape(1, width), w_s_bf16, jnp.transpose(b_s))


def kernel(x_prompt, x_sample, cache_k_l0, cache_v_l0, cache_k_l1, cache_v_l1, cache_k_l2, cache_v_l2, cache_k_l3, cache_v_l3, c, c_ctx, w_mod_0, b_mod_0, w_in_0, w_out_0, ln_g_0, ln_b_0, pool_w_0, pool_scale_0, diff_lam_0, diff_subln_0, w_mod_1, b_mod_1, w_in_1, w_out_1, ln_g_1, ln_b_1, rpb_1, sgu_ln_1, sgu_w_1, sgu_b_1, w_mod_2, b_mod_2, w_in_2, w_out_2, ln_g_2, ln_b_2, pool_w_2, pool_scale_2, diff_lam_2, diff_subln_2, w_mod_3, b_mod_3, w_in_3, w_out_3, ln_g_3, ln_b_3, rpb_3, sgu_ln_3, sgu_w_3, sgu_b_3):
    cache_k = [cache_k_l0, cache_k_l1, cache_k_l2, cache_k_l3]
    cache_v = [cache_v_l0, cache_v_l1, cache_v_l2, cache_v_l3]
    w_mod = [w_mod_0, w_mod_1, w_mod_2, w_mod_3]
    b_mod = [b_mod_0, b_mod_1, b_mod_2, b_mod_3]
    w_in = [w_in_0, w_in_1, w_in_2, w_in_3]
    w_out = [w_out_0, w_out_1, w_out_2, w_out_3]
    ln_g = [ln_g_0, ln_g_1, ln_g_2, ln_g_3]
    ln_b = [ln_b_0, ln_b_1, ln_b_2, ln_b_3]
    even_p = {0: (pool_w_0, pool_scale_0, diff_lam_0, diff_subln_0),
              2: (pool_w_2, pool_scale_2, diff_lam_2, diff_subln_2)}
    odd_p = {1: (rpb_1, sgu_ln_1, sgu_w_1, sgu_b_1),
             3: (rpb_3, sgu_ln_3, sgu_w_3, sgu_b_3)}

    bp, lp, _ = x_prompt.shape
    bs, ls, _ = x_sample.shape
    xp = x_prompt.reshape(bp * lp, D_MODEL)
    xs = x_sample.reshape(bs * ls, D_MODEL)
    cond = jnp.concatenate([c, c_ctx[None, :], jnp.zeros((COND_ROWS - bs - 1, D_MODEL), F32)], axis=0)

    new_k, new_v = [], []
    for l in range(DEPTH):
        mod3 = ada_params(cond, w_mod[l], b_mod[l]).reshape(COND_ROWS, 1, 3 * D_MODEL)
        w_in_l = w_in[l].astype(BF16)
        n_in = w_in_l.shape[1]
        pp = in_projection(xp, mod3, w_in_l, None, F32)
        ps = in_projection(xs, mod3, w_in_l, ls, BF16)
        pp3 = pp.reshape(bp, lp, n_in)
        ps3 = ps.reshape(bs, ls, n_in)
        if l % 2 == 0:
            pool_w, pool_scale, diff_lam, diff_subln = even_p[l]
            pool_w = pool_w.astype(BF16)
            ya_p = pool_mixer(pp3, pool_w, pool_scale)
            ya_s = pool_mixer(ps3, pool_w, pool_scale)
            yb_p, nk, nv = diff_attention_context(pp3, diff_lam, diff_subln, l)
            yb_s = diff_attention_latent(ps3, diff_lam, diff_subln, l, cache_k[l], cache_v[l])
        else:
            rpb, sgu_ln, sgu_w, sgu_b = odd_p[l]
            sgu_w = sgu_w.astype(BF16)
            ya_p, nk, nv = context_attention(pp3)
            ya_s = neighborhood_attention(ps3, cache_k[l], cache_v[l], rpb)
            yb_p = spatial_gating(pp, sgu_ln, sgu_w, sgu_b)
            yb_s = spatial_gating(ps, sgu_ln, sgu_w, sgu_b)
        half = ya_p.shape[-1]
        w_out_l = w_out[l].astype(BF16)
        xp = out_projection(ya_p.reshape(bp * lp, half), yb_p.reshape(bp * lp, half), xp, mod3,
                            w_out_l, ln_g[l], ln_b[l], None)
        xs = out_projection(ya_s.reshape(bs * ls, half), yb_s.reshape(bs * ls, half), xs, mod3,
                            w_out_l, ln_g[l], ln_b[l], ls)
        new_k.append(nk)
        new_v.append(nv)

    return (xp.reshape(bp, lp, D_MODEL), xs.reshape(bs, ls, D_MODEL),
            new_k[0], new_v[0], new_k[1], new_v[1], new_k[2], new_v[2], new_k[3], new_v[3])
```

```python
import functools
import math

import numpy as np
import jax
import jax.numpy as jnp
from jax import lax
from jax.experimental import pallas as pl
from jax.experimental.pallas import tpu as pltpu

F32 = jnp.float32
BF16 = jnp.bfloat16

D_MODEL = 2048
DEPTH = 4
GRID_W = 64
GRID_ROWS = 64
HEADS = 8
HEAD_DIM = 128
QK_HALF = 64
POOL_WINDOWS = (2, 4, 8, 16)
POOL_GROUP_DIM = 256
SGU_CHUNK = 128
SGU_GROUP_DIM = 256
NA_WIN_R = 8
NA_WIN_C = 16
NA_Q_ROWS = 8
NA_K_ROWS = 16
ROPE_BASE = 10000.0
LN_EPS = 1e-5
NEG_INF = -1e30
DEEPNORM_ALPHA = (2 * DEPTH) ** 0.25
LOG2E = math.log2(math.e)

VMEM_LIMIT_BYTES = 56 * 1024 * 1024
COND_ROWS = 8
CTX_ROW = 4

PROJ_TM = 1024
PROJ_TN = 1024
OUT_TM = 512
OUT_ROW_GROUP = 256
ADA_TN = 512
POOL_ROWS = 256
POOL_HALO = 16
POOL_CHUNKS_PER_ITER = 8
SGU_TM = 512
DIFF_TQ = 256
DIFF_TILES_PER_ITER = 2
PREP_ROWS = 512


def _params(*sem):
    return pltpu.CompilerParams(dimension_semantics=sem, vmem_limit_bytes=VMEM_LIMIT_BYTES)


def _silu(x):
    return x * jax.nn.sigmoid(x)


def _dot_nt(a, b):
    return lax.dot_general(a, b, (((1,), (1,)), ((), ())), preferred_element_type=F32)


def _ada_kernel(c_ref, w_ref, b_ref, o_ref):
    a = _silu(c_ref[...]).astype(BF16)
    o_ref[...] = jnp.dot(a, w_ref[...].astype(BF16), preferred_element_type=F32) + b_ref[...]


def ada_params(cond, w_mod, b_mod):
    n = w_mod.shape[1]
    return pl.pallas_call(
        _ada_kernel,
        out_shape=jax.ShapeDtypeStruct((COND_ROWS, n), F32),
        grid=(n // ADA_TN,),
        in_specs=[pl.BlockSpec((COND_ROWS, D_MODEL), lambda j: (0, 0)),
                  pl.BlockSpec((D_MODEL, ADA_TN), lambda j: (0, j)),
                  pl.BlockSpec((1, ADA_TN), lambda j: (0, j))],
        out_specs=pl.BlockSpec((COND_ROWS, ADA_TN), lambda j: (0, j)),
        compiler_params=_params("parallel"),
        name="ada_params",
    )(cond, w_mod, b_mod.reshape(1, n))


def _inproj_kernel(x_ref, sh_ref, sc_ref, w_ref, o_ref, h_scr):
    @pl.when(pl.program_id(1) == 0)
    def _():
        h = x_ref[...] * (1.0 + sc_ref[0]) + sh_ref[0]
        h_scr[...] = h.astype(BF16)

    o_ref[...] = jnp.dot(h_scr[...], w_ref[...], preferred_element_type=F32).astype(o_ref.dtype)


def in_projection(x2d, mod3, w_bf16, rows_per_cond, out_dtype):
    m = x2d.shape[0]
    n = w_bf16.shape[1]
    if rows_per_cond is None:
        cond_row = lambda i: CTX_ROW
    else:
        blocks_per_cond = rows_per_cond // PROJ_TM
        cond_row = lambda i: i // blocks_per_cond
    return pl.pallas_call(
        _inproj_kernel,
        out_shape=jax.ShapeDtypeStruct((m, n), out_dtype),
        grid=(m // PROJ_TM, n // PROJ_TN),
        in_specs=[pl.BlockSpec((PROJ_TM, D_MODEL), lambda i, j: (i, 0)),
                  pl.BlockSpec((1, 1, D_MODEL), lambda i, j: (cond_row(i), 0, 0)),
                  pl.BlockSpec((1, 1, D_MODEL), lambda i, j: (cond_row(i), 0, 1)),
                  pl.BlockSpec((D_MODEL, PROJ_TN), lambda i, j: (0, j))],
        out_specs=pl.BlockSpec((PROJ_TM, PROJ_TN), lambda i, j: (i, j)),
        scratch_shapes=[pltpu.VMEM((PROJ_TM, D_MODEL), BF16)],
        compiler_params=_params("parallel", "arbitrary"),
        name="in_projection",
    )(x2d, mod3, mod3, w_bf16)


def _outproj_kernel(ya_ref, yb_ref, x_ref, g_ref, w_ref, lng_ref, lnb_ref, o_ref):
    half = ya_ref.shape[1]
    for r in range(OUT_TM // OUT_ROW_GROUP):
        rows = slice(r * OUT_ROW_GROUP, (r + 1) * OUT_ROW_GROUP)
        acc = jnp.dot(ya_ref[rows, :], w_ref[0:half, :], preferred_element_type=F32)
        acc = acc + jnp.dot(yb_ref[rows, :], w_ref[half:, :], preferred_element_type=F32)
        z = DEEPNORM_ALPHA * x_ref[rows, :] + g_ref[0] * acc
        mu = jnp.mean(z, axis=-1, keepdims=True)
        zc = z - mu
        var = jnp.mean(zc * zc, axis=-1, keepdims=True)
        o_ref[rows, :] = zc * lax.rsqrt(var + LN_EPS) * lng_ref[...] + lnb_ref[...]


def out_projection(ya, yb, x2d, mod3, w_bf16, ln_g, ln_b, rows_per_cond):
    m = x2d.shape[0]
    half = ya.shape[1]
    if rows_per_cond is None:
        cond_row = lambda i: CTX_ROW
    else:
        blocks_per_cond = rows_per_cond // OUT_TM
        cond_row = lambda i: i // blocks_per_cond
    return pl.pallas_call(
        _outproj_kernel,
        out_shape=jax.ShapeDtypeStruct((m, D_MODEL), F32),
        grid=(m // OUT_TM,),
        in_specs=[pl.BlockSpec((OUT_TM, half), lambda i: (i, 0)),
                  pl.BlockSpec((OUT_TM, half), lambda i: (i, 0)),
                  pl.BlockSpec((OUT_TM, D_MODEL), lambda i: (i, 0)),
                  pl.BlockSpec((1, 1, D_MODEL), lambda i: (cond_row(i), 0, 2)),
                  pl.BlockSpec((2 * half, D_MODEL), lambda i: (0, 0)),
                  pl.BlockSpec((1, D_MODEL), lambda i: (0, 0)),
                  pl.BlockSpec((1, D_MODEL), lambda i: (0, 0))],
        out_specs=pl.BlockSpec((OUT_TM, D_MODEL), lambda i: (i, 0)),
        compiler_params=_params("parallel"),
        name="out_projection",
    )(ya, yb, x2d, mod3, w_bf16, ln_g.reshape(1, D_MODEL), ln_b.reshape(1, D_MODEL))


def _pool_bands():
    assert max(POOL_WINDOWS) // 2 <= POOL_HALO
    i = np.arange(POOL_ROWS)[:, None]
    own = np.arange(POOL_ROWS)[None, :]
    mains = [(own >= i - w // 2) & (own < i + w // 2) for w in POOL_WINDOWS]
    top_i = np.arange(POOL_HALO)[:, None]
    bot_i = POOL_ROWS - POOL_HALO + top_i
    before = np.arange(POOL_HALO)[None, :] - POOL_HALO
    after = POOL_ROWS + np.arange(POOL_HALO)[None, :]
    halos = [np.stack([before >= top_i - w // 2, after < bot_i + w // 2]) for w in POOL_WINDOWS]
    return np.stack(mains).astype(np.float32), np.stack(halos).astype(np.float32)


def _pool_kernel(a_ref, g_ref, w_ref, s_ref, bm_ref, bh_ref, o_ref, *, seq):
    n_chunks = seq // POOL_ROWS
    group = pl.program_id(1)
    half = jnp.int32(0)
    for gi, window in enumerate(POOL_WINDOWS):
        half = jnp.where(group == gi, window // 2, half)

    def chunk(bi, ci):
        r0 = pl.multiple_of(ci * POOL_ROWS, POOL_ROWS)
        main = a_ref[bi, pl.ds(r0, POOL_ROWS), :]
        win = jnp.dot(bm_ref[0], main.astype(BF16), preferred_element_type=F32)
        if n_chunks > 1:
            p0 = pl.multiple_of(jnp.maximum(r0 - POOL_HALO, 0), POOL_HALO)
            n0 = pl.multiple_of(jnp.minimum(r0 + POOL_ROWS, seq - POOL_HALO), POOL_HALO)
            prev = jnp.where(ci > 0, a_ref[bi, pl.ds(p0, POOL_HALO), :].astype(F32), 0.0).astype(BF16)
            nxt = jnp.where(ci < n_chunks - 1, a_ref[bi, pl.ds(n0, POOL_HALO), :].astype(F32), 0.0).astype(BF16)
            top = win[0:POOL_HALO] + jnp.dot(bh_ref[0, 0], prev, preferred_element_type=F32)
            bot = win[POOL_ROWS - POOL_HALO:] + jnp.dot(bh_ref[0, 1], nxt, preferred_element_type=F32)
            win = jnp.concatenate([top, win[POOL_HALO:POOL_ROWS - POOL_HALO], bot], axis=0)
        t = r0 + lax.broadcasted_iota(jnp.int32, (POOL_ROWS, 1), 0)
        cnt = (jnp.minimum(t + half, seq) - jnp.maximum(t - half, 0)).astype(F32)
        pooled = win / cnt - main.astype(F32)
        mixed = jnp.dot(pooled.astype(BF16), w_ref[0], preferred_element_type=F32)
        gate = g_ref[bi, pl.ds(r0, POOL_ROWS), :].astype(F32)
        y = mixed * s_ref[...] * _silu(gate)
        o_ref[bi, pl.ds(r0, POOL_ROWS), :] = y.astype(o_ref.dtype)

    def group_of_chunks(it, carry):
        for u in range(POOL_CHUNKS_PER_ITER):
            idx = it * POOL_CHUNKS_PER_ITER + u
            chunk(idx // n_chunks, idx % n_chunks)
        return carry

    lax.fori_loop(0, a_ref.shape[0] * n_chunks // POOL_CHUNKS_PER_ITER, group_of_chunks, 0)


def pool_mixer(p3, w_pool_bf16, pool_scale):
    b, seq, _ = p3.shape
    groups = len(POOL_WINDOWS)
    band_main, band_halo = (jnp.asarray(m, BF16) for m in _pool_bands())
    bb = max(1, POOL_CHUNKS_PER_ITER * POOL_ROWS // seq)
    return pl.pallas_call(
        functools.partial(_pool_kernel, seq=seq),
        out_shape=jax.ShapeDtypeStruct((b, seq, groups * POOL_GROUP_DIM), BF16),
        grid=(b // bb, groups),
        in_specs=[pl.BlockSpec((bb, seq, POOL_GROUP_DIM), lambda i, g: (i, 0, g)),
                  pl.BlockSpec((bb, seq, POOL_GROUP_DIM), lambda i, g: (i, 0, groups + g)),
                  pl.BlockSpec((1, POOL_GROUP_DIM, POOL_GROUP_DIM), lambda i, g: (g, 0, 0)),
                  pl.BlockSpec((1, POOL_GROUP_DIM), lambda i, g: (0, g)),
                  pl.BlockSpec((1, POOL_ROWS, POOL_ROWS), lambda i, g: (g, 0, 0)),
                  pl.BlockSpec((1, 2, POOL_HALO, POOL_HALO), lambda i, g: (g, 0, 0, 0))],
        out_specs=pl.BlockSpec((bb, seq, POOL_GROUP_DIM), lambda i, g: (i, 0, g)),
        compiler_params=_params("parallel", "parallel"),
        name="pool_mixer",
    )(p3, p3, w_pool_bf16, pool_scale.reshape(1, groups * POOL_GROUP_DIM), band_main, band_halo)


def _rope_tables():
    t = np.arange(GRID_ROWS * GRID_W)
    row = (t // GRID_W).astype(np.float64)[:, None]
    col = (t % GRID_W).astype(np.float64)[:, None]
    axis_dim = QK_HALF // 2
    inv = 1.0 / (ROPE_BASE ** (np.arange(0, axis_dim, 2, dtype=np.float64) / axis_dim))
    lane = np.arange(HEAD_DIM)
    within = lane % QK_HALF
    pos = np.where((within < axis_dim)[None, :], row, col)
    ang = pos * inv[within % (axis_dim // 2)][None, :]
    is_a = ((within % axis_dim) < axis_dim // 2)[None, :]
    cos = np.cos(ang)
    sin = np.sin(ang)
    return (cos.astype(np.float32),
            np.where(is_a, -sin, 0.0).astype(np.float32),
            np.where(is_a, 0.0, sin).astype(np.float32))


def _rotate_lanes(x, cos, sin_a, sin_b):
    shift = QK_HALF // 4
    return (x * cos + pltpu.roll(x, HEAD_DIM - shift, 1) * sin_a
            + pltpu.roll(x, shift, 1) * sin_b)


def _diff_lambda(lam_ref, lam_init):
    dl = lam_ref[...]
    return (jnp.exp(jnp.sum(dl[0:1] * dl[1:2], axis=-1, keepdims=True))
            - jnp.exp(jnp.sum(dl[2:3] * dl[3:4], axis=-1, keepdims=True)) + lam_init)


def _split_components(q):
    lane = lax.broadcasted_iota(jnp.int32, (1, HEAD_DIM), 1)
    return (jnp.where(lane < QK_HALF, q, 0.0).astype(BF16),
            jnp.where(lane >= QK_HALF, q, 0.0).astype(BF16))


def _diff_combine(e1, e2, lam, v, sub, lam_init, gate):
    w1 = 1.0 / jnp.sum(e1, axis=-1, keepdims=True)
    w2 = lam / jnp.sum(e2, axis=-1, keepdims=True)
    p = (e1 * w1 - e2 * w2).astype(BF16)
    o = jnp.dot(p, v, preferred_element_type=F32)
    o = o * lax.rsqrt(jnp.mean(o * o, axis=-1, keepdims=True) + LN_EPS)
    o = o * sub * (1.0 - lam_init)
    return o * _silu(gate)


def _diff_latent_kernel(q_ref, k_ref, v_ref, g_ref, ck_ref, cv_ref, cos_ref, sa_ref, sb_ref,
                        lam_ref, sub_ref, o_ref, kt_scr, v_scr, *, seq, tq, lam_init):
    lane = lax.broadcasted_iota(jnp.int32, (1, HEAD_DIM), 1)
    ones_col = jnp.broadcast_to(jnp.where(lane == 0, 1.0, 0.0).astype(BF16), (PREP_ROWS, HEAD_DIM))

    def stage(ci, carry):
        rows = pl.ds(pl.multiple_of(ci * PREP_ROWS, PREP_ROWS), PREP_ROWS)
        k = _rotate_lanes(k_ref[0, rows, :].astype(F32), cos_ref[rows, :], sa_ref[rows, :], sb_ref[rows, :])
        kt_scr[:, rows] = jnp.transpose(k).astype(BF16)
        v_scr[rows, 0:HEAD_DIM] = v_ref[0, rows, :].astype(BF16)
        v_scr[rows, HEAD_DIM:] = ones_col
        return carry

    lax.fori_loop(0, seq // PREP_ROWS, stage, 0)
    kt_scr[:, seq:] = jnp.transpose(ck_ref[0, 0]).astype(BF16)
    v_scr[seq:, 0:HEAD_DIM] = cv_ref[0, 0].astype(BF16)
    v_scr[seq:, HEAD_DIM:] = ones_col[0:v_scr.shape[0] - seq]
    lam = _diff_lambda(lam_ref, lam_init)

    def tiles(it, carry):
        keys_t = kt_scr[...]
        values = v_scr[...]
        rows = [pl.ds(pl.multiple_of((it * DIFF_TILES_PER_ITER + u) * tq, tq), tq)
                for u in range(DIFF_TILES_PER_ITER)]
        qs = [_split_components(
                  _rotate_lanes(q_ref[0, r, :].astype(F32), cos_ref[r, :], sa_ref[r, :], sb_ref[r, :])
                  * (QK_HALF ** -0.5 * LOG2E)) for r in rows]
        s = [[jnp.dot(qc, keys_t, preferred_element_type=F32) for qc in pair] for pair in qs]
        e = [[jnp.exp2(sc - jnp.max(sc, axis=-1, keepdims=True)).astype(BF16) for sc in pair] for pair in s]
        res = [[jnp.dot(ec, values, preferred_element_type=F32) for ec in pair] for pair in e]
        for r, (r1, r2) in zip(rows, res):
            o = (r1[:, 0:HEAD_DIM] * (1.0 / r1[:, HEAD_DIM:HEAD_DIM + 1])
                 - r2[:, 0:HEAD_DIM] * (lam / r2[:, HEAD_DIM:HEAD_DIM + 1]))
            o = o * lax.rsqrt(jnp.mean(o * o, axis=-1, keepdims=True) + LN_EPS)
            o = o * sub_ref[...] * (1.0 - lam_init)
            o_ref[0, r, :] = (o * _silu(g_ref[0, r, :].astype(F32))).astype(o_ref.dtype)
        return carry

    lax.fori_loop(0, seq // (tq * DIFF_TILES_PER_ITER), tiles, 0)


def diff_attention_latent(p3, diff_lam, diff_subln, layer_idx, ctx_k, ctx_v):
    b, seq, _ = p3.shape
    ctx_len = ctx_k.shape[2]
    tq = DIFF_TQ
    nk = seq + ctx_len
    lam_init = 0.8 - 0.6 * math.exp(-0.3 * layer_idx)
    q0, k0, v0, g0 = 2 * HEADS, 3 * HEADS, 4 * HEADS, 5 * HEADS
    const = lambda i, h: (0, 0)
    head_blk = lambda off: pl.BlockSpec((1, seq, HEAD_DIM), lambda i, h: (i, 0, off + h))
    ctx_spec = pl.BlockSpec((1, 1, ctx_len, HEAD_DIM), lambda i, h: (i, h, 0, 0))
    tab_spec = pl.BlockSpec((seq, HEAD_DIM), const, pipeline_mode=pl.Buffered(1))
    return pl.pallas_call(
        functools.partial(_diff_latent_kernel, seq=seq, tq=tq, lam_init=lam_init),
        out_shape=jax.ShapeDtypeStruct((b, seq, HEADS * HEAD_DIM), BF16),
        grid=(b, HEADS),
        in_specs=[head_blk(q0), head_blk(k0), head_blk(v0), head_blk(g0),
                  ctx_spec, ctx_spec, tab_spec, tab_spec, tab_spec,
                  pl.BlockSpec(diff_lam.shape, const),
                  pl.BlockSpec((1, HEAD_DIM), const)],
        out_specs=pl.BlockSpec((1, seq, HEAD_DIM), lambda i, h: (i, 0, h)),
        scratch_shapes=[pltpu.VMEM((HEAD_DIM, nk), BF16), pltpu.VMEM((nk, 2 * HEAD_DIM), BF16)],
        compiler_params=_params("parallel", "parallel"),
        name="diff_attention_latent",
    )(p3, p3, p3, p3, ctx_k, ctx_v, *[jnp.asarray(tab) for tab in _rope_tables()],
      diff_lam, diff_subln.reshape(1, HEAD_DIM))


def _diff_context_kernel(q_ref, k_ref, v_ref, g_ref, lam_ref, sub_ref, o_ref, kc_ref, vc_ref, *, lam_init):
    lam = _diff_lambda(lam_ref, lam_init)
    heads = [slice(h * HEAD_DIM, (h + 1) * HEAD_DIM) for h in range(HEADS)]
    seq = q_ref.shape[1]
    lane = lax.broadcasted_iota(jnp.int32, (1, HEAD_DIM), 1)
    ones_col = jnp.broadcast_to(jnp.where(lane == 0, 1.0, 0.0).astype(BF16), (seq, HEAD_DIM))
    k = [k_ref[0, :, c] for c in heads]
    v = [v_ref[0, :, c] for c in heads]
    for h in range(HEADS):
        kc_ref[0, h] = k[h]
        vc_ref[0, h] = v[h]
    qs = [_split_components(q_ref[0, :, c] * (QK_HALF ** -0.5 * LOG2E)) for c in heads]
    kb = [x.astype(BF16) for x in k]
    v_aug = [jnp.concatenate([x.astype(BF16), ones_col], axis=-1) for x in v]
    s = [[_dot_nt(qc, kb[h]) for qc in qs[h]] for h in range(HEADS)]
    e = [[jnp.exp2(sc - jnp.max(sc, axis=-1, keepdims=True)).astype(BF16) for sc in sh] for sh in s]
    r = [[jnp.dot(ec, v_aug[h], preferred_element_type=F32) for ec in e[h]] for h in range(HEADS)]
    for h, c in enumerate(heads):
        r1, r2 = r[h]
        o = (r1[:, 0:HEAD_DIM] * (1.0 / r1[:, HEAD_DIM:HEAD_DIM + 1])
             - r2[:, 0:HEAD_DIM] * (lam / r2[:, HEAD_DIM:HEAD_DIM + 1]))
        o = o * lax.rsqrt(jnp.mean(o * o, axis=-1, keepdims=True) + LN_EPS)
        o = o * sub_ref[...] * (1.0 - lam_init)
        o_ref[0, :, c] = (o * _silu(g_ref[0, :, c])).astype(o_ref.dtype)


def diff_attention_context(p3, diff_lam, diff_subln, layer_idx):
    b, seq, _ = p3.shape
    width = HEADS * HEAD_DIM
    lam_init = 0.8 - 0.6 * math.exp(-0.3 * layer_idx)
    blk = lambda c: pl.BlockSpec((1, seq, width), lambda i: (i, 0, c))
    cache_spec = pl.BlockSpec((1, HEADS, seq, HEAD_DIM), lambda i: (i, 0, 0, 0))
    return pl.pallas_call(
        functools.partial(_diff_context_kernel, lam_init=lam_init),
        out_shape=[jax.ShapeDtypeStruct((b, seq, width), BF16),
                   jax.ShapeDtypeStruct((b, HEADS, seq, HEAD_DIM), F32),
                   jax.ShapeDtypeStruct((b, HEADS, seq, HEAD_DIM), F32)],
        grid=(b,),
        in_specs=[blk(2), blk(3), blk(4), blk(5),
                  pl.BlockSpec(diff_lam.shape, lambda i: (0, 0)),
                  pl.BlockSpec((1, HEAD_DIM), lambda i: (0, 0))],
        out_specs=[pl.BlockSpec((1, seq, width), lambda i: (i, 0, 0)), cache_spec, cache_spec],
        compiler_params=_params("parallel"),
        name="diff_attention_context",
    )(p3, p3, p3, p3, diff_lam, diff_subln.reshape(1, HEAD_DIM))


def _ctx_attn_kernel(q_ref, k_ref, v_ref, g_ref, o_ref, kc_ref, vc_ref):
    heads = [slice(h * HEAD_DIM, (h + 1) * HEAD_DIM) for h in range(HEADS)]
    seq = q_ref.shape[1]
    lane = lax.broadcasted_iota(jnp.int32, (1, HEAD_DIM), 1)
    ones_col = jnp.broadcast_to(jnp.where(lane == 0, 1.0, 0.0).astype(BF16), (seq, HEAD_DIM))
    k = [k_ref[0, :, c] for c in heads]
    v = [v_ref[0, :, c] for c in heads]
    for h in range(HEADS):
        kc_ref[0, h] = k[h]
        vc_ref[0, h] = v[h]
    q = [(q_ref[0, :, c] * (HEAD_DIM ** -0.5 * LOG2E)).astype(BF16) for c in heads]
    s = [_dot_nt(q[h], k[h].astype(BF16)) for h in range(HEADS)]
    e = [jnp.exp2(sh - jnp.max(sh, axis=-1, keepdims=True)).astype(BF16) for sh in s]
    r = [jnp.dot(e[h], jnp.concatenate([v[h].astype(BF16), ones_col], axis=-1), preferred_element_type=F32)
         for h in range(HEADS)]
    for h, c in enumerate(heads):
        o = r[h][:, 0:HEAD_DIM] / r[h][:, HEAD_DIM:HEAD_DIM + 1]
        o_ref[0, :, c] = (o * _silu(g_ref[0, :, c])).astype(o_ref.dtype)


def context_attention(p3):
    b, seq, _ = p3.shape
    width = HEADS * HEAD_DIM
    blk = lambda c: pl.BlockSpec((1, seq, width), lambda i: (i, 0, c))
    cache_spec = pl.BlockSpec((1, HEADS, seq, HEAD_DIM), lambda i: (i, 0, 0, 0))
    return pl.pallas_call(
        _ctx_attn_kernel,
        out_shape=[jax.ShapeDtypeStruct((b, seq, width), BF16),
                   jax.ShapeDtypeStruct((b, HEADS, seq, HEAD_DIM), F32),
                   jax.ShapeDtypeStruct((b, HEADS, seq, HEAD_DIM), F32)],
        grid=(b,),
        in_specs=[blk(0), blk(1), blk(2), blk(3)],
        out_specs=[pl.BlockSpec((1, seq, width), lambda i: (i, 0, 0)), cache_spec, cache_spec],
        compiler_params=_params("parallel"),
        name="context_attention",
    )(p3, p3, p3, p3)


NA_TILE_Q = NA_Q_ROWS * GRID_W
NA_TILE_K = NA_K_ROWS * GRID_W
NA_TILES = GRID_ROWS // NA_Q_ROWS
NA_TILES_PER_ITER = 2
NA_CASES = 3
NA_DR = 2 * NA_WIN_R - 1


def _na_key_row_start(tile):
    return int(np.clip(NA_Q_ROWS * tile - NA_WIN_R // 2, 0, GRID_ROWS - NA_K_ROWS))


def _na_bias_layout():
    dr = np.full((NA_CASES, NA_Q_ROWS, NA_K_ROWS), NA_DR, np.int32)
    for case, tile in enumerate((0, 1, NA_TILES - 1)):
        ks = _na_key_row_start(tile)
        for qr_local in range(NA_Q_ROWS):
            qr = NA_Q_ROWS * tile + qr_local
            rs = int(np.clip(qr - NA_WIN_R // 2, 0, GRID_ROWS - NA_WIN_R))
            for kr_local in range(NA_K_ROWS):
                kr = ks + kr_local
                if rs <= kr < rs + NA_WIN_R:
                    dr[case, qr_local, kr_local] = kr - qr + NA_WIN_R - 1
    pairs = sorted({(int(a), int(b)) for a, b in zip(dr[..., 0::2].ravel(), dr[..., 1::2].ravel())})
    index = {p: i for i, p in enumerate(pairs)}
    block = np.array([[[index[(int(dr[c, q, 2 * p]), int(dr[c, q, 2 * p + 1]))]
                        for p in range(NA_K_ROWS // 2)] for q in range(NA_Q_ROWS)] for c in range(NA_CASES)])
    return pairs, block


def _na_bias_blocks(rpb):
    qc = np.arange(GRID_W)[:, None]
    kc = np.arange(GRID_W)[None, :]
    cstart = np.clip(qc - NA_WIN_C // 2, 0, GRID_W - NA_WIN_C)
    col_valid = (kc >= cstart) & (kc < cstart + NA_WIN_C)
    dc = np.clip(kc - qc + NA_WIN_C - 1, 0, 2 * NA_WIN_C - 2)
    per_dr = jnp.where(jnp.asarray(col_valid), rpb[:, :, dc], NEG_INF)
    per_dr = jnp.concatenate([per_dr, jnp.full_like(per_dr[:, :1], NEG_INF)], axis=1)
    pairs, _ = _na_bias_layout()
    left = np.array([p[0] for p in pairs])
    right = np.array([p[1] for p in pairs])
    return jnp.concatenate([per_dr[:, left], per_dr[:, right]], axis=-1)


def _na_kernel(q_ref, k_ref, v_ref, g_ref, ck_ref, cv_ref, tab_ref, o_ref,
               bias_scr, ck_scr, cv_scr, v_scr):
    b = pl.program_id(1)
    seq = v_ref.shape[1]

    @pl.when(b == 0)
    def _build_bias():
        _, block = _na_bias_layout()
        for case in range(NA_CASES):
            for qr in range(NA_Q_ROWS):
                for p in range(NA_K_ROWS // 2):
                    bias_scr[case, qr * GRID_W:(qr + 1) * GRID_W, p * 128:(p + 1) * 128] = (
                        tab_ref[0, int(block[case, qr, p])] * LOG2E)

    lane = lax.broadcasted_iota(jnp.int32, (1, HEAD_DIM), 1)
    ones_col = jnp.broadcast_to(jnp.where(lane == 0, 1.0, 0.0).astype(BF16), (PREP_ROWS, HEAD_DIM))
    ck_scr[...] = ck_ref[0, 0].astype(BF16)
    cv_scr[:, 0:HEAD_DIM] = cv_ref[0, 0].astype(BF16)
    cv_scr[:, HEAD_DIM:] = ones_col[0:cv_scr.shape[0]]

    def stage(ci, carry):
        rows = pl.ds(pl.multiple_of(ci * PREP_ROWS, PREP_ROWS), PREP_ROWS)
        v_scr[rows, 0:HEAD_DIM] = v_ref[0, rows, :]
        v_scr[rows, HEAD_DIM:] = ones_col
        return carry

    lax.fori_loop(0, seq // PREP_ROWS, stage, 0)
    half_rows = NA_TILE_Q // 2

    def tiles(it, carry):
        k_ctx = ck_scr[...]
        v_ctx = cv_scr[...]
        q, bias, keys, rows = [], [], [], []
        for u in range(NA_TILES_PER_ITER):
            t = it * NA_TILES_PER_ITER + u
            case = jnp.where(t == 0, 0, jnp.where(t == NA_TILES - 1, 2, 1))
            key_row = jnp.clip(NA_Q_ROWS * t - NA_WIN_R // 2, 0, GRID_ROWS - NA_K_ROWS)
            for i in range(2):
                r = pl.ds(pl.multiple_of(t * NA_TILE_Q + i * half_rows, half_rows), half_rows)
                rows.append(r)
                q.append((q_ref[0, r, :].astype(F32) * (HEAD_DIM ** -0.5 * LOG2E)).astype(BF16))
                bias.append((case, slice(i * half_rows, (i + 1) * half_rows)))
                keys.append(pl.ds(pl.multiple_of(key_row * GRID_W, 256), NA_TILE_K))
        s_loc = [_dot_nt(qh, k_ref[0, kk, :]) + bias_scr[c, lr, :] for qh, kk, (c, lr) in zip(q, keys, bias)]
        s_ctx = [_dot_nt(qh, k_ctx) for qh in q]
        m = [jnp.maximum(jnp.max(sl, axis=-1, keepdims=True), jnp.max(sc, axis=-1, keepdims=True))
             for sl, sc in zip(s_loc, s_ctx)]
        e_loc = [jnp.exp2(sl - mh).astype(BF16) for sl, mh in zip(s_loc, m)]
        e_ctx = [jnp.exp2(sc - mh).astype(BF16) for sc, mh in zip(s_ctx, m)]
        res = [jnp.dot(el, v_scr[kk, :], preferred_element_type=F32)
               + jnp.dot(ec, v_ctx, preferred_element_type=F32)
               for el, ec, kk in zip(e_loc, e_ctx, keys)]
        for r, rh in zip(rows, res):
            o = rh[:, 0:HEAD_DIM] / rh[:, HEAD_DIM:HEAD_DIM + 1]
            o_ref[0, r, :] = (o * _silu(g_ref[0, r, :].astype(F32))).astype(o_ref.dtype)
        return carry

    lax.fori_loop(0, NA_TILES // NA_TILES_PER_ITER, tiles, 0)


def neighborhood_attention(p3, ctx_k, ctx_v, rpb):
    b, seq, _ = p3.shape
    ctx_len = ctx_k.shape[2]
    tab = _na_bias_blocks(rpb)
    n_pairs = tab.shape[1]
    ctx_spec = pl.BlockSpec((1, 1, ctx_len, HEAD_DIM), lambda h, i: (i, h, 0, 0))
    head_blk = lambda off: pl.BlockSpec((1, seq, HEAD_DIM), lambda h, i: (i, 0, off + h))
    return pl.pallas_call(
        _na_kernel,
        out_shape=jax.ShapeDtypeStruct((b, seq, HEADS * HEAD_DIM), BF16),
        grid=(HEADS, b),
        in_specs=[head_blk(0), head_blk(HEADS), head_blk(2 * HEADS), head_blk(3 * HEADS),
                  ctx_spec, ctx_spec,
                  pl.BlockSpec((1, n_pairs, GRID_W, 2 * GRID_W), lambda h, i: (h, 0, 0, 0))],
        out_specs=pl.BlockSpec((1, seq, HEAD_DIM), lambda h, i: (i, 0, h)),
        scratch_shapes=[pltpu.VMEM((NA_CASES, NA_TILE_Q, NA_TILE_K), F32),
                        pltpu.VMEM((ctx_len, HEAD_DIM), BF16), pltpu.VMEM((ctx_len, 2 * HEAD_DIM), BF16),
                        pltpu.VMEM((seq, 2 * HEAD_DIM), BF16)],
        compiler_params=_params("arbitrary", "arbitrary"),
        name="neighborhood_attention",
    )(p3, p3, p3, p3, ctx_k, ctx_v, tab)


def _sgu_kernel(u_ref, v_ref, g_ref, ln_ref, w_ref, b_ref, o_ref):
    v = v_ref[...].astype(F32)
    mu = jnp.mean(v, axis=-1, keepdims=True)
    vc = v - mu
    var = jnp.mean(vc * vc, axis=-1, keepdims=True)
    vn = (vc * lax.rsqrt(var + LN_EPS) * ln_ref[...]).astype(BF16)
    groups = w_ref.shape[0]
    for n in range(v.shape[0] // SGU_CHUNK):
        rows = slice(n * SGU_CHUNK, (n + 1) * SGU_CHUNK)
        for g in range(groups):
            cols = slice(g * SGU_GROUP_DIM, (g + 1) * SGU_GROUP_DIM)
            s = jnp.dot(w_ref[g], vn[rows, cols], preferred_element_type=F32) + b_ref[:, g:g + 1]
            y = u_ref[rows, cols].astype(F32) * s * _silu(g_ref[rows, cols].astype(F32))
            o_ref[rows, cols] = y.astype(o_ref.dtype)


def spatial_gating(p2, sgu_ln, w_s_bf16, b_s):
    m = p2.shape[0]
    groups = w_s_bf16.shape[0]
    width = groups * SGU_GROUP_DIM
    blk = lambda c: pl.BlockSpec((SGU_TM, width), lambda i: (i, c))
    return pl.pallas_call(
        _sgu_kernel,
        out_shape=jax.ShapeDtypeStruct((m, width), BF16),
        grid=(m // SGU_TM,),
        in_specs=[blk(4), blk(5), blk(6),
                  pl.BlockSpec((1, width), lambda i: (0, 0)),
                  pl.BlockSpec((groups, SGU_CHUNK, SGU_CHUNK), lambda i: (0, 0, 0)),
                  pl.BlockSpec((SGU_CHUNK, groups), lambda i: (0, 0))],
        out_specs=pl.BlockSpec((SGU_TM, width), lambda i: (i, 0)),
        compiler_params=_params("parallel"),
        name="spatial_gating",
    )(p2, p2, p2, sgu_ln.reshape(1, width), w_s_bf16, jnp.transpose(b_s))


def kernel(x_prompt, x_sample, cache_k_l0, cache_v_l0, cache_k_l1, cache_v_l1, cache_k_l2, cache_v_l2, cache_k_l3, cache_v_l3, c, c_ctx, w_mod_0, b_mod_0, w_in_0, w_out_0, ln_g_0, ln_b_0, pool_w_0, pool_scale_0, diff_lam_0, diff_subln_0, w_mod_1, b_mod_1, w_in_1, w_out_1, ln_g_1, ln_b_1, rpb_1, sgu_ln_1, sgu_w_1, sgu_b_1, w_mod_2, b_mod_2, w_in_2, w_out_2, ln_g_2, ln_b_2, pool_w_2, pool_scale_2, diff_lam_2, diff_subln_2, w_mod_3, b_mod_3, w_in_3, w_out_3, ln_g_3, ln_b_3, rpb_3, sgu_ln_3, sgu_w_3, sgu_b_3):
    cache_k = [cache_k_l0, cache_k_l1, cache_k_l2, cache_k_l3]
    cache_v = [cache_v_l0, cache_v_l1, cache_v_l2, cache_v_l3]
    w_mod = [w_mod_0, w_mod_1, w_mod_2, w_mod_3]
    b_mod = [b_mod_0, b_mod_1, b_mod_2, b_mod_3]
    w_in = [w_in_0, w_in_1, w_in_2, w_in_3]
    w_out = [w_out_0, w_out_1, w_out_2, w_out_3]
    ln_g = [ln_g_0, ln_g_1, ln_g_2, ln_g_3]
    ln_b = [ln_b_0, ln_b_1, ln_b_2, ln_b_3]
    even_p = {0: (pool_w_0, pool_scale_0, diff_lam_0, diff_subln_0),
              2: (pool_w_2, pool_scale_2, diff_lam_2, diff_subln_2)}
    odd_p = {1: (rpb_1, sgu_ln_1, sgu_w_1, sgu_b_1),
             3: (rpb_3, sgu_ln_3, sgu_w_3, sgu_b_3)}

    bp, lp, _ = x_prompt.shape
    bs, ls, _ = x_sample.shape
    xp = x_prompt.reshape(bp * lp, D_MODEL)
    xs = x_sample.reshape(bs * ls, D_MODEL)
    cond = jnp.concatenate([c, c_ctx[None, :], jnp.zeros((COND_ROWS - bs - 1, D_MODEL), F32)], axis=0)

    new_k, new_v = [], []
    for l in range(DEPTH):
        mod3 = ada_params(cond, w_mod[l], b_mod[l]).reshape(COND_ROWS, 1, 3 * D_MODEL)
        w_in_l = w_in[l].astype(BF16)
        n_in = w_in_l.shape[1]
        pp = in_projection(xp, mod3, w_in_l, None, F32)
        ps = in_projection(xs, mod3, w_in_l, ls, BF16)
        pp3 = pp.reshape(bp, lp, n_in)
        ps3 = ps.reshape(bs, ls, n_in)
        if l % 2 == 0:
            pool_w, pool_scale, diff_lam, diff_subln = even_p[l]
            pool_w = pool_w.astype(BF16)
            ya_p = pool_mixer(pp3, pool_w, pool_scale)
            ya_s = pool_mixer(ps3, pool_w, pool_scale)
            yb_p, nk, nv = diff_attention_context(pp3, diff_lam, diff_subln, l)
            yb_s = diff_attention_latent(ps3, diff_lam, diff_subln, l, cache_k[l], cache_v[l])
        else:
            rpb, sgu_ln, sgu_w, sgu_b = odd_p[l]
            sgu_w = sgu_w.astype(BF16)
            ya_p, nk, nv = context_attention(pp3)
            ya_s = neighborhood_attention(ps3, cache_k[l], cache_v[l], rpb)
            yb_p = spatial_gating(pp, sgu_ln, sgu_w, sgu_b)
            yb_s = spatial_gating(ps, sgu_ln, sgu_w, sgu_b)
        half = ya_p.shape[-1]
        w_out_l = w_out[l].astype(BF16)
        xp = out_projection(ya_p.reshape(bp * lp, half), yb_p.reshape(bp * lp, half), xp, mod3,
                            w_out_l, ln_g[l], ln_b[l], None)
        xs = out_projection(ya_s.reshape(bs * ls, half), yb_s.reshape(bs * ls, half), xs, mod3,
                            w_out_l, ln_g[l], ln_b[l], ls)
        new_k.append(nk)
        new_v.append(nv)

    return (xp.reshape(bp, lp, D_MODEL), xs.reshape(bs, ls, D_MODEL),
            new_k[0], new_v[0], new_k[1], new_v[1], new_k[2], new_v[2], new_k[3], new_v[3])
```

```python
import functools
import math

import numpy as np
import jax
import jax.numpy as jnp
from jax import lax
from jax.experimental import pallas as pl
from jax.experimental.pallas import tpu as pltpu

F32 = jnp.float32
BF16 = jnp.bfloat16

D_MODEL = 2048
DEPTH = 4
GRID_W = 64
GRID_ROWS = 64
HEADS = 8
HEAD_DIM = 128
QK_HALF = 64
POOL_WINDOWS = (2, 4, 8, 16)
POOL_GROUP_DIM = 256
SGU_CHUNK = 128
SGU_GROUP_DIM = 256
NA_WIN_R = 8
NA_WIN_C = 16
NA_Q_ROWS = 8
NA_K_ROWS = 16
ROPE_BASE = 10000.0
LN_EPS = 1e-5
NEG_INF = -1e30
DEEPNORM_ALPHA = (2 * DEPTH) ** 0.25
LOG2E = math.log2(math.e)

VMEM_LIMIT_BYTES = 56 * 1024 * 1024
COND_ROWS = 8
CTX_ROW = 4

PROJ_TM = 1024
PROJ_TN = 1024
OUT_TM = 512
OUT_ROW_GROUP = 256
ADA_TN = 512
POOL_ROWS = 256
POOL_HALO = 16
POOL_CHUNKS_PER_ITER = 8
SGU_TM = 512
DIFF_TQ = 256
DIFF_TILES_PER_ITER = 2
PREP_ROWS = 512


def _params(*sem):
    return pltpu.CompilerParams(dimension_semantics=sem, vmem_limit_bytes=VMEM_LIMIT_BYTES)


def _silu(x):
    return x * jax.nn.sigmoid(x)


def _dot_nt(a, b):
    return lax.dot_general(a, b, (((1,), (1,)), ((), ())), preferred_element_type=F32)


def _ada_kernel(c_ref, w_ref, b_ref, o_ref):
    a = _silu(c_ref[...]).astype(BF16)
    o_ref[...] = jnp.dot(a, w_ref[...].astype(BF16), preferred_element_type=F32) + b_ref[...]


def ada_params(cond, w_mod, b_mod):
    n = w_mod.shape[1]
    return pl.pallas_call(
        _ada_kernel,
        out_shape=jax.ShapeDtypeStruct((COND_ROWS, n), F32),
        grid=(n // ADA_TN,),
        in_specs=[pl.BlockSpec((COND_ROWS, D_MODEL), lambda j: (0, 0)),
                  pl.BlockSpec((D_MODEL, ADA_TN), lambda j: (0, j)),
                  pl.BlockSpec((1, ADA_TN), lambda j: (0, j))],
        out_specs=pl.BlockSpec((COND_ROWS, ADA_TN), lambda j: (0, j)),
        compiler_params=_params("parallel"),
        name="ada_params",
    )(cond, w_mod, b_mod.reshape(1, n))


def _inproj_kernel(x_ref, sh_ref, sc_ref, w_ref, o_ref, *rest, kv_block):
    h_scr = rest[-1]
    j = pl.program_id(1)

    @pl.when(j == 0)
    def _():
        h = x_ref[...] * (1.0 + sc_ref[0]) + sh_ref[0]
        h_scr[...] = h.astype(BF16)

    acc = jnp.dot(h_scr[...], w_ref[...], preferred_element_type=F32)
    o_ref[...] = acc.astype(o_ref.dtype)
    if kv_block is not None:
        for which, cache_ref in enumerate(rest[:2]):
            @pl.when(j == kv_block + which)
            def _(cache_ref=cache_ref):
                n_batch, n_heads, seq, head_dim = cache_ref.shape
                for bi in range(n_batch):
                    for hd in range(n_heads):
                        cache_ref[bi, hd] = acc[bi * seq:(bi + 1) * seq, hd * head_dim:(hd + 1) * head_dim]


def in_projection(x2d, mod3, w_bf16, rows_per_cond, kv_block=None, cache_seq=None):
    m = x2d.shape[0]
    n = w_bf16.shape[1]
    if rows_per_cond is None:
        cond_row = lambda i: CTX_ROW
    else:
        blocks_per_cond = rows_per_cond // PROJ_TM
        cond_row = lambda i: i // blocks_per_cond
    out_shape = [jax.ShapeDtypeStruct((m, n), BF16)]
    out_specs = [pl.BlockSpec((PROJ_TM, PROJ_TN), lambda i, j: (i, j))]
    if kv_block is not None:
        assert PROJ_TN == HEADS * HEAD_DIM and PROJ_TM % cache_seq == 0
        per_block = PROJ_TM // cache_seq
        cache_spec = pl.BlockSpec((per_block, HEADS, cache_seq, HEAD_DIM), lambda i, j: (i, 0, 0, 0))
        out_shape += [jax.ShapeDtypeStruct((m // cache_seq, HEADS, cache_seq, HEAD_DIM), F32)] * 2
        out_specs += [cache_spec, cache_spec]
    outs = pl.pallas_call(
        functools.partial(_inproj_kernel, kv_block=kv_block),
        out_shape=out_shape,
        grid=(m // PROJ_TM, n // PROJ_TN),
        in_specs=[pl.BlockSpec((PROJ_TM, D_MODEL), lambda i, j: (i, 0)),
                  pl.BlockSpec((1, 1, D_MODEL), lambda i, j: (cond_row(i), 0, 0)),
                  pl.BlockSpec((1, 1, D_MODEL), lambda i, j: (cond_row(i), 0, 1)),
                  pl.BlockSpec((D_MODEL, PROJ_TN), lambda i, j: (0, j))],
        out_specs=out_specs,
        scratch_shapes=[pltpu.VMEM((PROJ_TM, D_MODEL), BF16)],
        compiler_params=_params("parallel", "arbitrary"),
        name="in_projection",
    )(x2d, mod3, mod3, w_bf16)
    return outs[0] if kv_block is None else outs


def _outproj_kernel(ya_ref, yb_ref, x_ref, g_ref, w_ref, lng_ref, lnb_ref, o_ref):
    half = ya_ref.shape[1]
    for r in range(OUT_TM // OUT_ROW_GROUP):
        rows = slice(r * OUT_ROW_GROUP, (r + 1) * OUT_ROW_GROUP)
        acc = jnp.dot(ya_ref[rows, :], w_ref[0:half, :], preferred_element_type=F32)
        acc = acc + jnp.dot(yb_ref[rows, :], w_ref[half:, :], preferred_element_type=F32)
        z = DEEPNORM_ALPHA * x_ref[rows, :] + g_ref[0] * acc
        mu = jnp.mean(z, axis=-1, keepdims=True)
        zc = z - mu
        var = jnp.mean(zc * zc, axis=-1, keepdims=True)
        o_ref[rows, :] = zc * lax.rsqrt(var + LN_EPS) * lng_ref[...] + lnb_ref[...]


def out_projection(ya, yb, x2d, mod3, w_bf16, ln_g, ln_b, rows_per_cond):
    m = x2d.shape[0]
    half = ya.shape[1]
    if rows_per_cond is None:
        cond_row = lambda i: CTX_ROW
    else:
        blocks_per_cond = rows_per_cond // OUT_TM
        cond_row = lambda i: i // blocks_per_cond
    return pl.pallas_call(
        _outproj_kernel,
        out_shape=jax.ShapeDtypeStruct((m, D_MODEL), F32),
        grid=(m // OUT_TM,),
        in_specs=[pl.BlockSpec((OUT_TM, half), lambda i: (i, 0)),
                  pl.BlockSpec((OUT_TM, half), lambda i: (i, 0)),
                  pl.BlockSpec((OUT_TM, D_MODEL), lambda i: (i, 0)),
                  pl.BlockSpec((1, 1, D_MODEL), lambda i: (cond_row(i), 0, 2)),
                  pl.BlockSpec((2 * half, D_MODEL), lambda i: (0, 0)),
                  pl.BlockSpec((1, D_MODEL), lambda i: (0, 0)),
                  pl.BlockSpec((1, D_MODEL), lambda i: (0, 0))],
        out_specs=pl.BlockSpec((OUT_TM, D_MODEL), lambda i: (i, 0)),
        compiler_params=_params("parallel"),
        name="out_projection",
    )(ya, yb, x2d, mod3, w_bf16, ln_g.reshape(1, D_MODEL), ln_b.reshape(1, D_MODEL))


def _pool_bands():
    assert max(POOL_WINDOWS) // 2 <= POOL_HALO
    i = np.arange(POOL_ROWS)[:, None]
    own = np.arange(POOL_ROWS)[None, :]
    mains = [(own >= i - w // 2) & (own < i + w // 2) for w in POOL_WINDOWS]
    top_i = np.arange(POOL_HALO)[:, None]
    bot_i = POOL_ROWS - POOL_HALO + top_i
    before = np.arange(POOL_HALO)[None, :] - POOL_HALO
    after = POOL_ROWS + np.arange(POOL_HALO)[None, :]
    halos = [np.stack([before >= top_i - w // 2, after < bot_i + w // 2]) for w in POOL_WINDOWS]
    return np.stack(mains).astype(np.float32), np.stack(halos).astype(np.float32)


def _pool_kernel(a_ref, g_ref, w_ref, s_ref, bm_ref, bh_ref, o_ref, *, seq):
    n_chunks = seq // POOL_ROWS
    group = pl.program_id(1)
    half = jnp.int32(0)
    for gi, window in enumerate(POOL_WINDOWS):
        half = jnp.where(group == gi, window // 2, half)

    def chunk(bi, ci):
        r0 = pl.multiple_of(ci * POOL_ROWS, POOL_ROWS)
        main = a_ref[bi, pl.ds(r0, POOL_ROWS), :]
        win = jnp.dot(bm_ref[0], main.astype(BF16), preferred_element_type=F32)
        if n_chunks > 1:
            p0 = pl.multiple_of(jnp.maximum(r0 - POOL_HALO, 0), POOL_HALO)
            n0 = pl.multiple_of(jnp.minimum(r0 + POOL_ROWS, seq - POOL_HALO), POOL_HALO)
            prev = jnp.where(ci > 0, a_ref[bi, pl.ds(p0, POOL_HALO), :].astype(F32), 0.0).astype(BF16)
            nxt = jnp.where(ci < n_chunks - 1, a_ref[bi, pl.ds(n0, POOL_HALO), :].astype(F32), 0.0).astype(BF16)
            top = win[0:POOL_HALO] + jnp.dot(bh_ref[0, 0], prev, preferred_element_type=F32)
            bot = win[POOL_ROWS - POOL_HALO:] + jnp.dot(bh_ref[0, 1], nxt, preferred_element_type=F32)
            win = jnp.concatenate([top, win[POOL_HALO:POOL_ROWS - POOL_HALO], bot], axis=0)
        t = r0 + lax.broadcasted_iota(jnp.int32, (POOL_ROWS, 1), 0)
        cnt = (jnp.minimum(t + half, seq) - jnp.maximum(t - half, 0)).astype(F32)
        pooled = win / cnt - main.astype(F32)
        mixed = jnp.dot(pooled.astype(BF16), w_ref[0], preferred_element_type=F32)
        gate = g_ref[bi, pl.ds(r0, POOL_ROWS), :].astype(F32)
        y = mixed * s_ref[...] * _silu(gate)
        o_ref[bi, pl.ds(r0, POOL_ROWS), :] = y.astype(o_ref.dtype)

    def group_of_chunks(it, carry):
        for u in range(POOL_CHUNKS_PER_ITER):
            idx = it * POOL_CHUNKS_PER_ITER + u
            chunk(idx // n_chunks, idx % n_chunks)
        return carry

    lax.fori_loop(0, a_ref.shape[0] * n_chunks // POOL_CHUNKS_PER_ITER, group_of_chunks, 0)


def pool_mixer(p3, w_pool_bf16, pool_scale):
    b, seq, _ = p3.shape
    groups = len(POOL_WINDOWS)
    band_main, band_halo = (jnp.asarray(m, BF16) for m in _pool_bands())
    bb = max(1, POOL_CHUNKS_PER_ITER * POOL_ROWS // seq)
    return pl.pallas_call(
        functools.partial(_pool_kernel, seq=seq),
        out_shape=jax.ShapeDtypeStruct((b, seq, groups * POOL_GROUP_DIM), BF16),
        grid=(b // bb, groups),
        in_specs=[pl.BlockSpec((bb, seq, POOL_GROUP_DIM), lambda i, g: (i, 0, g)),
                  pl.BlockSpec((bb, seq, POOL_GROUP_DIM), lambda i, g: (i, 0, groups + g)),
                  pl.BlockSpec((1, POOL_GROUP_DIM, POOL_GROUP_DIM), lambda i, g: (g, 0, 0)),
                  pl.BlockSpec((1, POOL_GROUP_DIM), lambda i, g: (0, g)),
                  pl.BlockSpec((1, POOL_ROWS, POOL_ROWS), lambda i, g: (g, 0, 0)),
                  pl.BlockSpec((1, 2, POOL_HALO, POOL_HALO), lambda i, g: (g, 0, 0, 0))],
        out_specs=pl.BlockSpec((bb, seq, POOL_GROUP_DIM), lambda i, g: (i, 0, g)),
        compiler_params=_params("parallel", "parallel"),
        name="pool_mixer",
    )(p3, p3, w_pool_bf16, pool_scale.reshape(1, groups * POOL_GROUP_DIM), band_main, band_halo)


def _rope_tables():
    t = np.arange(GRID_ROWS * GRID_W)
    row = (t // GRID_W).astype(np.float64)[:, None]
    col = (t % GRID_W).astype(np.float64)[:, None]
    axis_dim = QK_HALF // 2
    inv = 1.0 / (ROPE_BASE ** (np.arange(0, axis_dim, 2, dtype=np.float64) / axis_dim))
    lane = np.arange(HEAD_DIM)
    within = lane % QK_HALF
    pos = np.where((within < axis_dim)[None, :], row, col)
    ang = pos * inv[within % (axis_dim // 2)][None, :]
    is_a = ((within % axis_dim) < axis_dim // 2)[None, :]
    cos = np.cos(ang)
    sin = np.sin(ang)
    return (cos.astype(np.float32),
            np.where(is_a, -sin, 0.0).astype(np.float32),
            np.where(is_a, 0.0, sin).astype(np.float32))


def _rotate_lanes(x, cos, sin_a, sin_b):
    shift = QK_HALF // 4
    return (x * cos + pltpu.roll(x, HEAD_DIM - shift, 1) * sin_a
            + pltpu.roll(x, shift, 1) * sin_b)


def _diff_lambda(lam_ref, lam_init):
    dl = lam_ref[...]
    return (jnp.exp(jnp.sum(dl[0:1] * dl[1:2], axis=-1, keepdims=True))
            - jnp.exp(jnp.sum(dl[2:3] * dl[3:4], axis=-1, keepdims=True)) + lam_init)


def _split_components(q):
    lane = lax.broadcasted_iota(jnp.int32, (1, HEAD_DIM), 1)
    return (jnp.where(lane < QK_HALF, q, 0.0).astype(BF16),
            jnp.where(lane >= QK_HALF, q, 0.0).astype(BF16))


def _diff_combine(e1, e2, lam, v, sub, lam_init, gate):
    w1 = 1.0 / jnp.sum(e1, axis=-1, keepdims=True)
    w2 = lam / jnp.sum(e2, axis=-1, keepdims=True)
    p = (e1 * w1 - e2 * w2).astype(BF16)
    o = jnp.dot(p, v, preferred_element_type=F32)
    o = o * lax.rsqrt(jnp.mean(o * o, axis=-1, keepdims=True) + LN_EPS)
    o = o * sub * (1.0 - lam_init)
    return o * _silu(gate)


def _diff_latent_kernel(q_ref, k_ref, v_ref, g_ref, ck_ref, cv_ref, cos_ref, sa_ref, sb_ref,
                        lam_ref, sub_ref, o_ref, kt_scr, v_scr, *, seq, tq, lam_init):
    lane = lax.broadcasted_iota(jnp.int32, (1, HEAD_DIM), 1)
    ones_col = jnp.broadcast_to(jnp.where(lane == 0, 1.0, 0.0).astype(BF16), (PREP_ROWS, HEAD_DIM))

    def stage(ci, carry):
        rows = pl.ds(pl.multiple_of(ci * PREP_ROWS, PREP_ROWS), PREP_ROWS)
        k = _rotate_lanes(k_ref[0, rows, :].astype(F32), cos_ref[rows, :], sa_ref[rows, :], sb_ref[rows, :])
        kt_scr[:, rows] = jnp.transpose(k).astype(BF16)
        v_scr[rows, 0:HEAD_DIM] = v_ref[0, rows, :].astype(BF16)
        v_scr[rows, HEAD_DIM:] = ones_col
        return carry

    lax.fori_loop(0, seq // PREP_ROWS, stage, 0)
    kt_scr[:, seq:] = jnp.transpose(ck_ref[0, 0]).astype(BF16)
    v_scr[seq:, 0:HEAD_DIM] = cv_ref[0, 0].astype(BF16)
    v_scr[seq:, HEAD_DIM:] = ones_col[0:v_scr.shape[0] - seq]
    lam = _diff_lambda(lam_ref, lam_init)

    def tiles(it, carry):
        keys_t = kt_scr[...]
        values = v_scr[...]
        rows = [pl.ds(pl.multiple_of((it * DIFF_TILES_PER_ITER + u) * tq, tq), tq)
                for u in range(DIFF_TILES_PER_ITER)]
        qs = [_split_components(
                  _rotate_lanes(q_ref[0, r, :].astype(F32), cos_ref[r, :], sa_ref[r, :], sb_ref[r, :])
                  * (QK_HALF ** -0.5 * LOG2E)) for r in rows]
        s = [[jnp.dot(qc, keys_t, preferred_element_type=F32) for qc in pair] for pair in qs]
        e = [[jnp.exp2(sc - jnp.max(sc, axis=-1, keepdims=True)).astype(BF16) for sc in pair] for pair in s]
        res = [[jnp.dot(ec, values, preferred_element_type=F32) for ec in pair] for pair in e]
        for r, (r1, r2) in zip(rows, res):
            o = (r1[:, 0:HEAD_DIM] * (1.0 / r1[:, HEAD_DIM:HEAD_DIM + 1])
                 - r2[:, 0:HEAD_DIM] * (lam / r2[:, HEAD_DIM:HEAD_DIM + 1]))
            o = o * lax.rsqrt(jnp.mean(o * o, axis=-1, keepdims=True) + LN_EPS)
            o = o * sub_ref[...] * (1.0 - lam_init)
            o_ref[0, r, :] = (o * _silu(g_ref[0, r, :].astype(F32))).astype(o_ref.dtype)
        return carry

    lax.fori_loop(0, seq // (tq * DIFF_TILES_PER_ITER), tiles, 0)


def diff_attention_latent(p3, diff_lam, diff_subln, layer_idx, ctx_k, ctx_v):
    b, seq, _ = p3.shape
    ctx_len = ctx_k.shape[2]
    tq = DIFF_TQ
    nk = seq + ctx_len
    lam_init = 0.8 - 0.6 * math.exp(-0.3 * layer_idx)
    q0, k0, v0, g0 = 2 * HEADS, 3 * HEADS, 4 * HEADS, 5 * HEADS
    const = lambda i, h: (0, 0)
    head_blk = lambda off: pl.BlockSpec((1, seq, HEAD_DIM), lambda i, h: (i, 0, off + h))
    ctx_spec = pl.BlockSpec((1, 1, ctx_len, HEAD_DIM), lambda i, h: (i, h, 0, 0))
    tab_spec = pl.BlockSpec((seq, HEAD_DIM), const, pipeline_mode=pl.Buffered(1))
    return pl.pallas_call(
        functools.partial(_diff_latent_kernel, seq=seq, tq=tq, lam_init=lam_init),
        out_shape=jax.ShapeDtypeStruct((b, seq, HEADS * HEAD_DIM), BF16),
        grid=(b, HEADS),
        in_specs=[head_blk(q0), head_blk(k0), head_blk(v0), head_blk(g0),
                  ctx_spec, ctx_spec, tab_spec, tab_spec, tab_spec,
                  pl.BlockSpec(diff_lam.shape, const),
                  pl.BlockSpec((1, HEAD_DIM), const)],
        out_specs=pl.BlockSpec((1, seq, HEAD_DIM), lambda i, h: (i, 0, h)),
        scratch_shapes=[pltpu.VMEM((HEAD_DIM, nk), BF16), pltpu.VMEM((nk, 2 * HEAD_DIM), BF16)],
        compiler_params=_params("parallel", "parallel"),
        name="diff_attention_latent",
    )(p3, p3, p3, p3, ctx_k, ctx_v, *[jnp.asarray(tab) for tab in _rope_tables()],
      diff_lam, diff_subln.reshape(1, HEAD_DIM))


def _diff_context_kernel(q_ref, k_ref, v_ref, g_ref, lam_ref, sub_ref, o_ref, *, lam_init):
    lam = _diff_lambda(lam_ref, lam_init)
    heads = [slice(h * HEAD_DIM, (h + 1) * HEAD_DIM) for h in range(HEADS)]
    seq = q_ref.shape[1]
    lane = lax.broadcasted_iota(jnp.int32, (1, HEAD_DIM), 1)
    ones_col = jnp.broadcast_to(jnp.where(lane == 0, 1.0, 0.0).astype(BF16), (seq, HEAD_DIM))
    qs = [_split_components(q_ref[0, :, c].astype(F32) * (QK_HALF ** -0.5 * LOG2E)) for c in heads]
    kb = [k_ref[0, :, c] for c in heads]
    v_aug = [jnp.concatenate([v_ref[0, :, c], ones_col], axis=-1) for c in heads]
    s = [[_dot_nt(qc, kb[h]) for qc in qs[h]] for h in range(HEADS)]
    e = [[jnp.exp2(sc - jnp.max(sc, axis=-1, keepdims=True)).astype(BF16) for sc in sh] for sh in s]
    r = [[jnp.dot(ec, v_aug[h], preferred_element_type=F32) for ec in e[h]] for h in range(HEADS)]
    for h, c in enumerate(heads):
        r1, r2 = r[h]
        o = (r1[:, 0:HEAD_DIM] * (1.0 / r1[:, HEAD_DIM:HEAD_DIM + 1])
             - r2[:, 0:HEAD_DIM] * (lam / r2[:, HEAD_DIM:HEAD_DIM + 1]))
        o = o * lax.rsqrt(jnp.mean(o * o, axis=-1, keepdims=True) + LN_EPS)
        o = o * sub_ref[...] * (1.0 - lam_init)
        o_ref[0, :, c] = (o * _silu(g_ref[0, :, c].astype(F32))).astype(o_ref.dtype)


def diff_attention_context(p3, diff_lam, diff_subln, layer_idx):
    b, seq, _ = p3.shape
    width = HEADS * HEAD_DIM
    lam_init = 0.8 - 0.6 * math.exp(-0.3 * layer_idx)
    blk = lambda c: pl.BlockSpec((1, seq, width), lambda i: (i, 0, c))
    return pl.pallas_call(
        functools.partial(_diff_context_kernel, lam_init=lam_init),
        out_shape=jax.ShapeDtypeStruct((b, seq, width), BF16),
        grid=(b,),
        in_specs=[blk(2), blk(3), blk(4), blk(5),
                  pl.BlockSpec(diff_lam.shape, lambda i: (0, 0)),
                  pl.BlockSpec((1, HEAD_DIM), lambda i: (0, 0))],
        out_specs=pl.BlockSpec((1, seq, width), lambda i: (i, 0, 0)),
        compiler_params=_params("parallel"),
        name="diff_attention_context",
    )(p3, p3, p3, p3, diff_lam, diff_subln.reshape(1, HEAD_DIM))


def _ctx_attn_kernel(q_ref, k_ref, v_ref, g_ref, o_ref):
    heads = [slice(h * HEAD_DIM, (h + 1) * HEAD_DIM) for h in range(HEADS)]
    seq = q_ref.shape[1]
    lane = lax.broadcasted_iota(jnp.int32, (1, HEAD_DIM), 1)
    ones_col = jnp.broadcast_to(jnp.where(lane == 0, 1.0, 0.0).astype(BF16), (seq, HEAD_DIM))
    q = [(q_ref[0, :, c].astype(F32) * (HEAD_DIM ** -0.5 * LOG2E)).astype(BF16) for c in heads]
    s = [_dot_nt(q[h], k_ref[0, :, c]) for h, c in enumerate(heads)]
    e = [jnp.exp2(sh - jnp.max(sh, axis=-1, keepdims=True)).astype(BF16) for sh in s]
    r = [jnp.dot(e[h], jnp.concatenate([v_ref[0, :, c], ones_col], axis=-1), preferred_element_type=F32)
         for h, c in enumerate(heads)]
    for h, c in enumerate(heads):
        o = r[h][:, 0:HEAD_DIM] / r[h][:, HEAD_DIM:HEAD_DIM + 1]
        o_ref[0, :, c] = (o * _silu(g_ref[0, :, c].astype(F32))).astype(o_ref.dtype)


def context_attention(p3):
    b, seq, _ = p3.shape
    width = HEADS * HEAD_DIM
    blk = lambda c: pl.BlockSpec((1, seq, width), lambda i: (i, 0, c))
    return pl.pallas_call(
        _ctx_attn_kernel,
        out_shape=jax.ShapeDtypeStruct((b, seq, width), BF16),
        grid=(b,),
        in_specs=[blk(0), blk(1), blk(2), blk(3)],
        out_specs=pl.BlockSpec((1, seq, width), lambda i: (i, 0, 0)),
        compiler_params=_params("parallel"),
        name="context_attention",
    )(p3, p3, p3, p3)


NA_TILE_Q = NA_Q_ROWS * GRID_W
NA_TILE_K = NA_K_ROWS * GRID_W
NA_TILES = GRID_ROWS // NA_Q_ROWS
NA_TILES_PER_ITER = 2
NA_CASES = 3
NA_DR = 2 * NA_WIN_R - 1


def _na_key_row_start(tile):
    return int(np.clip(NA_Q_ROWS * tile - NA_WIN_R // 2, 0, GRID_ROWS - NA_K_ROWS))


def _na_bias_layout():
    dr = np.full((NA_CASES, NA_Q_ROWS, NA_K_ROWS), NA_DR, np.int32)
    for case, tile in enumerate((0, 1, NA_TILES - 1)):
        ks = _na_key_row_start(tile)
        for qr_local in range(NA_Q_ROWS):
            qr = NA_Q_ROWS * tile + qr_local
            rs = int(np.clip(qr - NA_WIN_R // 2, 0, GRID_ROWS - NA_WIN_R))
            for kr_local in range(NA_K_ROWS):
                kr = ks + kr_local
                if rs <= kr < rs + NA_WIN_R:
                    dr[case, qr_local, kr_local] = kr - qr + NA_WIN_R - 1
    pairs = sorted({(int(a), int(b)) for a, b in zip(dr[..., 0::2].ravel(), dr[..., 1::2].ravel())})
    index = {p: i for i, p in enumerate(pairs)}
    block = np.array([[[index[(int(dr[c, q, 2 * p]), int(dr[c, q, 2 * p + 1]))]
                        for p in range(NA_K_ROWS // 2)] for q in range(NA_Q_ROWS)] for c in range(NA_CASES)])
    return pairs, block


def _na_bias_blocks(rpb):
    qc = np.arange(GRID_W)[:, None]
    kc = np.arange(GRID_W)[None, :]
    cstart = np.clip(qc - NA_WIN_C // 2, 0, GRID_W - NA_WIN_C)
    col_valid = (kc >= cstart) & (kc < cstart + NA_WIN_C)
    dc = np.clip(kc - qc + NA_WIN_C - 1, 0, 2 * NA_WIN_C - 2)
    per_dr = jnp.where(jnp.asarray(col_valid), rpb[:, :, dc], NEG_INF)
    per_dr = jnp.concatenate([per_dr, jnp.full_like(per_dr[:, :1], NEG_INF)], axis=1)
    pairs, _ = _na_bias_layout()
    left = np.array([p[0] for p in pairs])
    right = np.array([p[1] for p in pairs])
    return jnp.concatenate([per_dr[:, left], per_dr[:, right]], axis=-1)


def _na_kernel(q_ref, k_ref, v_ref, g_ref, ck_ref, cv_ref, tab_ref, o_ref,
               bias_scr, ck_scr, cv_scr, v_scr):
    b = pl.program_id(1)
    seq = v_ref.shape[1]

    @pl.when(b == 0)
    def _build_bias():
        _, block = _na_bias_layout()
        for case in range(NA_CASES):
            for qr in range(NA_Q_ROWS):
                for p in range(NA_K_ROWS // 2):
                    bias_scr[case, qr * GRID_W:(qr + 1) * GRID_W, p * 128:(p + 1) * 128] = (
                        tab_ref[0, int(block[case, qr, p])] * LOG2E)

    lane = lax.broadcasted_iota(jnp.int32, (1, HEAD_DIM), 1)
    ones_col = jnp.broadcast_to(jnp.where(lane == 0, 1.0, 0.0).astype(BF16), (PREP_ROWS, HEAD_DIM))
    ck_scr[...] = ck_ref[0, 0].astype(BF16)
    cv_scr[:, 0:HEAD_DIM] = cv_ref[0, 0].astype(BF16)
    cv_scr[:, HEAD_DIM:] = ones_col[0:cv_scr.shape[0]]

    def stage(ci, carry):
        rows = pl.ds(pl.multiple_of(ci * PREP_ROWS, PREP_ROWS), PREP_ROWS)
        v_scr[rows, 0:HEAD_DIM] = v_ref[0, rows, :]
        v_scr[rows, HEAD_DIM:] = ones_col
        return carry

    lax.fori_loop(0, seq // PREP_ROWS, stage, 0)
    half_rows = NA_TILE_Q // 2

    def tiles(it, carry):
        k_ctx = ck_scr[...]
        v_ctx = cv_scr[...]
        q, bias, keys, rows = [], [], [], []
        for u in range(NA_TILES_PER_ITER):
            t = it * NA_TILES_PER_ITER + u
            case = jnp.where(t == 0, 0, jnp.where(t == NA_TILES - 1, 2, 1))
            key_row = jnp.clip(NA_Q_ROWS * t - NA_WIN_R // 2, 0, GRID_ROWS - NA_K_ROWS)
            for i in range(2):
                r = pl.ds(pl.multiple_of(t * NA_TILE_Q + i * half_rows, half_rows), half_rows)
                rows.append(r)
                q.append((q_ref[0, r, :].astype(F32) * (HEAD_DIM ** -0.5 * LOG2E)).astype(BF16))
                bias.append((case, slice(i * half_rows, (i + 1) * half_rows)))
                keys.append(pl.ds(pl.multiple_of(key_row * GRID_W, 256), NA_TILE_K))
        s_loc = [_dot_nt(qh, k_ref[0, kk, :]) + bias_scr[c, lr, :] for qh, kk, (c, lr) in zip(q, keys, bias)]
        s_ctx = [_dot_nt(qh, k_ctx) for qh in q]
        m = [jnp.maximum(jnp.max(sl, axis=-1, keepdims=True), jnp.max(sc, axis=-1, keepdims=True))
             for sl, sc in zip(s_loc, s_ctx)]
        e_loc = [jnp.exp2(sl - mh).astype(BF16) for sl, mh in zip(s_loc, m)]
        e_ctx = [jnp.exp2(sc - mh).astype(BF16) for sc, mh in zip(s_ctx, m)]
        res = [jnp.dot(el, v_scr[kk, :], preferred_element_type=F32)
               + jnp.dot(ec, v_ctx, preferred_element_type=F32)
               for el, ec, kk in zip(e_loc, e_ctx, keys)]
        for r, rh in zip(rows, res):
            o = rh[:, 0:HEAD_DIM] / rh[:, HEAD_DIM:HEAD_DIM + 1]
            o_ref[0, r, :] = (o * _silu(g_ref[0, r, :].astype(F32))).astype(o_ref.dtype)
        return carry

    lax.fori_loop(0, NA_TILES // NA_TILES_PER_ITER, tiles, 0)


def neighborhood_attention(p3, ctx_k, ctx_v, rpb):
    b, seq, _ = p3.shape
    ctx_len = ctx_k.shape[2]
    tab = _na_bias_blocks(rpb)
    n_pairs = tab.shape[1]
    ctx_spec = pl.BlockSpec((1, 1, ctx_len, HEAD_DIM), lambda h, i: (i, h, 0, 0))
    head_blk = lambda off: pl.BlockSpec((1, seq, HEAD_DIM), lambda h, i: (i, 0, off + h))
    return pl.pallas_call(
        _na_kernel,
        out_shape=jax.ShapeDtypeStruct((b, seq, HEADS * HEAD_DIM), BF16),
        grid=(HEADS, b),
        in_specs=[head_blk(0), head_blk(HEADS), head_blk(2 * HEADS), head_blk(3 * HEADS),
                  ctx_spec, ctx_spec,
                  pl.BlockSpec((1, n_pairs, GRID_W, 2 * GRID_W), lambda h, i: (h, 0, 0, 0))],
        out_specs=pl.BlockSpec((1, seq, HEAD_DIM), lambda h, i: (i, 0, h)),
        scratch_shapes=[pltpu.VMEM((NA_CASES, NA_TILE_Q, NA_TILE_K), F32),
                        pltpu.VMEM((ctx_len, HEAD_DIM), BF16), pltpu.VMEM((ctx_len, 2 * HEAD_DIM), BF16),
                        pltpu.VMEM((seq, 2 * HEAD_DIM), BF16)],
        compiler_params=_params("arbitrary", "arbitrary"),
        name="neighborhood_attention",
    )(p3, p3, p3, p3, ctx_k, ctx_v, tab)


def _sgu_kernel(u_ref, v_ref, g_ref, ln_ref, w_ref, b_ref, o_ref):
    v = v_ref[...].astype(F32)
    mu = jnp.mean(v, axis=-1, keepdims=True)
    vc = v - mu
    var = jnp.mean(vc * vc, axis=-1, keepdims=True)
    vn = (vc * lax.rsqrt(var + LN_EPS) * ln_ref[...]).astype(BF16)
    groups = w_ref.shape[0]
    for n in range(v.shape[0] // SGU_CHUNK):
        rows = slice(n * SGU_CHUNK, (n + 1) * SGU_CHUNK)
        for g in range(groups):
            cols = slice(g * SGU_GROUP_DIM, (g + 1) * SGU_GROUP_DIM)
            s = jnp.dot(w_ref[g], vn[rows, cols], preferred_element_type=F32) + b_ref[:, g:g + 1]
            y = u_ref[rows, cols].astype(F32) * s * _silu(g_ref[rows, cols].astype(F32))
            o_ref[rows, cols] = y.astype(o_ref.dtype)


def spatial_gating(p2, sgu_ln, w_s_bf16, b_s):
    m = p2.shape[0]
    groups = w_s_bf16.shape[0]
    width = groups * SGU_GROUP_DIM
    blk = lambda c: pl.BlockSpec((SGU_TM, width), lambda i: (i, c))
    return pl.pallas_call(
        _sgu_kernel,
        out_shape=jax.ShapeDtypeStruct((m, width), BF16),
        grid=(m // SGU_TM,),
        in_specs=[blk(4), blk(5), blk(6),
                  pl.BlockSpec((1, width), lambda i: (0, 0)),
                  pl.BlockSpec((groups, SGU_CHUNK, SGU_CHUNK), lambda i: (0, 0, 0)),
                  pl.BlockSpec((SGU_CHUNK, groups), lambda i: (0, 0))],
        out_specs=pl.BlockSpec((SGU_TM, width), lambda i: (i, 0)),
        compiler_params=_params("parallel"),
        name="spatial_gating",
    )(p2, p2, p2, sgu_ln.reshape(1, width), w_s_bf16, jnp.transpose(b_s))


def kernel(x_prompt, x_sample, cache_k_l0, cache_v_l0, cache_k_l1, cache_v_l1, cache_k_l2, cache_v_l2, cache_k_l3, cache_v_l3, c, c_ctx, w_mod_0, b_mod_0, w_in_0, w_out_0, ln_g_0, ln_b_0, pool_w_0, pool_scale_0, diff_lam_0, diff_subln_0, w_mod_1, b_mod_1, w_in_1, w_out_1, ln_g_1, ln_b_1, rpb_1, sgu_ln_1, sgu_w_1, sgu_b_1, w_mod_2, b_mod_2, w_in_2, w_out_2, ln_g_2, ln_b_2, pool_w_2, pool_scale_2, diff_lam_2, diff_subln_2, w_mod_3, b_mod_3, w_in_3, w_out_3, ln_g_3, ln_b_3, rpb_3, sgu_ln_3, sgu_w_3, sgu_b_3):
    cache_k = [cache_k_l0, cache_k_l1, cache_k_l2, cache_k_l3]
    cache_v = [cache_v_l0, cache_v_l1, cache_v_l2, cache_v_l3]
    w_mod = [w_mod_0, w_mod_1, w_mod_2, w_mod_3]
    b_mod = [b_mod_0, b_mod_1, b_mod_2, b_mod_3]
    w_in = [w_in_0, w_in_1, w_in_2, w_in_3]
    w_out = [w_out_0, w_out_1, w_out_2, w_out_3]
    ln_g = [ln_g_0, ln_g_1, ln_g_2, ln_g_3]
    ln_b = [ln_b_0, ln_b_1, ln_b_2, ln_b_3]
    even_p = {0: (pool_w_0, pool_scale_0, diff_lam_0, diff_subln_0),
              2: (pool_w_2, pool_scale_2, diff_lam_2, diff_subln_2)}
    odd_p = {1: (rpb_1, sgu_ln_1, sgu_w_1, sgu_b_1),
             3: (rpb_3, sgu_ln_3, sgu_w_3, sgu_b_3)}

    bp, lp, _ = x_prompt.shape
    bs, ls, _ = x_sample.shape
    xp = x_prompt.reshape(bp * lp, D_MODEL)
    xs = x_sample.reshape(bs * ls, D_MODEL)
    cond = jnp.concatenate([c, c_ctx[None, :], jnp.zeros((COND_ROWS - bs - 1, D_MODEL), F32)], axis=0)

    new_k, new_v = [], []
    for l in range(DEPTH):
        mod3 = ada_params(cond, w_mod[l], b_mod[l]).reshape(COND_ROWS, 1, 3 * D_MODEL)
        w_in_l = w_in[l].astype(BF16)
        n_in = w_in_l.shape[1]
        kv_block = (3 if l % 2 == 0 else 1) * HEADS * HEAD_DIM // PROJ_TN
        pp, nk, nv = in_projection(xp, mod3, w_in_l, None, kv_block=kv_block, cache_seq=lp)
        ps = in_projection(xs, mod3, w_in_l, ls)
        pp3 = pp.reshape(bp, lp, n_in)
        ps3 = ps.reshape(bs, ls, n_in)
        if l % 2 == 0:
            pool_w, pool_scale, diff_lam, diff_subln = even_p[l]
            pool_w = pool_w.astype(BF16)
            ya_p = pool_mixer(pp3, pool_w, pool_scale)
            ya_s = pool_mixer(ps3, pool_w, pool_scale)
            yb_p = diff_attention_context(pp3, diff_lam, diff_subln, l)
            yb_s = diff_attention_latent(ps3, diff_lam, diff_subln, l, cache_k[l], cache_v[l])
        else:
            rpb, sgu_ln, sgu_w, sgu_b = odd_p[l]
            sgu_w = sgu_w.astype(BF16)
            ya_p = context_attention(pp3)
            ya_s = neighborhood_attention(ps3, cache_k[l], cache_v[l], rpb)
            yb_p = spatial_gating(pp, sgu_ln, sgu_w, sgu_b)
            yb_s = spatial_gating(ps, sgu_ln, sgu_w, sgu_b)
        half = ya_p.shape[-1]
        w_out_l = w_out[l].astype(BF16)
        xp = out_projection(ya_p.reshape(bp * lp, half), yb_p.reshape(bp * lp, half), xp, mod3,
                            w_out_l, ln_g[l], ln_b[l], None)
        xs = out_projection(ya_s.reshape(bs * ls, half), yb_s.reshape(bs * ls, half), xs, mod3,
                            w_out_l, ln_g[l], ln_b[l], ls)
        new_k.append(nk)
        new_v.append(nv)

    return (xp.reshape(bp, lp, D_MODEL), xs.reshape(bs, ls, D_MODEL),
            new_k[0], new_v[0], new_k[1], new_v[1], new_k[2], new_v[2], new_k[3], new_v[3])
```

```python
import functools
import math

import numpy as np
import jax
import jax.numpy as jnp
from jax import lax
from jax.experimental import pallas as pl
from jax.experimental.pallas import tpu as pltpu

F32 = jnp.float32
BF16 = jnp.bfloat16

D_MODEL = 2048
DEPTH = 4
GRID_W = 64
GRID_ROWS = 64
HEADS = 8
HEAD_DIM = 128
QK_HALF = 64
POOL_WINDOWS = (2, 4, 8, 16)
POOL_GROUP_DIM = 256
SGU_CHUNK = 128
SGU_GROUP_DIM = 256
NA_WIN_R = 8
NA_WIN_C = 16
NA_Q_ROWS = 8
NA_K_ROWS = 16
ROPE_BASE = 10000.0
LN_EPS = 1e-5
NEG_INF = -1e30
DEEPNORM_ALPHA = (2 * DEPTH) ** 0.25
LOG2E = math.log2(math.e)

VMEM_LIMIT_BYTES = 56 * 1024 * 1024
COND_ROWS = 8
CTX_ROW = 4

PROJ_TM = 1024
PROJ_TN = 1024
OUT_TM = 1024
OUT_ROW_GROUP = 256
ADA_TN = 512
POOL_ROWS = 256
POOL_HALO = 16
POOL_CHUNKS_PER_ITER = 8
SGU_TM = 512
DIFF_TQ = 256
DIFF_TILES_PER_ITER = 4
PREP_ROWS = 512


def _params(*sem):
    return pltpu.CompilerParams(dimension_semantics=sem, vmem_limit_bytes=VMEM_LIMIT_BYTES)


def _silu(x):
    return x * jax.nn.sigmoid(x)


def _dot_nt(a, b):
    return lax.dot_general(a, b, (((1,), (1,)), ((), ())), preferred_element_type=F32)


def _ada_kernel(c_ref, w_ref, b_ref, o_ref):
    a = _silu(c_ref[...]).astype(BF16)
    o_ref[...] = jnp.dot(a, w_ref[...].astype(BF16), preferred_element_type=F32) + b_ref[...]


def ada_params(cond, w_mod, b_mod):
    n = w_mod.shape[1]
    return pl.pallas_call(
        _ada_kernel,
        out_shape=jax.ShapeDtypeStruct((COND_ROWS, n), F32),
        grid=(n // ADA_TN,),
        in_specs=[pl.BlockSpec((COND_ROWS, D_MODEL), lambda j: (0, 0)),
                  pl.BlockSpec((D_MODEL, ADA_TN), lambda j: (0, j)),
                  pl.BlockSpec((1, ADA_TN), lambda j: (0, j))],
        out_specs=pl.BlockSpec((COND_ROWS, ADA_TN), lambda j: (0, j)),
        compiler_params=_params("parallel"),
        name="ada_params",
    )(cond, w_mod, b_mod.reshape(1, n))


def _inproj_kernel(x_ref, sh_ref, sc_ref, w_ref, o_ref, *rest, kv_block):
    h_scr = rest[-1]
    j = pl.program_id(1)

    @pl.when(j == 0)
    def _():
        h = x_ref[...] * (1.0 + sc_ref[0]) + sh_ref[0]
        h_scr[...] = h.astype(BF16)

    acc = jnp.dot(h_scr[...], w_ref[...], preferred_element_type=F32)
    o_ref[...] = acc.astype(o_ref.dtype)
    if kv_block is not None:
        for which, cache_ref in enumerate(rest[:2]):
            @pl.when(j == kv_block + which)
            def _(cache_ref=cache_ref):
                n_batch, n_heads, seq, head_dim = cache_ref.shape
                for bi in range(n_batch):
                    for hd in range(n_heads):
                        cache_ref[bi, hd] = acc[bi * seq:(bi + 1) * seq, hd * head_dim:(hd + 1) * head_dim]


def in_projection(x2d, mod3, w_bf16, rows_per_cond, kv_block=None, cache_seq=None):
    m = x2d.shape[0]
    n = w_bf16.shape[1]
    if rows_per_cond is None:
        cond_row = lambda i: CTX_ROW
    else:
        blocks_per_cond = rows_per_cond // PROJ_TM
        cond_row = lambda i: i // blocks_per_cond
    out_shape = [jax.ShapeDtypeStruct((m, n), BF16)]
    out_specs = [pl.BlockSpec((PROJ_TM, PROJ_TN), lambda i, j: (i, j))]
    if kv_block is not None:
        assert PROJ_TN == HEADS * HEAD_DIM and PROJ_TM % cache_seq == 0
        per_block = PROJ_TM // cache_seq
        cache_spec = pl.BlockSpec((per_block, HEADS, cache_seq, HEAD_DIM), lambda i, j: (i, 0, 0, 0))
        out_shape += [jax.ShapeDtypeStruct((m // cache_seq, HEADS, cache_seq, HEAD_DIM), F32)] * 2
        out_specs += [cache_spec, cache_spec]
    outs = pl.pallas_call(
        functools.partial(_inproj_kernel, kv_block=kv_block),
        out_shape=out_shape,
        grid=(m // PROJ_TM, n // PROJ_TN),
        in_specs=[pl.BlockSpec((PROJ_TM, D_MODEL), lambda i, j: (i, 0)),
                  pl.BlockSpec((1, 1, D_MODEL), lambda i, j: (cond_row(i), 0, 0)),
                  pl.BlockSpec((1, 1, D_MODEL), lambda i, j: (cond_row(i), 0, 1)),
                  pl.BlockSpec((D_MODEL, PROJ_TN), lambda i, j: (0, j))],
        out_specs=out_specs,
        scratch_shapes=[pltpu.VMEM((PROJ_TM, D_MODEL), BF16)],
        compiler_params=_params("parallel", "arbitrary"),
        name="in_projection",
    )(x2d, mod3, mod3, w_bf16)
    return outs[0] if kv_block is None else outs


def _outproj_kernel(ya_ref, yb_ref, x_ref, g_ref, w_ref, lng_ref, lnb_ref, o_ref):
    half = ya_ref.shape[1]
    for r in range(OUT_TM // OUT_ROW_GROUP):
        rows = slice(r * OUT_ROW_GROUP, (r + 1) * OUT_ROW_GROUP)
        acc = jnp.dot(ya_ref[rows, :], w_ref[0:half, :], preferred_element_type=F32)
        acc = acc + jnp.dot(yb_ref[rows, :], w_ref[half:, :], preferred_element_type=F32)
        z = DEEPNORM_ALPHA * x_ref[rows, :] + g_ref[0] * acc
        mu = jnp.mean(z, axis=-1, keepdims=True)
        zc = z - mu
        var = jnp.mean(zc * zc, axis=-1, keepdims=True)
        o_ref[rows, :] = zc * lax.rsqrt(var + LN_EPS) * lng_ref[...] + lnb_ref[...]


def out_projection(ya, yb, x2d, mod3, w_bf16, ln_g, ln_b, rows_per_cond):
    m = x2d.shape[0]
    half = ya.shape[1]
    if rows_per_cond is None:
        cond_row = lambda i: CTX_ROW
    else:
        blocks_per_cond = rows_per_cond // OUT_TM
        cond_row = lambda i: i // blocks_per_cond
    return pl.pallas_call(
        _outproj_kernel,
        out_shape=jax.ShapeDtypeStruct((m, D_MODEL), F32),
        grid=(m // OUT_TM,),
        in_specs=[pl.BlockSpec((OUT_TM, half), lambda i: (i, 0)),
                  pl.BlockSpec((OUT_TM, half), lambda i: (i, 0)),
                  pl.BlockSpec((OUT_TM, D_MODEL), lambda i: (i, 0)),
                  pl.BlockSpec((1, 1, D_MODEL), lambda i: (cond_row(i), 0, 2)),
                  pl.BlockSpec((2 * half, D_MODEL), lambda i: (0, 0), pipeline_mode=pl.Buffered(1)),
                  pl.BlockSpec((1, D_MODEL), lambda i: (0, 0)),
                  pl.BlockSpec((1, D_MODEL), lambda i: (0, 0))],
        out_specs=pl.BlockSpec((OUT_TM, D_MODEL), lambda i: (i, 0)),
        compiler_params=_params("parallel"),
        name="out_projection",
    )(ya, yb, x2d, mod3, w_bf16, ln_g.reshape(1, D_MODEL), ln_b.reshape(1, D_MODEL))


def _pool_bands():
    assert max(POOL_WINDOWS) // 2 <= POOL_HALO
    i = np.arange(POOL_ROWS)[:, None]
    own = np.arange(POOL_ROWS)[None, :]
    mains = [(own >= i - w // 2) & (own < i + w // 2) for w in POOL_WINDOWS]
    top_i = np.arange(POOL_HALO)[:, None]
    bot_i = POOL_ROWS - POOL_HALO + top_i
    before = np.arange(POOL_HALO)[None, :] - POOL_HALO
    after = POOL_ROWS + np.arange(POOL_HALO)[None, :]
    halos = [np.stack([before >= top_i - w // 2, after < bot_i + w // 2]) for w in POOL_WINDOWS]
    return np.stack(mains).astype(np.float32), np.stack(halos).astype(np.float32)


def _pool_kernel(a_ref, g_ref, w_ref, s_ref, bm_ref, bh_ref, o_ref, *, seq):
    n_chunks = seq // POOL_ROWS
    group = pl.program_id(1)
    half = jnp.int32(0)
    for gi, window in enumerate(POOL_WINDOWS):
        half = jnp.where(group == gi, window // 2, half)

    def chunk(bi, ci):
        r0 = pl.multiple_of(ci * POOL_ROWS, POOL_ROWS)
        main = a_ref[bi, pl.ds(r0, POOL_ROWS), :]
        win = jnp.dot(bm_ref[0], main.astype(BF16), preferred_element_type=F32)
        if n_chunks > 1:
            p0 = pl.multiple_of(jnp.maximum(r0 - POOL_HALO, 0), POOL_HALO)
            n0 = pl.multiple_of(jnp.minimum(r0 + POOL_ROWS, seq - POOL_HALO), POOL_HALO)
            prev = jnp.where(ci > 0, a_ref[bi, pl.ds(p0, POOL_HALO), :].astype(F32), 0.0).astype(BF16)
            nxt = jnp.where(ci < n_chunks - 1, a_ref[bi, pl.ds(n0, POOL_HALO), :].astype(F32), 0.0).astype(BF16)
            top = win[0:POOL_HALO] + jnp.dot(bh_ref[0, 0], prev, preferred_element_type=F32)
            bot = win[POOL_ROWS - POOL_HALO:] + jnp.dot(bh_ref[0, 1], nxt, preferred_element_type=F32)
            win = jnp.concatenate([top, win[POOL_HALO:POOL_ROWS - POOL_HALO], bot], axis=0)
        t = r0 + lax.broadcasted_iota(jnp.int32, (POOL_ROWS, 1), 0)
        cnt = (jnp.minimum(t + half, seq) - jnp.maximum(t - half, 0)).astype(F32)
        pooled = win / cnt - main.astype(F32)
        mixed = jnp.dot(pooled.astype(BF16), w_ref[0], preferred_element_type=F32)
        gate = g_ref[bi, pl.ds(r0, POOL_ROWS), :].astype(F32)
        y = mixed * s_ref[...] * _silu(gate)
        o_ref[bi, pl.ds(r0, POOL_ROWS), :] = y.astype(o_ref.dtype)

    def group_of_chunks(it, carry):
        for u in range(POOL_CHUNKS_PER_ITER):
            idx = it * POOL_CHUNKS_PER_ITER + u
            chunk(idx // n_chunks, idx % n_chunks)
        return carry

    lax.fori_loop(0, a_ref.shape[0] * n_chunks // POOL_CHUNKS_PER_ITER, group_of_chunks, 0)


def pool_mixer(p3, w_pool_bf16, pool_scale):
    b, seq, _ = p3.shape
    groups = len(POOL_WINDOWS)
    band_main, band_halo = (jnp.asarray(m, BF16) for m in _pool_bands())
    bb = max(1, POOL_CHUNKS_PER_ITER * POOL_ROWS // seq)
    return pl.pallas_call(
        functools.partial(_pool_kernel, seq=seq),
        out_shape=jax.ShapeDtypeStruct((b, seq, groups * POOL_GROUP_DIM), BF16),
        grid=(b // bb, groups),
        in_specs=[pl.BlockSpec((bb, seq, POOL_GROUP_DIM), lambda i, g: (i, 0, g)),
                  pl.BlockSpec((bb, seq, POOL_GROUP_DIM), lambda i, g: (i, 0, groups + g)),
                  pl.BlockSpec((1, POOL_GROUP_DIM, POOL_GROUP_DIM), lambda i, g: (g, 0, 0)),
                  pl.BlockSpec((1, POOL_GROUP_DIM), lambda i, g: (0, g)),
                  pl.BlockSpec((1, POOL_ROWS, POOL_ROWS), lambda i, g: (g, 0, 0)),
                  pl.BlockSpec((1, 2, POOL_HALO, POOL_HALO), lambda i, g: (g, 0, 0, 0))],
        out_specs=pl.BlockSpec((bb, seq, POOL_GROUP_DIM), lambda i, g: (i, 0, g)),
        compiler_params=_params("parallel", "parallel"),
        name="pool_mixer",
    )(p3, p3, w_pool_bf16, pool_scale.reshape(1, groups * POOL_GROUP_DIM), band_main, band_halo)


def _rope_tables():
    t = np.arange(GRID_ROWS * GRID_W)
    row = (t // GRID_W).astype(np.float64)[:, None]
    col = (t % GRID_W).astype(np.float64)[:, None]
    axis_dim = QK_HALF // 2
    inv = 1.0 / (ROPE_BASE ** (np.arange(0, axis_dim, 2, dtype=np.float64) / axis_dim))
    lane = np.arange(HEAD_DIM)
    within = lane % QK_HALF
    pos = np.where((within < axis_dim)[None, :], row, col)
    ang = pos * inv[within % (axis_dim // 2)][None, :]
    is_a = ((within % axis_dim) < axis_dim // 2)[None, :]
    cos = np.cos(ang)
    sin = np.sin(ang)
    return (cos.astype(np.float32),
            np.where(is_a, -sin, 0.0).astype(np.float32),
            np.where(is_a, 0.0, sin).astype(np.float32))


def _rotate_lanes(x, cos, sin_a, sin_b):
    shift = QK_HALF // 4
    return (x * cos + pltpu.roll(x, HEAD_DIM - shift, 1) * sin_a
            + pltpu.roll(x, shift, 1) * sin_b)


def _diff_lambda(lam_ref, lam_init):
    dl = lam_ref[...]
    return (jnp.exp(jnp.sum(dl[0:1] * dl[1:2], axis=-1, keepdims=True))
            - jnp.exp(jnp.sum(dl[2:3] * dl[3:4], axis=-1, keepdims=True)) + lam_init)


def _split_components(q):
    lane = lax.broadcasted_iota(jnp.int32, (1, HEAD_DIM), 1)
    return (jnp.where(lane < QK_HALF, q, 0.0).astype(BF16),
            jnp.where(lane >= QK_HALF, q, 0.0).astype(BF16))


def _diff_combine(e1, e2, lam, v, sub, lam_init, gate):
    w1 = 1.0 / jnp.sum(e1, axis=-1, keepdims=True)
    w2 = lam / jnp.sum(e2, axis=-1, keepdims=True)
    p = (e1 * w1 - e2 * w2).astype(BF16)
    o = jnp.dot(p, v, preferred_element_type=F32)
    o = o * lax.rsqrt(jnp.mean(o * o, axis=-1, keepdims=True) + LN_EPS)
    o = o * sub * (1.0 - lam_init)
    return o * _silu(gate)


def _diff_latent_kernel(q_ref, k_ref, v_ref, g_ref, ck_ref, cv_ref, cos_ref, sa_ref, sb_ref,
                        lam_ref, sub_ref, o_ref, kt_scr, v_scr, *, seq, tq, lam_init):
    lane = lax.broadcasted_iota(jnp.int32, (1, HEAD_DIM), 1)
    ones_col = jnp.broadcast_to(jnp.where(lane == 0, 1.0, 0.0).astype(BF16), (PREP_ROWS, HEAD_DIM))

    def stage(ci, carry):
        rows = pl.ds(pl.multiple_of(ci * PREP_ROWS, PREP_ROWS), PREP_ROWS)
        k = _rotate_lanes(k_ref[0, rows, :].astype(F32), cos_ref[rows, :], sa_ref[rows, :], sb_ref[rows, :])
        kt_scr[rows, :] = k.astype(BF16)
        v_scr[rows, 0:HEAD_DIM] = v_ref[0, rows, :].astype(BF16)
        v_scr[rows, HEAD_DIM:] = ones_col
        return carry

    lax.fori_loop(0, seq // PREP_ROWS, stage, 0)
    kt_scr[seq:, :] = ck_ref[0, 0].astype(BF16)
    v_scr[seq:, 0:HEAD_DIM] = cv_ref[0, 0].astype(BF16)
    v_scr[seq:, HEAD_DIM:] = ones_col[0:v_scr.shape[0] - seq]
    lam = _diff_lambda(lam_ref, lam_init)

    def tiles(it, carry):
        keys_t = kt_scr[...]
        values = v_scr[...]
        for u in range(DIFF_TILES_PER_ITER):
            r = pl.ds(pl.multiple_of((it * DIFF_TILES_PER_ITER + u) * tq, tq), tq)
            pair = _split_components(
                _rotate_lanes(q_ref[0, r, :].astype(F32), cos_ref[r, :], sa_ref[r, :], sb_ref[r, :])
                * (QK_HALF ** -0.5 * LOG2E))
            s = [_dot_nt(qc, keys_t) for qc in pair]
            e = [jnp.exp2(sc - jnp.max(sc, axis=-1, keepdims=True)).astype(BF16) for sc in s]
            r1, r2 = [jnp.dot(ec, values, preferred_element_type=F32) for ec in e]
            o = (r1[:, 0:HEAD_DIM] * (1.0 / r1[:, HEAD_DIM:HEAD_DIM + 1])
                 - r2[:, 0:HEAD_DIM] * (lam / r2[:, HEAD_DIM:HEAD_DIM + 1]))
            o = o * lax.rsqrt(jnp.mean(o * o, axis=-1, keepdims=True) + LN_EPS)
            o = o * sub_ref[...] * (1.0 - lam_init)
            o_ref[0, r, :] = (o * _silu(g_ref[0, r, :].astype(F32))).astype(o_ref.dtype)
        return carry

    lax.fori_loop(0, seq // (tq * DIFF_TILES_PER_ITER), tiles, 0)


def diff_attention_latent(p3, diff_lam, diff_subln, layer_idx, ctx_k, ctx_v):
    b, seq, _ = p3.shape
    ctx_len = ctx_k.shape[2]
    tq = DIFF_TQ
    nk = seq + ctx_len
    lam_init = 0.8 - 0.6 * math.exp(-0.3 * layer_idx)
    q0, k0, v0, g0 = 2 * HEADS, 3 * HEADS, 4 * HEADS, 5 * HEADS
    const = lambda i, h: (0, 0)
    head_blk = lambda off: pl.BlockSpec((1, seq, HEAD_DIM), lambda i, h: (i, 0, off + h))
    ctx_spec = pl.BlockSpec((1, 1, ctx_len, HEAD_DIM), lambda i, h: (i, h, 0, 0))
    tab_spec = pl.BlockSpec((seq, HEAD_DIM), const, pipeline_mode=pl.Buffered(1))
    return pl.pallas_call(
        functools.partial(_diff_latent_kernel, seq=seq, tq=tq, lam_init=lam_init),
        out_shape=jax.ShapeDtypeStruct((b, seq, HEADS * HEAD_DIM), BF16),
        grid=(b, HEADS),
        in_specs=[head_blk(q0), head_blk(k0), head_blk(v0), head_blk(g0),
                  ctx_spec, ctx_spec, tab_spec, tab_spec, tab_spec,
                  pl.BlockSpec(diff_lam.shape, const),
                  pl.BlockSpec((1, HEAD_DIM), const)],
        out_specs=pl.BlockSpec((1, seq, HEAD_DIM), lambda i, h: (i, 0, h)),
        scratch_shapes=[pltpu.VMEM((nk, HEAD_DIM), BF16), pltpu.VMEM((nk, 2 * HEAD_DIM), BF16)],
        compiler_params=_params("parallel", "parallel"),
        name="diff_attention_latent",
    )(p3, p3, p3, p3, ctx_k, ctx_v, *[jnp.asarray(tab) for tab in _rope_tables()],
      diff_lam, diff_subln.reshape(1, HEAD_DIM))


def _diff_context_kernel(q_ref, k_ref, v_ref, g_ref, lam_ref, sub_ref, o_ref, *, lam_init):
    lam = _diff_lambda(lam_ref, lam_init)
    heads = [slice(h * HEAD_DIM, (h + 1) * HEAD_DIM) for h in range(HEADS)]
    seq = q_ref.shape[1]
    lane = lax.broadcasted_iota(jnp.int32, (1, HEAD_DIM), 1)
    ones_col = jnp.broadcast_to(jnp.where(lane == 0, 1.0, 0.0).astype(BF16), (seq, HEAD_DIM))
    qs = [_split_components(q_ref[0, :, c].astype(F32) * (QK_HALF ** -0.5 * LOG2E)) for c in heads]
    kb = [k_ref[0, :, c] for c in heads]
    v_aug = [jnp.concatenate([v_ref[0, :, c], ones_col], axis=-1) for c in heads]
    s = [[_dot_nt(qc, kb[h]) for qc in qs[h]] for h in range(HEADS)]
    e = [[jnp.exp2(sc - jnp.max(sc, axis=-1, keepdims=True)).astype(BF16) for sc in sh] for sh in s]
    r = [[jnp.dot(ec, v_aug[h], preferred_element_type=F32) for ec in e[h]] for h in range(HEADS)]
    for h, c in enumerate(heads):
        r1, r2 = r[h]
        o = (r1[:, 0:HEAD_DIM] * (1.0 / r1[:, HEAD_DIM:HEAD_DIM + 1])
             - r2[:, 0:HEAD_DIM] * (lam / r2[:, HEAD_DIM:HEAD_DIM + 1]))
        o = o * lax.rsqrt(jnp.mean(o * o, axis=-1, keepdims=True) + LN_EPS)
        o = o * sub_ref[...] * (1.0 - lam_init)
        o_ref[0, :, c] = (o * _silu(g_ref[0, :, c].astype(F32))).astype(o_ref.dtype)


def diff_attention_context(p3, diff_lam, diff_subln, layer_idx):
    b, seq, _ = p3.shape
    width = HEADS * HEAD_DIM
    lam_init = 0.8 - 0.6 * math.exp(-0.3 * layer_idx)
    blk = lambda c: pl.BlockSpec((1, seq, width), lambda i: (i, 0, c))
    return pl.pallas_call(
        functools.partial(_diff_context_kernel, lam_init=lam_init),
        out_shape=jax.ShapeDtypeStruct((b, seq, width), BF16),
        grid=(b,),
        in_specs=[blk(2), blk(3), blk(4), blk(5),
                  pl.BlockSpec(diff_lam.shape, lambda i: (0, 0)),
                  pl.BlockSpec((1, HEAD_DIM), lambda i: (0, 0))],
        out_specs=pl.BlockSpec((1, seq, width), lambda i: (i, 0, 0)),
        compiler_params=_params("parallel"),
        name="diff_attention_context",
    )(p3, p3, p3, p3, diff_lam, diff_subln.reshape(1, HEAD_DIM))


def _ctx_attn_kernel(q_ref, k_ref, v_ref, g_ref, o_ref):
    heads = [slice(h * HEAD_DIM, (h + 1) * HEAD_DIM) for h in range(HEADS)]
    seq = q_ref.shape[1]
    lane = lax.broadcasted_iota(jnp.int32, (1, HEAD_DIM), 1)
    ones_col = jnp.broadcast_to(jnp.where(lane == 0, 1.0, 0.0).astype(BF16), (seq, HEAD_DIM))
    q = [(q_ref[0, :, c].astype(F32) * (HEAD_DIM ** -0.5 * LOG2E)).astype(BF16) for c in heads]
    s = [_dot_nt(q[h], k_ref[0, :, c]) for h, c in enumerate(heads)]
    e = [jnp.exp2(sh - jnp.max(sh, axis=-1, keepdims=True)).astype(BF16) for sh in s]
    r = [jnp.dot(e[h], jnp.concatenate([v_ref[0, :, c], ones_col], axis=-1), preferred_element_type=F32)
         for h, c in enumerate(heads)]
    for h, c in enumerate(heads):
        o = r[h][:, 0:HEAD_DIM] / r[h][:, HEAD_DIM:HEAD_DIM + 1]
        o_ref[0, :, c] = (o * _silu(g_ref[0, :, c].astype(F32))).astype(o_ref.dtype)


def context_attention(p3):
    b, seq, _ = p3.shape
    width = HEADS * HEAD_DIM
    blk = lambda c: pl.BlockSpec((1, seq, width), lambda i: (i, 0, c))
    return pl.pallas_call(
        _ctx_attn_kernel,
        out_shape=jax.ShapeDtypeStruct((b, seq, width), BF16),
        grid=(b,),
        in_specs=[blk(0), blk(1), blk(2), blk(3)],
        out_specs=pl.BlockSpec((1, seq, width), lambda i: (i, 0, 0)),
        compiler_params=_params("parallel"),
        name="context_attention",
    )(p3, p3, p3, p3)


NA_TILE_Q = NA_Q_ROWS * GRID_W
NA_TILE_K = NA_K_ROWS * GRID_W
NA_TILES = GRID_ROWS // NA_Q_ROWS
NA_TILES_PER_ITER = 2
NA_CASES = 3
NA_DR = 2 * NA_WIN_R - 1


def _na_key_row_start(tile):
    return int(np.clip(NA_Q_ROWS * tile - NA_WIN_R // 2, 0, GRID_ROWS - NA_K_ROWS))


def _na_bias_layout():
    dr = np.full((NA_CASES, NA_Q_ROWS, NA_K_ROWS), NA_DR, np.int32)
    for case, tile in enumerate((0, 1, NA_TILES - 1)):
        ks = _na_key_row_start(tile)
        for qr_local in range(NA_Q_ROWS):
            qr = NA_Q_ROWS * tile + qr_local
            rs = int(np.clip(qr - NA_WIN_R // 2, 0, GRID_ROWS - NA_WIN_R))
            for kr_local in range(NA_K_ROWS):
                kr = ks + kr_local
                if rs <= kr < rs + NA_WIN_R:
                    dr[case, qr_local, kr_local] = kr - qr + NA_WIN_R - 1
    pairs = sorted({(int(a), int(b)) for a, b in zip(dr[..., 0::2].ravel(), dr[..., 1::2].ravel())})
    index = {p: i for i, p in enumerate(pairs)}
    block = np.array([[[index[(int(dr[c, q, 2 * p]), int(dr[c, q, 2 * p + 1]))]
                        for p in range(NA_K_ROWS // 2)] for q in range(NA_Q_ROWS)] for c in range(NA_CASES)])
    return pairs, block


def _na_bias_blocks(rpb):
    qc = np.arange(GRID_W)[:, None]
    kc = np.arange(GRID_W)[None, :]
    cstart = np.clip(qc - NA_WIN_C // 2, 0, GRID_W - NA_WIN_C)
    col_valid = (kc >= cstart) & (kc < cstart + NA_WIN_C)
    dc = np.clip(kc - qc + NA_WIN_C - 1, 0, 2 * NA_WIN_C - 2)
    onehot = (dc.reshape(-1)[None, :] == np.arange(2 * NA_WIN_C - 1)[:, None]).astype(np.float32)
    toeplitz = jnp.einsum('hrj,jn->hrn', rpb, jnp.asarray(onehot), precision=lax.Precision.HIGHEST)
    per_dr = jnp.where(jnp.asarray(col_valid), toeplitz.reshape(rpb.shape[:2] + dc.shape), NEG_INF)
    per_dr = jnp.concatenate([per_dr, jnp.full_like(per_dr[:, :1], NEG_INF)], axis=1)
    pairs, _ = _na_bias_layout()
    left = np.array([p[0] for p in pairs])
    right = np.array([p[1] for p in pairs])
    return jnp.concatenate([per_dr[:, left], per_dr[:, right]], axis=-1)


def _na_kernel(q_ref, k_ref, v_ref, g_ref, ck_ref, cv_ref, tab_ref, o_ref,
               bias_scr, ck_scr, cv_scr, v_scr):
    b = pl.program_id(1)
    seq = v_ref.shape[1]

    @pl.when(b == 0)
    def _build_bias():
        _, block = _na_bias_layout()
        for case in range(NA_CASES):
            for qr in range(NA_Q_ROWS):
                for p in range(NA_K_ROWS // 2):
                    bias_scr[case, qr * GRID_W:(qr + 1) * GRID_W, p * 128:(p + 1) * 128] = (
                        tab_ref[0, int(block[case, qr, p])] * LOG2E)

    lane = lax.broadcasted_iota(jnp.int32, (1, HEAD_DIM), 1)
    ones_col = jnp.broadcast_to(jnp.where(lane == 0, 1.0, 0.0).astype(BF16), (PREP_ROWS, HEAD_DIM))
    ck_scr[...] = ck_ref[0, 0].astype(BF16)
    cv_scr[:, 0:HEAD_DIM] = cv_ref[0, 0].astype(BF16)
    cv_scr[:, HEAD_DIM:] = ones_col[0:cv_scr.shape[0]]

    def stage(ci, carry):
        rows = pl.ds(pl.multiple_of(ci * PREP_ROWS, PREP_ROWS), PREP_ROWS)
        v_scr[rows, 0:HEAD_DIM] = v_ref[0, rows, :]
        v_scr[rows, HEAD_DIM:] = ones_col
        return carry

    lax.fori_loop(0, seq // PREP_ROWS, stage, 0)
    half_rows = NA_TILE_Q // 2

    def tiles(it, carry):
        k_ctx = ck_scr[...]
        v_ctx = cv_scr[...]
        q, bias, keys, rows = [], [], [], []
        for u in range(NA_TILES_PER_ITER):
            t = it * NA_TILES_PER_ITER + u
            case = jnp.where(t == 0, 0, jnp.where(t == NA_TILES - 1, 2, 1))
            key_row = jnp.clip(NA_Q_ROWS * t - NA_WIN_R // 2, 0, GRID_ROWS - NA_K_ROWS)
            for i in range(2):
                r = pl.ds(pl.multiple_of(t * NA_TILE_Q + i * half_rows, half_rows), half_rows)
                rows.append(r)
                q.append((q_ref[0, r, :].astype(F32) * (HEAD_DIM ** -0.5 * LOG2E)).astype(BF16))
                bias.append((case, slice(i * half_rows, (i + 1) * half_rows)))
                keys.append(pl.ds(pl.multiple_of(key_row * GRID_W, 256), NA_TILE_K))
        s_loc = [_dot_nt(qh, k_ref[0, kk, :]) + bias_scr[c, lr, :] for qh, kk, (c, lr) in zip(q, keys, bias)]
        s_ctx = [_dot_nt(qh, k_ctx) for qh in q]
        m = [jnp.maximum(jnp.max(sl, axis=-1, keepdims=True), jnp.max(sc, axis=-1, keepdims=True))
             for sl, sc in zip(s_loc, s_ctx)]
        e_loc = [jnp.exp2(sl - mh).astype(BF16) for sl, mh in zip(s_loc, m)]
        e_ctx = [jnp.exp2(sc - mh).astype(BF16) for sc, mh in zip(s_ctx, m)]
        res = [jnp.dot(el, v_scr[kk, :], preferred_element_type=F32)
               + jnp.dot(ec, v_ctx, preferred_element_type=F32)
               for el, ec, kk in zip(e_loc, e_ctx, keys)]
        for r, rh in zip(rows, res):
            o = rh[:, 0:HEAD_DIM] / rh[:, HEAD_DIM:HEAD_DIM + 1]
            o_ref[0, r, :] = (o * _silu(g_ref[0, r, :].astype(F32))).astype(o_ref.dtype)
        return carry

    lax.fori_loop(0, NA_TILES // NA_TILES_PER_ITER, tiles, 0)


def neighborhood_attention(p3, ctx_k, ctx_v, rpb):
    b, seq, _ = p3.shape
    ctx_len = ctx_k.shape[2]
    tab = _na_bias_blocks(rpb)
    n_pairs = tab.shape[1]
    ctx_spec = pl.BlockSpec((1, 1, ctx_len, HEAD_DIM), lambda h, i: (i, h, 0, 0))
    head_blk = lambda off: pl.BlockSpec((1, seq, HEAD_DIM), lambda h, i: (i, 0, off + h))
    return pl.pallas_call(
        _na_kernel,
        out_shape=jax.ShapeDtypeStruct((b, seq, HEADS * HEAD_DIM), BF16),
        grid=(HEADS, b),
        in_specs=[head_blk(0), head_blk(HEADS), head_blk(2 * HEADS), head_blk(3 * HEADS),
                  ctx_spec, ctx_spec,
                  pl.BlockSpec((1, n_pairs, GRID_W, 2 * GRID_W), lambda h, i: (h, 0, 0, 0))],
        out_specs=pl.BlockSpec((1, seq, HEAD_DIM), lambda h, i: (i, 0, h)),
        scratch_shapes=[pltpu.VMEM((NA_CASES, NA_TILE_Q, NA_TILE_K), F32),
                        pltpu.VMEM((ctx_len, HEAD_DIM), BF16), pltpu.VMEM((ctx_len, 2 * HEAD_DIM), BF16),
                        pltpu.VMEM((seq, 2 * HEAD_DIM), BF16)],
        compiler_params=_params("arbitrary", "arbitrary"),
        name="neighborhood_attention",
    )(p3, p3, p3, p3, ctx_k, ctx_v, tab)


def _sgu_kernel(u_ref, v_ref, g_ref, ln_ref, w_ref, b_ref, o_ref):
    v = v_ref[...].astype(F32)
    mu = jnp.mean(v, axis=-1, keepdims=True)
    vc = v - mu
    var = jnp.mean(vc * vc, axis=-1, keepdims=True)
    vn = (vc * lax.rsqrt(var + LN_EPS) * ln_ref[...]).astype(BF16)
    groups = w_ref.shape[0]
    for n in range(v.shape[0] // SGU_CHUNK):
        rows = slice(n * SGU_CHUNK, (n + 1) * SGU_CHUNK)
        for g in range(groups):
            cols = slice(g * SGU_GROUP_DIM, (g + 1) * SGU_GROUP_DIM)
            s = jnp.dot(w_ref[g], vn[rows, cols], preferred_element_type=F32) + b_ref[:, g:g + 1]
            y = u_ref[rows, cols].astype(F32) * s * _silu(g_ref[rows, cols].astype(F32))
            o_ref[rows, cols] = y.astype(o_ref.dtype)


def spatial_gating(p2, sgu_ln, w_s_bf16, b_s):
    m = p2.shape[0]
    groups = w_s_bf16.shape[0]
    width = groups * SGU_GROUP_DIM
    blk = lambda c: pl.BlockSpec((SGU_TM, width), lambda i: (i, c))
    return pl.pallas_call(
        _sgu_kernel,
        out_shape=jax.ShapeDtypeStruct((m, width), BF16),
        grid=(m // SGU_TM,),
        in_specs=[blk(4), blk(5), blk(6),
                  pl.BlockSpec((1, width), lambda i: (0, 0)),
                  pl.BlockSpec((groups, SGU_CHUNK, SGU_CHUNK), lambda i: (0, 0, 0)),
                  pl.BlockSpec((SGU_CHUNK, groups), lambda i: (0, 0))],
        out_specs=pl.BlockSpec((SGU_TM, width), lambda i: (i, 0)),
        compiler_params=_params("parallel"),
        name="spatial_gating",
    )(p2, p2, p2, sgu_ln.reshape(1, width), w_s_bf16, jnp.transpose(b_s))


def kernel(x_prompt, x_sample, cache_k_l0, cache_v_l0, cache_k_l1, cache_v_l1, cache_k_l2, cache_v_l2, cache_k_l3, cache_v_l3, c, c_ctx, w_mod_0, b_mod_0, w_in_0, w_out_0, ln_g_0, ln_b_0, pool_w_0, pool_scale_0, diff_lam_0, diff_subln_0, w_mod_1, b_mod_1, w_in_1, w_out_1, ln_g_1, ln_b_1, rpb_1, sgu_ln_1, sgu_w_1, sgu_b_1, w_mod_2, b_mod_2, w_in_2, w_out_2, ln_g_2, ln_b_2, pool_w_2, pool_scale_2, diff_lam_2, diff_subln_2, w_mod_3, b_mod_3, w_in_3, w_out_3, ln_g_3, ln_b_3, rpb_3, sgu_ln_3, sgu_w_3, sgu_b_3):
    cache_k = [cache_k_l0, cache_k_l1, cache_k_l2, cache_k_l3]
    cache_v = [cache_v_l0, cache_v_l1, cache_v_l2, cache_v_l3]
    w_mod = [w_mod_0, w_mod_1, w_mod_2, w_mod_3]
    b_mod = [b_mod_0, b_mod_1, b_mod_2, b_mod_3]
    w_in = [w_in_0, w_in_1, w_in_2, w_in_3]
    w_out = [w_out_0, w_out_1, w_out_2, w_out_3]
    ln_g = [ln_g_0, ln_g_1, ln_g_2, ln_g_3]
    ln_b = [ln_b_0, ln_b_1, ln_b_2, ln_b_3]
    even_p = {0: (pool_w_0, pool_scale_0, diff_lam_0, diff_subln_0),
              2: (pool_w_2, pool_scale_2, diff_lam_2, diff_subln_2)}
    odd_p = {1: (rpb_1, sgu_ln_1, sgu_w_1, sgu_b_1),
             3: (rpb_3, sgu_ln_3, sgu_w_3, sgu_b_3)}

    bp, lp, _ = x_prompt.shape
    bs, ls, _ = x_sample.shape
    xp = x_prompt.reshape(bp * lp, D_MODEL)
    xs = x_sample.reshape(bs * ls, D_MODEL)
    cond = jnp.concatenate([c, c_ctx[None, :], jnp.zeros((COND_ROWS - bs - 1, D_MODEL), F32)], axis=0)

    new_k, new_v = [], []
    for l in range(DEPTH):
        mod3 = ada_params(cond, w_mod[l], b_mod[l]).reshape(COND_ROWS, 1, 3 * D_MODEL)
        w_in_l = w_in[l].astype(BF16)
        n_in = w_in_l.shape[1]
        kv_block = (3 if l % 2 == 0 else 1) * HEADS * HEAD_DIM // PROJ_TN
        pp, nk, nv = in_projection(xp, mod3, w_in_l, None, kv_block=kv_block, cache_seq=lp)
        ps = in_projection(xs, mod3, w_in_l, ls)
        pp3 = pp.reshape(bp, lp, n_in)
        ps3 = ps.reshape(bs, ls, n_in)
        if l % 2 == 0:
            pool_w, pool_scale, diff_lam, diff_subln = even_p[l]
            pool_w = pool_w.astype(BF16)
            ya_p = pool_mixer(pp3, pool_w, pool_scale)
            ya_s = pool_mixer(ps3, pool_w, pool_scale)
            yb_p = diff_attention_context(pp3, diff_lam, diff_subln, l)
            yb_s = diff_attention_latent(ps3, diff_lam, diff_subln, l, cache_k[l], cache_v[l])
        else:
            rpb, sgu_ln, sgu_w, sgu_b = odd_p[l]
            sgu_w = sgu_w.astype(BF16)
            ya_p = context_attention(pp3)
            ya_s = neighborhood_attention(ps3, cache_k[l], cache_v[l], rpb)
            yb_p = spatial_gating(pp, sgu_ln, sgu_w, sgu_b)
            yb_s = spatial_gating(ps, sgu_ln, sgu_w, sgu_b)
        half = ya_p.shape[-1]
        w_out_l = w_out[l].astype(BF16)
        xp = out_projection(ya_p.reshape(bp * lp, half), yb_p.reshape(bp * lp, half), xp, mod3,
                            w_out_l, ln_g[l], ln_b[l], None)
        xs = out_projection(ya_s.reshape(bs * ls, half), yb_s.reshape(bs * ls, half), xs, mod3,
                            w_out_l, ln_g[l], ln_b[l], ls)
        new_k.append(nk)
        new_v.append(nv)

    return (xp.reshape(bp, lp, D_MODEL), xs.reshape(bs, ls, D_MODEL),
            new_k[0], new_v[0], new_k[1], new_v[1], new_k[2], new_v[2], new_k[3], new_v[3])
```

```python
import functools
import math

import numpy as np
import jax
import jax.numpy as jnp
from jax import lax
from jax.experimental import pallas as pl
from jax.experimental.pallas import tpu as pltpu

F32 = jnp.float32
BF16 = jnp.bfloat16

D_MODEL = 2048
DEPTH = 4
GRID_W = 64
GRID_ROWS = 64
HEADS = 8
HEAD_DIM = 128
QK_HALF = 64
POOL_WINDOWS = (2, 4, 8, 16)
POOL_GROUP_DIM = 256
SGU_CHUNK = 128
SGU_GROUP_DIM = 256
NA_WIN_R = 8
NA_WIN_C = 16
NA_Q_ROWS = 8
NA_K_ROWS = 16
ROPE_BASE = 10000.0
LN_EPS = 1e-5
NEG_INF = -1e30
DEEPNORM_ALPHA = (2 * DEPTH) ** 0.25
LOG2E = math.log2(math.e)

VMEM_LIMIT_BYTES = 56 * 1024 * 1024
COND_ROWS = 8
CTX_ROW = 4

PROJ_TM = 1024
PROJ_TN = 1024
PROJ_TN_WIDE = (2048, 1792)
OUT_TM = 1024
OUT_ROW_GROUP = 256
ADA_TN = 512
POOL_ROWS = 256
POOL_HALO = 16
POOL_CHUNKS_PER_ITER = 8
SGU_TM = 512
DIFF_TQ = 256
DIFF_TILES_PER_ITER = 4
PREP_ROWS = 512


def _params(*sem):
    return pltpu.CompilerParams(dimension_semantics=sem, vmem_limit_bytes=VMEM_LIMIT_BYTES)


def _silu(x):
    return x * jax.nn.sigmoid(x)


def _dot_nt(a, b):
    return lax.dot_general(a, b, (((1,), (1,)), ((), ())), preferred_element_type=F32)


def _ada_kernel(c_ref, w_ref, b_ref, o_ref):
    a = _silu(c_ref[...]).astype(BF16)
    o_ref[...] = jnp.dot(a, w_ref[...].astype(BF16), preferred_element_type=F32) + b_ref[...]


def ada_params(cond, w_mod, b_mod):
    n = w_mod.shape[1]
    return pl.pallas_call(
        _ada_kernel,
        out_shape=jax.ShapeDtypeStruct((COND_ROWS, n), F32),
        grid=(n // ADA_TN,),
        in_specs=[pl.BlockSpec((COND_ROWS, D_MODEL), lambda j: (0, 0)),
                  pl.BlockSpec((D_MODEL, ADA_TN), lambda j: (0, j)),
                  pl.BlockSpec((1, ADA_TN), lambda j: (0, j))],
        out_specs=pl.BlockSpec((COND_ROWS, ADA_TN), lambda j: (0, j)),
        compiler_params=_params("parallel"),
        name="ada_params",
    )(cond, w_mod, b_mod.reshape(1, n))


def _inproj_kernel(x_ref, sh_ref, sc_ref, w_ref, o_ref, *rest, kv_block):
    h_scr = rest[-1]
    j = pl.program_id(1)

    @pl.when(j == 0)
    def _():
        h = x_ref[...] * (1.0 + sc_ref[0]) + sh_ref[0]
        h_scr[...] = h.astype(BF16)

    acc = jnp.dot(h_scr[...], w_ref[...], preferred_element_type=F32)
    o_ref[...] = acc.astype(o_ref.dtype)
    if kv_block is not None:
        for which, cache_ref in enumerate(rest[:2]):
            @pl.when(j == kv_block + which)
            def _(cache_ref=cache_ref):
                n_batch, n_heads, seq, head_dim = cache_ref.shape
                for bi in range(n_batch):
                    for hd in range(n_heads):
                        cache_ref[bi, hd] = acc[bi * seq:(bi + 1) * seq, hd * head_dim:(hd + 1) * head_dim]


def in_projection(x2d, mod3, w_bf16, rows_per_cond, kv_block=None, cache_seq=None):
    m = x2d.shape[0]
    n = w_bf16.shape[1]
    if rows_per_cond is None:
        cond_row = lambda i: CTX_ROW
    else:
        blocks_per_cond = rows_per_cond // PROJ_TM
        cond_row = lambda i: i // blocks_per_cond
    tn = PROJ_TN
    if kv_block is None:
        tn = next((c for c in PROJ_TN_WIDE if n % c == 0), PROJ_TN)
    out_shape = [jax.ShapeDtypeStruct((m, n), BF16)]
    out_specs = [pl.BlockSpec((PROJ_TM, tn), lambda i, j: (i, j))]
    if kv_block is not None:
        assert PROJ_TN == HEADS * HEAD_DIM and PROJ_TM % cache_seq == 0
        per_block = PROJ_TM // cache_seq
        cache_spec = pl.BlockSpec((per_block, HEADS, cache_seq, HEAD_DIM), lambda i, j: (i, 0, 0, 0))
        out_shape += [jax.ShapeDtypeStruct((m // cache_seq, HEADS, cache_seq, HEAD_DIM), F32)] * 2
        out_specs += [cache_spec, cache_spec]
    outs = pl.pallas_call(
        functools.partial(_inproj_kernel, kv_block=kv_block),
        out_shape=out_shape,
        grid=(m // PROJ_TM, n // tn),
        in_specs=[pl.BlockSpec((PROJ_TM, D_MODEL), lambda i, j: (i, 0)),
                  pl.BlockSpec((1, 1, D_MODEL), lambda i, j: (cond_row(i), 0, 0)),
                  pl.BlockSpec((1, 1, D_MODEL), lambda i, j: (cond_row(i), 0, 1)),
                  pl.BlockSpec((D_MODEL, tn), lambda i, j: (0, j))],
        out_specs=out_specs,
        scratch_shapes=[pltpu.VMEM((PROJ_TM, D_MODEL), BF16)],
        compiler_params=_params("parallel", "arbitrary"),
        name="in_projection",
    )(x2d, mod3, mod3, w_bf16)
    return outs[0] if kv_block is None else outs


def _outproj_kernel(ya_ref, yb_ref, x_ref, g_ref, w_ref, lng_ref, lnb_ref, o_ref):
    half = ya_ref.shape[1]
    for r in range(OUT_TM // OUT_ROW_GROUP):
        rows = slice(r * OUT_ROW_GROUP, (r + 1) * OUT_ROW_GROUP)
        acc = jnp.dot(ya_ref[rows, :], w_ref[0:half, :], preferred_element_type=F32)
        acc = acc + jnp.dot(yb_ref[rows, :], w_ref[half:, :], preferred_element_type=F32)
        z = DEEPNORM_ALPHA * x_ref[rows, :] + g_ref[0] * acc
        mu = jnp.mean(z, axis=-1, keepdims=True)
        zc = z - mu
        var = jnp.mean(zc * zc, axis=-1, keepdims=True)
        o_ref[rows, :] = zc * lax.rsqrt(var + LN_EPS) * lng_ref[...] + lnb_ref[...]


def out_projection(ya, yb, x2d, mod3, w_bf16, ln_g, ln_b, rows_per_cond):
    m = x2d.shape[0]
    half = ya.shape[1]
    if rows_per_cond is None:
        cond_row = lambda i: CTX_ROW
    else:
        blocks_per_cond = rows_per_cond // OUT_TM
        cond_row = lambda i: i // blocks_per_cond
    return pl.pallas_call(
        _outproj_kernel,
        out_shape=jax.ShapeDtypeStruct((m, D_MODEL), F32),
        grid=(m // OUT_TM,),
        in_specs=[pl.BlockSpec((OUT_TM, half), lambda i: (i, 0)),
                  pl.BlockSpec((OUT_TM, half), lambda i: (i, 0)),
                  pl.BlockSpec((OUT_TM, D_MODEL), lambda i: (i, 0)),
                  pl.BlockSpec((1, 1, D_MODEL), lambda i: (cond_row(i), 0, 2)),
                  pl.BlockSpec((2 * half, D_MODEL), lambda i: (0, 0), pipeline_mode=pl.Buffered(1)),
                  pl.BlockSpec((1, D_MODEL), lambda i: (0, 0)),
                  pl.BlockSpec((1, D_MODEL), lambda i: (0, 0))],
        out_specs=pl.BlockSpec((OUT_TM, D_MODEL), lambda i: (i, 0)),
        compiler_params=_params("parallel"),
        name="out_projection",
    )(ya, yb, x2d, mod3, w_bf16, ln_g.reshape(1, D_MODEL), ln_b.reshape(1, D_MODEL))


def _pool_bands():
    assert max(POOL_WINDOWS) // 2 <= POOL_HALO
    i = np.arange(POOL_ROWS)[:, None]
    own = np.arange(POOL_ROWS)[None, :]
    mains = [(own >= i - w // 2) & (own < i + w // 2) for w in POOL_WINDOWS]
    top_i = np.arange(POOL_HALO)[:, None]
    bot_i = POOL_ROWS - POOL_HALO + top_i
    before = np.arange(POOL_HALO)[None, :] - POOL_HALO
    after = POOL_ROWS + np.arange(POOL_HALO)[None, :]
    halos = [np.stack([before >= top_i - w // 2, after < bot_i + w // 2]) for w in POOL_WINDOWS]
    return np.stack(mains).astype(np.float32), np.stack(halos).astype(np.float32)


def _pool_kernel(a_ref, g_ref, w_ref, s_ref, bm_ref, bh_ref, o_ref, *, seq):
    n_chunks = seq // POOL_ROWS
    group = pl.program_id(1)
    half = jnp.int32(0)
    for gi, window in enumerate(POOL_WINDOWS):
        half = jnp.where(group == gi, window // 2, half)

    def chunk(bi, ci):
        r0 = pl.multiple_of(ci * POOL_ROWS, POOL_ROWS)
        main = a_ref[bi, pl.ds(r0, POOL_ROWS), :]
        win = jnp.dot(bm_ref[0], main.astype(BF16), preferred_element_type=F32)
        if n_chunks > 1:
            p0 = pl.multiple_of(jnp.maximum(r0 - POOL_HALO, 0), POOL_HALO)
            n0 = pl.multiple_of(jnp.minimum(r0 + POOL_ROWS, seq - POOL_HALO), POOL_HALO)
            prev = jnp.where(ci > 0, a_ref[bi, pl.ds(p0, POOL_HALO), :].astype(F32), 0.0).astype(BF16)
            nxt = jnp.where(ci < n_chunks - 1, a_ref[bi, pl.ds(n0, POOL_HALO), :].astype(F32), 0.0).astype(BF16)
            top = win[0:POOL_HALO] + jnp.dot(bh_ref[0, 0], prev, preferred_element_type=F32)
            bot = win[POOL_ROWS - POOL_HALO:] + jnp.dot(bh_ref[0, 1], nxt, preferred_element_type=F32)
            win = jnp.concatenate([top, win[POOL_HALO:POOL_ROWS - POOL_HALO], bot], axis=0)
        t = r0 + lax.broadcasted_iota(jnp.int32, (POOL_ROWS, 1), 0)
        cnt = (jnp.minimum(t + half, seq) - jnp.maximum(t - half, 0)).astype(F32)
        pooled = win / cnt - main.astype(F32)
        mixed = jnp.dot(pooled.astype(BF16), w_ref[0], preferred_element_type=F32)
        gate = g_ref[bi, pl.ds(r0, POOL_ROWS), :].astype(F32)
        y = mixed * s_ref[...] * _silu(gate)
        o_ref[bi, pl.ds(r0, POOL_ROWS), :] = y.astype(o_ref.dtype)

    def group_of_chunks(it, carry):
        for u in range(POOL_CHUNKS_PER_ITER):
            idx = it * POOL_CHUNKS_PER_ITER + u
            chunk(idx // n_chunks, idx % n_chunks)
        return carry

    lax.fori_loop(0, a_ref.shape[0] * n_chunks // POOL_CHUNKS_PER_ITER, group_of_chunks, 0)


def pool_mixer(p3, w_pool_bf16, pool_scale):
    b, seq, _ = p3.shape
    groups = len(POOL_WINDOWS)
    band_main, band_halo = (jnp.asarray(m, BF16) for m in _pool_bands())
    bb = max(1, POOL_CHUNKS_PER_ITER * POOL_ROWS // seq)
    return pl.pallas_call(
        functools.partial(_pool_kernel, seq=seq),
        out_shape=jax.ShapeDtypeStruct((b, seq, groups * POOL_GROUP_DIM), BF16),
        grid=(b // bb, groups),
        in_specs=[pl.BlockSpec((bb, seq, POOL_GROUP_DIM), lambda i, g: (i, 0, g)),
                  pl.BlockSpec((bb, seq, POOL_GROUP_DIM), lambda i, g: (i, 0, groups + g)),
                  pl.BlockSpec((1, POOL_GROUP_DIM, POOL_GROUP_DIM), lambda i, g: (g, 0, 0)),
                  pl.BlockSpec((1, POOL_GROUP_DIM), lambda i, g: (0, g)),
                  pl.BlockSpec((1, POOL_ROWS, POOL_ROWS), lambda i, g: (g, 0, 0)),
                  pl.BlockSpec((1, 2, POOL_HALO, POOL_HALO), lambda i, g: (g, 0, 0, 0))],
        out_specs=pl.BlockSpec((bb, seq, POOL_GROUP_DIM), lambda i, g: (i, 0, g)),
        compiler_params=_params("parallel", "parallel"),
        name="pool_mixer",
    )(p3, p3, w_pool_bf16, pool_scale.reshape(1, groups * POOL_GROUP_DIM), band_main, band_halo)


def _rope_tables():
    t = np.arange(GRID_ROWS * GRID_W)
    row = (t // GRID_W).astype(np.float64)[:, None]
    col = (t % GRID_W).astype(np.float64)[:, None]
    axis_dim = QK_HALF // 2
    inv = 1.0 / (ROPE_BASE ** (np.arange(0, axis_dim, 2, dtype=np.float64) / axis_dim))
    lane = np.arange(HEAD_DIM)
    within = lane % QK_HALF
    pos = np.where((within < axis_dim)[None, :], row, col)
    ang = pos * inv[within % (axis_dim // 2)][None, :]
    is_a = ((within % axis_dim) < axis_dim // 2)[None, :]
    cos = np.cos(ang)
    sin = np.sin(ang)
    return (cos.astype(np.float32),
            np.where(is_a, -sin, 0.0).astype(np.float32),
            np.where(is_a, 0.0, sin).astype(np.float32))


def _rotate_lanes(x, cos, sin_a, sin_b):
    shift = QK_HALF // 4
    return (x * cos + pltpu.roll(x, HEAD_DIM - shift, 1) * sin_a
            + pltpu.roll(x, shift, 1) * sin_b)


def _diff_lambda(lam_ref, lam_init):
    dl = lam_ref[...]
    return (jnp.exp(jnp.sum(dl[0:1] * dl[1:2], axis=-1, keepdims=True))
            - jnp.exp(jnp.sum(dl[2:3] * dl[3:4], axis=-1, keepdims=True)) + lam_init)


def _split_components(q):
    lane = lax.broadcasted_iota(jnp.int32, (1, HEAD_DIM), 1)
    return (jnp.where(lane < QK_HALF, q, 0.0).astype(BF16),
            jnp.where(lane >= QK_HALF, q, 0.0).astype(BF16))


def _diff_combine(e1, e2, lam, v, sub, lam_init, gate):
    w1 = 1.0 / jnp.sum(e1, axis=-1, keepdims=True)
    w2 = lam / jnp.sum(e2, axis=-1, keepdims=True)
    p = (e1 * w1 - e2 * w2).astype(BF16)
    o = jnp.dot(p, v, preferred_element_type=F32)
    o = o * lax.rsqrt(jnp.mean(o * o, axis=-1, keepdims=True) + LN_EPS)
    o = o * sub * (1.0 - lam_init)
    return o * _silu(gate)


def _diff_latent_kernel(q_ref, k_ref, v_ref, g_ref, ck_ref, cv_ref, cos_ref, sa_ref, sb_ref,
                        lam_ref, sub_ref, o_ref, kt_scr, v_scr, *, seq, tq, lam_init):
    lane = lax.broadcasted_iota(jnp.int32, (1, HEAD_DIM), 1)
    ones_col = jnp.broadcast_to(jnp.where(lane == 0, 1.0, 0.0).astype(BF16), (PREP_ROWS, HEAD_DIM))

    def stage(ci, carry):
        rows = pl.ds(pl.multiple_of(ci * PREP_ROWS, PREP_ROWS), PREP_ROWS)
        k = _rotate_lanes(k_ref[0, rows, :].astype(F32), cos_ref[rows, :], sa_ref[rows, :], sb_ref[rows, :])
        kt_scr[rows, :] = k.astype(BF16)
        v_scr[rows, 0:HEAD_DIM] = v_ref[0, rows, :].astype(BF16)
        v_scr[rows, HEAD_DIM:] = ones_col
        return carry

    lax.fori_loop(0, seq // PREP_ROWS, stage, 0)
    kt_scr[seq:, :] = ck_ref[0, 0].astype(BF16)
    v_scr[seq:, 0:HEAD_DIM] = cv_ref[0, 0].astype(BF16)
    v_scr[seq:, HEAD_DIM:] = ones_col[0:v_scr.shape[0] - seq]
    lam = _diff_lambda(lam_ref, lam_init)

    def tiles(it, carry):
        keys_t = kt_scr[...]
        values = v_scr[...]
        def rows_of(u):
            return pl.ds(pl.multiple_of((it * DIFF_TILES_PER_ITER + u) * tq, tq), tq)

        def scores(u):
            r = rows_of(u)
            pair = _split_components(
                _rotate_lanes(q_ref[0, r, :].astype(F32), cos_ref[r, :], sa_ref[r, :], sb_ref[r, :])
                * (QK_HALF ** -0.5 * LOG2E))
            return [_dot_nt(qc, keys_t) for qc in pair]

        def finish(u, s):
            r = rows_of(u)
            e = [jnp.exp2(sc - jnp.max(sc, axis=-1, keepdims=True)).astype(BF16) for sc in s]
            r1, r2 = [jnp.dot(ec, values, preferred_element_type=F32) for ec in e]
            o = (r1[:, 0:HEAD_DIM] * (1.0 / r1[:, HEAD_DIM:HEAD_DIM + 1])
                 - r2[:, 0:HEAD_DIM] * (lam / r2[:, HEAD_DIM:HEAD_DIM + 1]))
            o = o * lax.rsqrt(jnp.mean(o * o, axis=-1, keepdims=True) + LN_EPS)
            o = o * sub_ref[...] * (1.0 - lam_init)
            o_ref[0, r, :] = (o * _silu(g_ref[0, r, :].astype(F32))).astype(o_ref.dtype)

        pending = scores(0)
        for u in range(DIFF_TILES_PER_ITER):
            upcoming = scores(u + 1) if u + 1 < DIFF_TILES_PER_ITER else None
            finish(u, pending)
            pending = upcoming
        return carry

    lax.fori_loop(0, seq // (tq * DIFF_TILES_PER_ITER), tiles, 0)


def diff_attention_latent(p3, diff_lam, diff_subln, layer_idx, ctx_k, ctx_v):
    b, seq, _ = p3.shape
    ctx_len = ctx_k.shape[2]
    tq = DIFF_TQ
    nk = seq + ctx_len
    lam_init = 0.8 - 0.6 * math.exp(-0.3 * layer_idx)
    q0, k0, v0, g0 = 2 * HEADS, 3 * HEADS, 4 * HEADS, 5 * HEADS
    const = lambda i, h: (0, 0)
    head_blk = lambda off: pl.BlockSpec((1, seq, HEAD_DIM), lambda i, h: (i, 0, off + h))
    ctx_spec = pl.BlockSpec((1, 1, ctx_len, HEAD_DIM), lambda i, h: (i, h, 0, 0))
    tab_spec = pl.BlockSpec((seq, HEAD_DIM), const, pipeline_mode=pl.Buffered(1))
    return pl.pallas_call(
        functools.partial(_diff_latent_kernel, seq=seq, tq=tq, lam_init=lam_init),
        out_shape=jax.ShapeDtypeStruct((b, seq, HEADS * HEAD_DIM), BF16),
        grid=(b, HEADS),
        in_specs=[head_blk(q0), head_blk(k0), head_blk(v0), head_blk(g0),
                  ctx_spec, ctx_spec, tab_spec, tab_spec, tab_spec,
                  pl.BlockSpec(diff_lam.shape, const),
                  pl.BlockSpec((1, HEAD_DIM), const)],
        out_specs=pl.BlockSpec((1, seq, HEAD_DIM), lambda i, h: (i, 0, h)),
        scratch_shapes=[pltpu.VMEM((nk, HEAD_DIM), BF16), pltpu.VMEM((nk, 2 * HEAD_DIM), BF16)],
        compiler_params=_params("parallel", "parallel"),
        name="diff_attention_latent",
    )(p3, p3, p3, p3, ctx_k, ctx_v, *[jnp.asarray(tab) for tab in _rope_tables()],
      diff_lam, diff_subln.reshape(1, HEAD_DIM))


def _diff_context_kernel(q_ref, k_ref, v_ref, g_ref, lam_ref, sub_ref, o_ref, *, lam_init):
    lam = _diff_lambda(lam_ref, lam_init)
    heads = [slice(h * HEAD_DIM, (h + 1) * HEAD_DIM) for h in range(HEADS)]
    seq = q_ref.shape[1]
    lane = lax.broadcasted_iota(jnp.int32, (1, HEAD_DIM), 1)
    ones_col = jnp.broadcast_to(jnp.where(lane == 0, 1.0, 0.0).astype(BF16), (seq, HEAD_DIM))
    qs = [_split_components(q_ref[0, :, c].astype(F32) * (QK_HALF ** -0.5 * LOG2E)) for c in heads]
    kb = [k_ref[0, :, c] for c in heads]
    v_aug = [jnp.concatenate([v_ref[0, :, c], ones_col], axis=-1) for c in heads]
    s = [[_dot_nt(qc, kb[h]) for qc in qs[h]] for h in range(HEADS)]
    e = [[jnp.exp2(sc - jnp.max(sc, axis=-1, keepdims=True)).astype(BF16) for sc in sh] for sh in s]
    r = [[jnp.dot(ec, v_aug[h], preferred_element_type=F32) for ec in e[h]] for h in range(HEADS)]
    for h, c in enumerate(heads):
        r1, r2 = r[h]
        o = (r1[:, 0:HEAD_DIM] * (1.0 / r1[:, HEAD_DIM:HEAD_DIM + 1])
             - r2[:, 0:HEAD_DIM] * (lam / r2[:, HEAD_DIM:HEAD_DIM + 1]))
        o = o * lax.rsqrt(jnp.mean(o * o, axis=-1, keepdims=True) + LN_EPS)
        o = o * sub_ref[...] * (1.0 - lam_init)
        o_ref[0, :, c] = (o * _silu(g_ref[0, :, c].astype(F32))).astype(o_ref.dtype)


def diff_attention_context(p3, diff_lam, diff_subln, layer_idx):
    b, seq, _ = p3.shape
    width = HEADS * HEAD_DIM
    lam_init = 0.8 - 0.6 * math.exp(-0.3 * layer_idx)
    blk = lambda c: pl.BlockSpec((1, seq, width), lambda i: (i, 0, c))
    return pl.pallas_call(
        functools.partial(_diff_context_kernel, lam_init=lam_init),
        out_shape=jax.ShapeDtypeStruct((b, seq, width), BF16),
        grid=(b,),
        in_specs=[blk(2), blk(3), blk(4), blk(5),
                  pl.BlockSpec(diff_lam.shape, lambda i: (0, 0)),
                  pl.BlockSpec((1, HEAD_DIM), lambda i: (0, 0))],
        out_specs=pl.BlockSpec((1, seq, width), lambda i: (i, 0, 0)),
        compiler_params=_params("parallel"),
        name="diff_attention_context",
    )(p3, p3, p3, p3, diff_lam, diff_subln.reshape(1, HEAD_DIM))


def _ctx_attn_kernel(q_ref, k_ref, v_ref, g_ref, o_ref):
    heads = [slice(h * HEAD_DIM, (h + 1) * HEAD_DIM) for h in range(HEADS)]
    seq = q_ref.shape[1]
    lane = lax.broadcasted_iota(jnp.int32, (1, HEAD_DIM), 1)
    ones_col = jnp.broadcast_to(jnp.where(lane == 0, 1.0, 0.0).astype(BF16), (seq, HEAD_DIM))
    q = [(q_ref[0, :, c].astype(F32) * (HEAD_DIM ** -0.5 * LOG2E)).astype(BF16) for c in heads]
    s = [_dot_nt(q[h], k_ref[0, :, c]) for h, c in enumerate(heads)]
    e = [jnp.exp2(sh - jnp.max(sh, axis=-1, keepdims=True)).astype(BF16) for sh in s]
    r = [jnp.dot(e[h], jnp.concatenate([v_ref[0, :, c], ones_col], axis=-1), preferred_element_type=F32)
         for h, c in enumerate(heads)]
    for h, c in enumerate(heads):
        o = r[h][:, 0:HEAD_DIM] / r[h][:, HEAD_DIM:HEAD_DIM + 1]
        o_ref[0, :, c] = (o * _silu(g_ref[0, :, c].astype(F32))).astype(o_ref.dtype)


def context_attention(p3):
    b, seq, _ = p3.shape
    width = HEADS * HEAD_DIM
    blk = lambda c: pl.BlockSpec((1, seq, width), lambda i: (i, 0, c))
    return pl.pallas_call(
        _ctx_attn_kernel,
        out_shape=jax.ShapeDtypeStruct((b, seq, width), BF16),
        grid=(b,),
        in_specs=[blk(0), blk(1), blk(2), blk(3)],
        out_specs=pl.BlockSpec((1, seq, width), lambda i: (i, 0, 0)),
        compiler_params=_params("parallel"),
        name="context_attention",
    )(p3, p3, p3, p3)


NA_TILE_Q = NA_Q_ROWS * GRID_W
NA_TILE_K = NA_K_ROWS * GRID_W
NA_TILES = GRID_ROWS // NA_Q_ROWS
NA_TILES_PER_ITER = 4
NA_CASES = 3
NA_DR = 2 * NA_WIN_R - 1


def _na_key_row_start(tile):
    return int(np.clip(NA_Q_ROWS * tile - NA_WIN_R // 2, 0, GRID_ROWS - NA_K_ROWS))


def _na_bias_layout():
    dr = np.full((NA_CASES, NA_Q_ROWS, NA_K_ROWS), NA_DR, np.int32)
    for case, tile in enumerate((0, 1, NA_TILES - 1)):
        ks = _na_key_row_start(tile)
        for qr_local in range(NA_Q_ROWS):
            qr = NA_Q_ROWS * tile + qr_local
            rs = int(np.clip(qr - NA_WIN_R // 2, 0, GRID_ROWS - NA_WIN_R))
            for kr_local in range(NA_K_ROWS):
                kr = ks + kr_local
                if rs <= kr < rs + NA_WIN_R:
                    dr[case, qr_local, kr_local] = kr - qr + NA_WIN_R - 1
    pairs = sorted({(int(a), int(b)) for a, b in zip(dr[..., 0::2].ravel(), dr[..., 1::2].ravel())})
    index = {p: i for i, p in enumerate(pairs)}
    block = np.array([[[index[(int(dr[c, q, 2 * p]), int(dr[c, q, 2 * p + 1]))]
                        for p in range(NA_K_ROWS // 2)] for q in range(NA_Q_ROWS)] for c in range(NA_CASES)])
    return pairs, block


def _na_bias_blocks(rpb):
    qc = np.arange(GRID_W)[:, None]
    kc = np.arange(GRID_W)[None, :]
    cstart = np.clip(qc - NA_WIN_C // 2, 0, GRID_W - NA_WIN_C)
    col_valid = (kc >= cstart) & (kc < cstart + NA_WIN_C)
    dc = np.clip(kc - qc + NA_WIN_C - 1, 0, 2 * NA_WIN_C - 2)
    onehot = (dc.reshape(-1)[None, :] == np.arange(2 * NA_WIN_C - 1)[:, None]).astype(np.float32)
    toeplitz = jnp.einsum('hrj,jn->hrn', rpb, jnp.asarray(onehot), precision=lax.Precision.HIGHEST)
    per_dr = jnp.where(jnp.asarray(col_valid), toeplitz.reshape(rpb.shape[:2] + dc.shape), NEG_INF)
    per_dr = jnp.concatenate([per_dr, jnp.full_like(per_dr[:, :1], NEG_INF)], axis=1)
    pairs, _ = _na_bias_layout()
    left = np.array([p[0] for p in pairs])
    right = np.array([p[1] for p in pairs])
    return jnp.concatenate([per_dr[:, left], per_dr[:, right]], axis=-1)


def _na_kernel(q_ref, k_ref, v_ref, g_ref, ck_ref, cv_ref, tab_ref, o_ref,
               bias_scr, ck_scr, cv_scr, v_scr):
    b = pl.program_id(1)
    seq = v_ref.shape[1]

    @pl.when(b == 0)
    def _build_bias():
        _, block = _na_bias_layout()
        for case in range(NA_CASES):
            for qr in range(NA_Q_ROWS):
                for p in range(NA_K_ROWS // 2):
                    bias_scr[case, qr * GRID_W:(qr + 1) * GRID_W, p * 128:(p + 1) * 128] = (
                        tab_ref[0, int(block[case, qr, p])] * LOG2E)

    lane = lax.broadcasted_iota(jnp.int32, (1, HEAD_DIM), 1)
    ones_col = jnp.broadcast_to(jnp.where(lane == 0, 1.0, 0.0).astype(BF16), (PREP_ROWS, HEAD_DIM))
    ck_scr[...] = ck_ref[0, 0].astype(BF16)
    cv_scr[:, 0:HEAD_DIM] = cv_ref[0, 0].astype(BF16)
    cv_scr[:, HEAD_DIM:] = ones_col[0:cv_scr.shape[0]]

    def stage(ci, carry):
        rows = pl.ds(pl.multiple_of(ci * PREP_ROWS, PREP_ROWS), PREP_ROWS)
        v_scr[rows, 0:HEAD_DIM] = v_ref[0, rows, :]
        v_scr[rows, HEAD_DIM:] = ones_col
        return carry

    lax.fori_loop(0, seq // PREP_ROWS, stage, 0)
    half_rows = NA_TILE_Q // 2

    def tiles(it, carry):
        k_ctx = ck_scr[...]
        v_ctx = cv_scr[...]
        q, bias, keys, rows = [], [], [], []
        for u in range(NA_TILES_PER_ITER):
            t = it * NA_TILES_PER_ITER + u
            case = jnp.where(t == 0, 0, jnp.where(t == NA_TILES - 1, 2, 1))
            key_row = jnp.clip(NA_Q_ROWS * t - NA_WIN_R // 2, 0, GRID_ROWS - NA_K_ROWS)
            for i in range(2):
                r = pl.ds(pl.multiple_of(t * NA_TILE_Q + i * half_rows, half_rows), half_rows)
                rows.append(r)
                q.append((q_ref[0, r, :].astype(F32) * (HEAD_DIM ** -0.5 * LOG2E)).astype(BF16))
                bias.append((case, slice(i * half_rows, (i + 1) * half_rows)))
                keys.append(pl.ds(pl.multiple_of(key_row * GRID_W, 256), NA_TILE_K))
        s_loc = [_dot_nt(qh, k_ref[0, kk, :]) + bias_scr[c, lr, :] for qh, kk, (c, lr) in zip(q, keys, bias)]
        s_ctx = [_dot_nt(qh, k_ctx) for qh in q]
        m = [jnp.maximum(jnp.max(sl, axis=-1, keepdims=True), jnp.max(sc, axis=-1, keepdims=True))
             for sl, sc in zip(s_loc, s_ctx)]
        e_loc = [jnp.exp2(sl - mh).astype(BF16) for sl, mh in zip(s_loc, m)]
        e_ctx = [jnp.exp2(sc - mh).astype(BF16) for sc, mh in zip(s_ctx, m)]
        res = [jnp.dot(el, v_scr[kk, :], preferred_element_type=F32)
               + jnp.dot(ec, v_ctx, preferred_element_type=F32)
               for el, ec, kk in zip(e_loc, e_ctx, keys)]
        for r, rh in zip(rows, res):
            o = rh[:, 0:HEAD_DIM] / rh[:, HEAD_DIM:HEAD_DIM + 1]
            o_ref[0, r, :] = (o * _silu(g_ref[0, r, :].astype(F32))).astype(o_ref.dtype)
        return carry

    lax.fori_loop(0, NA_TILES // NA_TILES_PER_ITER, tiles, 0)


def neighborhood_attention(p3, ctx_k, ctx_v, rpb):
    b, seq, _ = p3.shape
    ctx_len = ctx_k.shape[2]
    tab = _na_bias_blocks(rpb)
    n_pairs = tab.shape[1]
    ctx_spec = pl.BlockSpec((1, 1, ctx_len, HEAD_DIM), lambda h, i: (i, h, 0, 0))
    head_blk = lambda off: pl.BlockSpec((1, seq, HEAD_DIM), lambda h, i: (i, 0, off + h))
    return pl.pallas_call(
        _na_kernel,
        out_shape=jax.ShapeDtypeStruct((b, seq, HEADS * HEAD_DIM), BF16),
        grid=(HEADS, b),
        in_specs=[head_blk(0), head_blk(HEADS), head_blk(2 * HEADS), head_blk(3 * HEADS),
                  ctx_spec, ctx_spec,
                  pl.BlockSpec((1, n_pairs, GRID_W, 2 * GRID_W), lambda h, i: (h, 0, 0, 0))],
        out_specs=pl.BlockSpec((1, seq, HEAD_DIM), lambda h, i: (i, 0, h)),
        scratch_shapes=[pltpu.VMEM((NA_CASES, NA_TILE_Q, NA_TILE_K), F32),
                        pltpu.VMEM((ctx_len, HEAD_DIM), BF16), pltpu.VMEM((ctx_len, 2 * HEAD_DIM), BF16),
                        pltpu.VMEM((seq, 2 * HEAD_DIM), BF16)],
        compiler_params=_params("arbitrary", "arbitrary"),
        name="neighborhood_attention",
    )(p3, p3, p3, p3, ctx_k, ctx_v, tab)


def _sgu_kernel(u_ref, v_ref, g_ref, ln_ref, w_ref, b_ref, o_ref):
    v = v_ref[...].astype(F32)
    mu = jnp.mean(v, axis=-1, keepdims=True)
    vc = v - mu
    var = jnp.mean(vc * vc, axis=-1, keepdims=True)
    vn = (vc * lax.rsqrt(var + LN_EPS) * ln_ref[...]).astype(BF16)
    groups = w_ref.shape[0]
    for n in range(v.shape[0] // SGU_CHUNK):
        rows = slice(n * SGU_CHUNK, (n + 1) * SGU_CHUNK)
        for g in range(groups):
            cols = slice(g * SGU_GROUP_DIM, (g + 1) * SGU_GROUP_DIM)
            s = jnp.dot(w_ref[g], vn[rows, cols], preferred_element_type=F32) + b_ref[:, g:g + 1]
            y = u_ref[rows, cols].astype(F32) * s * _silu(g_ref[rows, cols].astype(F32))
            o_ref[rows, cols] = y.astype(o_ref.dtype)


def spatial_gating(p2, sgu_ln, w_s_bf16, b_s):
    m = p2.shape[0]
    groups = w_s_bf16.shape[0]
    width = groups * SGU_GROUP_DIM
    blk = lambda c: pl.BlockSpec((SGU_TM, width), lambda i: (i, c))
    return pl.pallas_call(
        _sgu_kernel,
        out_shape=jax.ShapeDtypeStruct((m, width), BF16),
        grid=(m // SGU_TM,),
        in_specs=[blk(4), blk(5), blk(6),
                  pl.BlockSpec((1, width), lambda i: (0, 0)),
                  pl.BlockSpec((groups, SGU_CHUNK, SGU_CHUNK), lambda i: (0, 0, 0)),
                  pl.BlockSpec((SGU_CHUNK, groups), lambda i: (0, 0))],
        out_specs=pl.BlockSpec((SGU_TM, width), lambda i: (i, 0)),
        compiler_params=_params("parallel"),
        name="spatial_gating",
    )(p2, p2, p2, sgu_ln.reshape(1, width), w_s_bf16, jnp.transpose(b_s))


def kernel(x_prompt, x_sample, cache_k_l0, cache_v_l0, cache_k_l1, cache_v_l1, cache_k_l2, cache_v_l2, cache_k_l3, cache_v_l3, c, c_ctx, w_mod_0, b_mod_0, w_in_0, w_out_0, ln_g_0, ln_b_0, pool_w_0, pool_scale_0, diff_lam_0, diff_subln_0, w_mod_1, b_mod_1, w_in_1, w_out_1, ln_g_1, ln_b_1, rpb_1, sgu_ln_1, sgu_w_1, sgu_b_1, w_mod_2, b_mod_2, w_in_2, w_out_2, ln_g_2, ln_b_2, pool_w_2, pool_scale_2, diff_lam_2, diff_subln_2, w_mod_3, b_mod_3, w_in_3, w_out_3, ln_g_3, ln_b_3, rpb_3, sgu_ln_3, sgu_w_3, sgu_b_3):
    cache_k = [cache_k_l0, cache_k_l1, cache_k_l2, cache_k_l3]
    cache_v = [cache_v_l0, cache_v_l1, cache_v_l2, cache_v_l3]
    w_mod = [w_mod_0, w_mod_1, w_mod_2, w_mod_3]
    b_mod = [b_mod_0, b_mod_1, b_mod_2, b_mod_3]
    w_in = [w_in_0, w_in_1, w_in_2, w_in_3]
    w_out = [w_out_0, w_out_1, w_out_2, w_out_3]
    ln_g = [ln_g_0, ln_g_1, ln_g_2, ln_g_3]
    ln_b = [ln_b_0, ln_b_1, ln_b_2, ln_b_3]
    even_p = {0: (pool_w_0, pool_scale_0, diff_lam_0, diff_subln_0),
              2: (pool_w_2, pool_scale_2, diff_lam_2, diff_subln_2)}
    odd_p = {1: (rpb_1, sgu_ln_1, sgu_w_1, sgu_b_1),
             3: (rpb_3, sgu_ln_3, sgu_w_3, sgu_b_3)}

    bp, lp, _ = x_prompt.shape
    bs, ls, _ = x_sample.shape
    xp = x_prompt.reshape(bp * lp, D_MODEL)
    xs = x_sample.reshape(bs * ls, D_MODEL)
    cond = jnp.concatenate([c, c_ctx[None, :], jnp.zeros((COND_ROWS - bs - 1, D_MODEL), F32)], axis=0)

    new_k, new_v = [], []
    for l in range(DEPTH):
        mod3 = ada_params(cond, w_mod[l], b_mod[l]).reshape(COND_ROWS, 1, 3 * D_MODEL)
        w_in_l = w_in[l].astype(BF16)
        n_in = w_in_l.shape[1]
        kv_block = (3 if l % 2 == 0 else 1) * HEADS * HEAD_DIM // PROJ_TN
        pp, nk, nv = in_projection(xp, mod3, w_in_l, None, kv_block=kv_block, cache_seq=lp)
        ps = in_projection(xs, mod3, w_in_l, ls)
        pp3 = pp.reshape(bp, lp, n_in)
        ps3 = ps.reshape(bs, ls, n_in)
        if l % 2 == 0:
            pool_w, pool_scale, diff_lam, diff_subln = even_p[l]
            pool_w = pool_w.astype(BF16)
            ya_p = pool_mixer(pp3, pool_w, pool_scale)
            ya_s = pool_mixer(ps3, pool_w, pool_scale)
            yb_p = diff_attention_context(pp3, diff_lam, diff_subln, l)
            yb_s = diff_attention_latent(ps3, diff_lam, diff_subln, l, cache_k[l], cache_v[l])
        else:
            rpb, sgu_ln, sgu_w, sgu_b = odd_p[l]
            sgu_w = sgu_w.astype(BF16)
            ya_p = context_attention(pp3)
            ya_s = neighborhood_attention(ps3, cache_k[l], cache_v[l], rpb)
            yb_p = spatial_gating(pp, sgu_ln, sgu_w, sgu_b)
            yb_s = spatial_gating(ps, sgu_ln, sgu_w, sgu_b)
        half = ya_p.shape[-1]
        w_out_l = w_out[l].astype(BF16)
        xp = out_projection(ya_p.reshape(bp * lp, half), yb_p.reshape(bp * lp, half), xp, mod3,
                            w_out_l, ln_g[l], ln_b[l], None)
        xs = out_projection(ya_s.reshape(bs * ls, half), yb_s.reshape(bs * ls, half), xs, mod3,
                            w_out_l, ln_g[l], ln_b[l], ls)
        new_k.append(nk)
        new_v.append(nv)

    return (xp.reshape(bp, lp, D_MODEL), xs.reshape(bs, ls, D_MODEL),
            new_k[0], new_v[0], new_k[1], new_v[1], new_k[2], new_v[2], new_k[3], new_v[3])
```

```python
import functools
import math

import numpy as np
import jax
import jax.numpy as jnp
from jax import lax
from jax.experimental import pallas as pl
from jax.experimental.pallas import tpu as pltpu

F32 = jnp.float32
BF16 = jnp.bfloat16

D_MODEL = 2048
DEPTH = 4
GRID_W = 64
GRID_ROWS = 64
HEADS = 8
HEAD_DIM = 128
QK_HALF = 64
POOL_WINDOWS = (2, 4, 8, 16)
POOL_GROUP_DIM = 256
SGU_CHUNK = 128
SGU_GROUP_DIM = 256
NA_WIN_R = 8
NA_WIN_C = 16
NA_Q_ROWS = 8
NA_K_ROWS = 16
ROPE_BASE = 10000.0
LN_EPS = 1e-5
NEG_INF = -1e30
DEEPNORM_ALPHA = (2 * DEPTH) ** 0.25
LOG2E = math.log2(math.e)

VMEM_LIMIT_BYTES = 56 * 1024 * 1024
COND_ROWS = 8
CTX_ROW = 4

PROJ_TM = 1024
PROJ_TN = 1024
PROJ_TN_WIDE = (2048, 1792)
OUT_TM = 1024
OUT_ROW_GROUP = 256
ADA_TN = 1536
POOL_ROWS = 256
POOL_HALO = 16
POOL_CHUNKS_PER_ITER = 8
SGU_TM = 1024
DIFF_TQ = 256
DIFF_TILES_PER_ITER = 4
PREP_ROWS = 512
CTX_BATCH_PER_STEP = 4


def _params(*sem):
    return pltpu.CompilerParams(dimension_semantics=sem, vmem_limit_bytes=VMEM_LIMIT_BYTES)


def _silu(x):
    return x * jax.nn.sigmoid(x)


def _dot_nt(a, b):
    return lax.dot_general(a, b, (((1,), (1,)), ((), ())), preferred_element_type=F32)


def _ada_kernel(c_ref, w_ref, b_ref, o_ref):
    a = _silu(c_ref[...]).astype(BF16)
    o_ref[...] = jnp.dot(a, w_ref[...].astype(BF16), preferred_element_type=F32) + b_ref[...]


def ada_params(cond, w_mod, b_mod):
    n = w_mod.shape[1]
    return pl.pallas_call(
        _ada_kernel,
        out_shape=jax.ShapeDtypeStruct((COND_ROWS, n), F32),
        grid=(n // ADA_TN,),
        in_specs=[pl.BlockSpec((COND_ROWS, D_MODEL), lambda j: (0, 0)),
                  pl.BlockSpec((D_MODEL, ADA_TN), lambda j: (0, j)),
                  pl.BlockSpec((1, ADA_TN), lambda j: (0, j))],
        out_specs=pl.BlockSpec((COND_ROWS, ADA_TN), lambda j: (0, j)),
        compiler_params=_params("parallel"),
        name="ada_params",
    )(cond, w_mod, b_mod.reshape(1, n))


def _inproj_kernel(x_ref, sh_ref, sc_ref, w_ref, o_ref, *rest, kv_block):
    h_scr = rest[-1]
    j = pl.program_id(1)

    @pl.when(j == 0)
    def _():
        h = x_ref[...] * (1.0 + sc_ref[0]) + sh_ref[0]
        h_scr[...] = h.astype(BF16)

    acc = jnp.dot(h_scr[...], w_ref[...], preferred_element_type=F32)
    o_ref[...] = acc.astype(o_ref.dtype)
    if kv_block is not None:
        for which, cache_ref in enumerate(rest[:2]):
            @pl.when(j == kv_block + which)
            def _(cache_ref=cache_ref):
                n_batch, n_heads, seq, head_dim = cache_ref.shape
                for bi in range(n_batch):
                    for hd in range(n_heads):
                        cache_ref[bi, hd] = acc[bi * seq:(bi + 1) * seq, hd * head_dim:(hd + 1) * head_dim]


def in_projection(x2d, mod3, w_bf16, rows_per_cond, kv_block=None, cache_seq=None):
    m = x2d.shape[0]
    n = w_bf16.shape[1]
    if rows_per_cond is None:
        cond_row = lambda i: CTX_ROW
    else:
        blocks_per_cond = rows_per_cond // PROJ_TM
        cond_row = lambda i: i // blocks_per_cond
    tn = PROJ_TN
    if kv_block is None:
        tn = next((c for c in PROJ_TN_WIDE if n % c == 0), PROJ_TN)
    out_shape = [jax.ShapeDtypeStruct((m, n), BF16)]
    out_specs = [pl.BlockSpec((PROJ_TM, tn), lambda i, j: (i, j))]
    if kv_block is not None:
        assert PROJ_TN == HEADS * HEAD_DIM and PROJ_TM % cache_seq == 0
        per_block = PROJ_TM // cache_seq
        cache_spec = pl.BlockSpec((per_block, HEADS, cache_seq, HEAD_DIM), lambda i, j: (i, 0, 0, 0))
        out_shape += [jax.ShapeDtypeStruct((m // cache_seq, HEADS, cache_seq, HEAD_DIM), F32)] * 2
        out_specs += [cache_spec, cache_spec]
    outs = pl.pallas_call(
        functools.partial(_inproj_kernel, kv_block=kv_block),
        out_shape=out_shape,
        grid=(m // PROJ_TM, n // tn),
        in_specs=[pl.BlockSpec((PROJ_TM, D_MODEL), lambda i, j: (i, 0)),
                  pl.BlockSpec((1, 1, D_MODEL), lambda i, j: (cond_row(i), 0, 0)),
                  pl.BlockSpec((1, 1, D_MODEL), lambda i, j: (cond_row(i), 0, 1)),
                  pl.BlockSpec((D_MODEL, tn), lambda i, j: (0, j))],
        out_specs=out_specs,
        scratch_shapes=[pltpu.VMEM((PROJ_TM, D_MODEL), BF16)],
        compiler_params=_params("parallel", "arbitrary"),
        name="in_projection",
    )(x2d, mod3, mod3, w_bf16)
    return outs[0] if kv_block is None else outs


def _outproj_kernel(ya_ref, yb_ref, x_ref, g_ref, w_ref, lng_ref, lnb_ref, o_ref):
    half = ya_ref.shape[1]
    for r in range(OUT_TM // OUT_ROW_GROUP):
        rows = slice(r * OUT_ROW_GROUP, (r + 1) * OUT_ROW_GROUP)
        acc = jnp.dot(ya_ref[rows, :], w_ref[0:half, :], preferred_element_type=F32)
        acc = acc + jnp.dot(yb_ref[rows, :], w_ref[half:, :], preferred_element_type=F32)
        z = DEEPNORM_ALPHA * x_ref[rows, :] + g_ref[0] * acc
        mu = jnp.mean(z, axis=-1, keepdims=True)
        zc = z - mu
        var = jnp.mean(zc * zc, axis=-1, keepdims=True)
        o_ref[rows, :] = zc * lax.rsqrt(var + LN_EPS) * lng_ref[...] + lnb_ref[...]


def out_projection(ya, yb, x2d, mod3, w_bf16, ln_g, ln_b, rows_per_cond):
    m = x2d.shape[0]
    half = ya.shape[1]
    if rows_per_cond is None:
        cond_row = lambda i: CTX_ROW
    else:
        blocks_per_cond = rows_per_cond // OUT_TM
        cond_row = lambda i: i // blocks_per_cond
    return pl.pallas_call(
        _outproj_kernel,
        out_shape=jax.ShapeDtypeStruct((m, D_MODEL), F32),
        grid=(m // OUT_TM,),
        in_specs=[pl.BlockSpec((OUT_TM, half), lambda i: (i, 0)),
                  pl.BlockSpec((OUT_TM, half), lambda i: (i, 0)),
                  pl.BlockSpec((OUT_TM, D_MODEL), lambda i: (i, 0)),
                  pl.BlockSpec((1, 1, D_MODEL), lambda i: (cond_row(i), 0, 2)),
                  pl.BlockSpec((2 * half, D_MODEL), lambda i: (0, 0), pipeline_mode=pl.Buffered(1)),
                  pl.BlockSpec((1, D_MODEL), lambda i: (0, 0)),
                  pl.BlockSpec((1, D_MODEL), lambda i: (0, 0))],
        out_specs=pl.BlockSpec((OUT_TM, D_MODEL), lambda i: (i, 0)),
        compiler_params=_params("parallel"),
        name="out_projection",
    )(ya, yb, x2d, mod3, w_bf16, ln_g.reshape(1, D_MODEL), ln_b.reshape(1, D_MODEL))


def _pool_bands():
    assert max(POOL_WINDOWS) // 2 <= POOL_HALO
    i = np.arange(POOL_ROWS)[:, None]
    own = np.arange(POOL_ROWS)[None, :]
    mains = [(own >= i - w // 2) & (own < i + w // 2) for w in POOL_WINDOWS]
    top_i = np.arange(POOL_HALO)[:, None]
    bot_i = POOL_ROWS - POOL_HALO + top_i
    before = np.arange(POOL_HALO)[None, :] - POOL_HALO
    after = POOL_ROWS + np.arange(POOL_HALO)[None, :]
    halos = [np.stack([before >= top_i - w // 2, after < bot_i + w // 2]) for w in POOL_WINDOWS]
    return np.stack(mains).astype(np.float32), np.stack(halos).astype(np.float32)


def _pool_kernel(a_ref, g_ref, w_ref, s_ref, bm_ref, bh_ref, o_ref, *, seq):
    n_chunks = seq // POOL_ROWS
    group = pl.program_id(1)
    half = jnp.int32(0)
    for gi, window in enumerate(POOL_WINDOWS):
        half = jnp.where(group == gi, window // 2, half)

    def chunk(bi, ci):
        r0 = pl.multiple_of(ci * POOL_ROWS, POOL_ROWS)
        main = a_ref[bi, pl.ds(r0, POOL_ROWS), :]
        win = jnp.dot(bm_ref[0], main.astype(BF16), preferred_element_type=F32)
        if n_chunks > 1:
            p0 = pl.multiple_of(jnp.maximum(r0 - POOL_HALO, 0), POOL_HALO)
            n0 = pl.multiple_of(jnp.minimum(r0 + POOL_ROWS, seq - POOL_HALO), POOL_HALO)
            prev = jnp.where(ci > 0, a_ref[bi, pl.ds(p0, POOL_HALO), :].astype(F32), 0.0).astype(BF16)
            nxt = jnp.where(ci < n_chunks - 1, a_ref[bi, pl.ds(n0, POOL_HALO), :].astype(F32), 0.0).astype(BF16)
            top = win[0:POOL_HALO] + jnp.dot(bh_ref[0, 0], prev, preferred_element_type=F32)
            bot = win[POOL_ROWS - POOL_HALO:] + jnp.dot(bh_ref[0, 1], nxt, preferred_element_type=F32)
            win = jnp.concatenate([top, win[POOL_HALO:POOL_ROWS - POOL_HALO], bot], axis=0)
        t = r0 + lax.broadcasted_iota(jnp.int32, (POOL_ROWS, 1), 0)
        cnt = (jnp.minimum(t + half, seq) - jnp.maximum(t - half, 0)).astype(F32)
        pooled = win / cnt - main.astype(F32)
        mixed = jnp.dot(pooled.astype(BF16), w_ref[0], preferred_element_type=F32)
        gate = g_ref[bi, pl.ds(r0, POOL_ROWS), :].astype(F32)
        y = mixed * s_ref[...] * _silu(gate)
        o_ref[bi, pl.ds(r0, POOL_ROWS), :] = y.astype(o_ref.dtype)

    def group_of_chunks(it, carry):
        for u in range(POOL_CHUNKS_PER_ITER):
            idx = it * POOL_CHUNKS_PER_ITER + u
            chunk(idx // n_chunks, idx % n_chunks)
        return carry

    lax.fori_loop(0, a_ref.shape[0] * n_chunks // POOL_CHUNKS_PER_ITER, group_of_chunks, 0)


def pool_mixer(p3, w_pool_bf16, pool_scale):
    b, seq, _ = p3.shape
    groups = len(POOL_WINDOWS)
    band_main, band_halo = (jnp.asarray(m, BF16) for m in _pool_bands())
    bb = max(1, POOL_CHUNKS_PER_ITER * POOL_ROWS // seq)
    return pl.pallas_call(
        functools.partial(_pool_kernel, seq=seq),
        out_shape=jax.ShapeDtypeStruct((b, seq, groups * POOL_GROUP_DIM), BF16),
        grid=(b // bb, groups),
        in_specs=[pl.BlockSpec((bb, seq, POOL_GROUP_DIM), lambda i, g: (i, 0, g)),
                  pl.BlockSpec((bb, seq, POOL_GROUP_DIM), lambda i, g: (i, 0, groups + g)),
                  pl.BlockSpec((1, POOL_GROUP_DIM, POOL_GROUP_DIM), lambda i, g: (g, 0, 0)),
                  pl.BlockSpec((1, POOL_GROUP_DIM), lambda i, g: (0, g)),
                  pl.BlockSpec((1, POOL_ROWS, POOL_ROWS), lambda i, g: (g, 0, 0)),
                  pl.BlockSpec((1, 2, POOL_HALO, POOL_HALO), lambda i, g: (g, 0, 0, 0))],
        out_specs=pl.BlockSpec((bb, seq, POOL_GROUP_DIM), lambda i, g: (i, 0, g)),
        compiler_params=_params("parallel", "parallel"),
        name="pool_mixer",
    )(p3, p3, w_pool_bf16, pool_scale.reshape(1, groups * POOL_GROUP_DIM), band_main, band_halo)


def _rope_tables():
    t = np.arange(GRID_ROWS * GRID_W)
    row = (t // GRID_W).astype(np.float64)[:, None]
    col = (t % GRID_W).astype(np.float64)[:, None]
    axis_dim = QK_HALF // 2
    inv = 1.0 / (ROPE_BASE ** (np.arange(0, axis_dim, 2, dtype=np.float64) / axis_dim))
    lane = np.arange(HEAD_DIM)
    within = lane % QK_HALF
    pos = np.where((within < axis_dim)[None, :], row, col)
    ang = pos * inv[within % (axis_dim // 2)][None, :]
    is_a = ((within % axis_dim) < axis_dim // 2)[None, :]
    cos = np.cos(ang)
    sin = np.sin(ang)
    return (cos.astype(np.float32),
            np.where(is_a, -sin, 0.0).astype(np.float32),
            np.where(is_a, 0.0, sin).astype(np.float32))


def _rotate_lanes(x, cos, sin_a, sin_b):
    shift = QK_HALF // 4
    return (x * cos + pltpu.roll(x, HEAD_DIM - shift, 1) * sin_a
            + pltpu.roll(x, shift, 1) * sin_b)


def _diff_lambda(lam_ref, lam_init):
    dl = lam_ref[...]
    return (jnp.exp(jnp.sum(dl[0:1] * dl[1:2], axis=-1, keepdims=True))
            - jnp.exp(jnp.sum(dl[2:3] * dl[3:4], axis=-1, keepdims=True)) + lam_init)


def _split_components(q):
    lane = lax.broadcasted_iota(jnp.int32, (1, HEAD_DIM), 1)
    return (jnp.where(lane < QK_HALF, q, 0.0).astype(BF16),
            jnp.where(lane >= QK_HALF, q, 0.0).astype(BF16))


def _diff_combine(e1, e2, lam, v, sub, lam_init, gate):
    w1 = 1.0 / jnp.sum(e1, axis=-1, keepdims=True)
    w2 = lam / jnp.sum(e2, axis=-1, keepdims=True)
    p = (e1 * w1 - e2 * w2).astype(BF16)
    o = jnp.dot(p, v, preferred_element_type=F32)
    o = o * lax.rsqrt(jnp.mean(o * o, axis=-1, keepdims=True) + LN_EPS)
    o = o * sub * (1.0 - lam_init)
    return o * _silu(gate)


def _diff_latent_kernel(q_ref, k_ref, v_ref, g_ref, ck_ref, cv_ref, cos_ref, sa_ref, sb_ref,
                        lam_ref, sub_ref, o_ref, kt_scr, v_scr, *, seq, tq, lam_init):
    lane = lax.broadcasted_iota(jnp.int32, (1, HEAD_DIM), 1)
    ones_col = jnp.broadcast_to(jnp.where(lane == 0, 1.0, 0.0).astype(BF16), (PREP_ROWS, HEAD_DIM))

    def stage(ci, carry):
        rows = pl.ds(pl.multiple_of(ci * PREP_ROWS, PREP_ROWS), PREP_ROWS)
        k = _rotate_lanes(k_ref[0, rows, :].astype(F32), cos_ref[rows, :], sa_ref[rows, :], sb_ref[rows, :])
        kt_scr[rows, :] = k.astype(BF16)
        v_scr[rows, 0:HEAD_DIM] = v_ref[0, rows, :].astype(BF16)
        v_scr[rows, HEAD_DIM:] = ones_col
        return carry

    lax.fori_loop(0, seq // PREP_ROWS, stage, 0)
    kt_scr[seq:, :] = ck_ref[0, 0].astype(BF16)
    v_scr[seq:, 0:HEAD_DIM] = cv_ref[0, 0].astype(BF16)
    v_scr[seq:, HEAD_DIM:] = ones_col[0:v_scr.shape[0] - seq]
    lam = _diff_lambda(lam_ref, lam_init)

    def tiles(it, carry):
        keys_t = kt_scr[...]
        values = v_scr[...]
        def rows_of(u):
            return pl.ds(pl.multiple_of((it * DIFF_TILES_PER_ITER + u) * tq, tq), tq)

        def scores(u):
            r = rows_of(u)
            pair = _split_components(
                _rotate_lanes(q_ref[0, r, :].astype(F32), cos_ref[r, :], sa_ref[r, :], sb_ref[r, :])
                * (QK_HALF ** -0.5 * LOG2E))
            return [_dot_nt(qc, keys_t) for qc in pair]

        def finish(u, s):
            r = rows_of(u)
            e = [jnp.exp2(sc - jnp.max(sc, axis=-1, keepdims=True)).astype(BF16) for sc in s]
            r1, r2 = [jnp.dot(ec, values, preferred_element_type=F32) for ec in e]
            o = (r1[:, 0:HEAD_DIM] * (1.0 / r1[:, HEAD_DIM:HEAD_DIM + 1])
                 - r2[:, 0:HEAD_DIM] * (lam / r2[:, HEAD_DIM:HEAD_DIM + 1]))
            o = o * lax.rsqrt(jnp.mean(o * o, axis=-1, keepdims=True) + LN_EPS)
            o = o * sub_ref[...] * (1.0 - lam_init)
            o_ref[0, r, :] = (o * _silu(g_ref[0, r, :].astype(F32))).astype(o_ref.dtype)

        pending = scores(0)
        for u in range(DIFF_TILES_PER_ITER):
            upcoming = scores(u + 1) if u + 1 < DIFF_TILES_PER_ITER else None
            finish(u, pending)
            pending = upcoming
        return carry

    lax.fori_loop(0, seq // (tq * DIFF_TILES_PER_ITER), tiles, 0)


def diff_attention_latent(p3, diff_lam, diff_subln, layer_idx, ctx_k, ctx_v):
    b, seq, _ = p3.shape
    ctx_len = ctx_k.shape[2]
    tq = DIFF_TQ
    nk = seq + ctx_len
    lam_init = 0.8 - 0.6 * math.exp(-0.3 * layer_idx)
    q0, k0, v0, g0 = 2 * HEADS, 3 * HEADS, 4 * HEADS, 5 * HEADS
    const = lambda i, h: (0, 0)
    head_blk = lambda off: pl.BlockSpec((1, seq, HEAD_DIM), lambda i, h: (i, 0, off + h))
    ctx_spec = pl.BlockSpec((1, 1, ctx_len, HEAD_DIM), lambda i, h: (i, h, 0, 0))
    tab_spec = pl.BlockSpec((seq, HEAD_DIM), const, pipeline_mode=pl.Buffered(1))
    return pl.pallas_call(
        functools.partial(_diff_latent_kernel, seq=seq, tq=tq, lam_init=lam_init),
        out_shape=jax.ShapeDtypeStruct((b, seq, HEADS * HEAD_DIM), BF16),
        grid=(b, HEADS),
        in_specs=[head_blk(q0), head_blk(k0), head_blk(v0), head_blk(g0),
                  ctx_spec, ctx_spec, tab_spec, tab_spec, tab_spec,
                  pl.BlockSpec(diff_lam.shape, const),
                  pl.BlockSpec((1, HEAD_DIM), const)],
        out_specs=pl.BlockSpec((1, seq, HEAD_DIM), lambda i, h: (i, 0, h)),
        scratch_shapes=[pltpu.VMEM((nk, HEAD_DIM), BF16), pltpu.VMEM((nk, 2 * HEAD_DIM), BF16)],
        compiler_params=_params("parallel", "parallel"),
        name="diff_attention_latent",
    )(p3, p3, p3, p3, ctx_k, ctx_v, *[jnp.asarray(tab) for tab in _rope_tables()],
      diff_lam, diff_subln.reshape(1, HEAD_DIM))


def _diff_context_kernel(q_ref, k_ref, v_ref, g_ref, lam_ref, sub_ref, o_ref, *, lam_init):
    lam = _diff_lambda(lam_ref, lam_init)
    heads = [slice(h * HEAD_DIM, (h + 1) * HEAD_DIM) for h in range(HEADS)]
    seq = q_ref.shape[1]
    lane = lax.broadcasted_iota(jnp.int32, (1, HEAD_DIM), 1)
    ones_col = jnp.broadcast_to(jnp.where(lane == 0, 1.0, 0.0).astype(BF16), (seq, HEAD_DIM))

    def one(bi, carry):
        qs = [_split_components(q_ref[bi, :, c].astype(F32) * (QK_HALF ** -0.5 * LOG2E)) for c in heads]
        kb = [k_ref[bi, :, c] for c in heads]
        v_aug = [jnp.concatenate([v_ref[bi, :, c], ones_col], axis=-1) for c in heads]
        s = [[_dot_nt(qc, kb[h]) for qc in qs[h]] for h in range(HEADS)]
        e = [[jnp.exp2(sc - jnp.max(sc, axis=-1, keepdims=True)).astype(BF16) for sc in sh] for sh in s]
        r = [[jnp.dot(ec, v_aug[h], preferred_element_type=F32) for ec in e[h]] for h in range(HEADS)]
        for h, c in enumerate(heads):
            r1, r2 = r[h]
            o = (r1[:, 0:HEAD_DIM] * (1.0 / r1[:, HEAD_DIM:HEAD_DIM + 1])
                 - r2[:, 0:HEAD_DIM] * (lam / r2[:, HEAD_DIM:HEAD_DIM + 1]))
            o = o * lax.rsqrt(jnp.mean(o * o, axis=-1, keepdims=True) + LN_EPS)
            o = o * sub_ref[...] * (1.0 - lam_init)
            o_ref[bi, :, c] = (o * _silu(g_ref[bi, :, c].astype(F32))).astype(o_ref.dtype)
        return carry

    lax.fori_loop(0, q_ref.shape[0], one, 0)


def diff_attention_context(p3, diff_lam, diff_subln, layer_idx):
    b, seq, _ = p3.shape
    width = HEADS * HEAD_DIM
    lam_init = 0.8 - 0.6 * math.exp(-0.3 * layer_idx)
    bb = CTX_BATCH_PER_STEP
    blk = lambda c: pl.BlockSpec((bb, seq, width), lambda i: (i, 0, c))
    return pl.pallas_call(
        functools.partial(_diff_context_kernel, lam_init=lam_init),
        out_shape=jax.ShapeDtypeStruct((b, seq, width), BF16),
        grid=(b // bb,),
        in_specs=[blk(2), blk(3), blk(4), blk(5),
                  pl.BlockSpec(diff_lam.shape, lambda i: (0, 0)),
                  pl.BlockSpec((1, HEAD_DIM), lambda i: (0, 0))],
        out_specs=pl.BlockSpec((bb, seq, width), lambda i: (i, 0, 0)),
        compiler_params=_params("parallel"),
        name="diff_attention_context",
    )(p3, p3, p3, p3, diff_lam, diff_subln.reshape(1, HEAD_DIM))


def _ctx_attn_kernel(q_ref, k_ref, v_ref, g_ref, o_ref):
    heads = [slice(h * HEAD_DIM, (h + 1) * HEAD_DIM) for h in range(HEADS)]
    seq = q_ref.shape[1]
    lane = lax.broadcasted_iota(jnp.int32, (1, HEAD_DIM), 1)
    ones_col = jnp.broadcast_to(jnp.where(lane == 0, 1.0, 0.0).astype(BF16), (seq, HEAD_DIM))

    def one(bi, carry):
        q = [(q_ref[bi, :, c].astype(F32) * (HEAD_DIM ** -0.5 * LOG2E)).astype(BF16) for c in heads]
        s = [_dot_nt(q[h], k_ref[bi, :, c]) for h, c in enumerate(heads)]
        e = [jnp.exp2(sh - jnp.max(sh, axis=-1, keepdims=True)).astype(BF16) for sh in s]
        r = [jnp.dot(e[h], jnp.concatenate([v_ref[bi, :, c], ones_col], axis=-1), preferred_element_type=F32)
             for h, c in enumerate(heads)]
        for h, c in enumerate(heads):
            o = r[h][:, 0:HEAD_DIM] / r[h][:, HEAD_DIM:HEAD_DIM + 1]
            o_ref[bi, :, c] = (o * _silu(g_ref[bi, :, c].astype(F32))).astype(o_ref.dtype)
        return carry

    lax.fori_loop(0, q_ref.shape[0], one, 0)


def context_attention(p3):
    b, seq, _ = p3.shape
    width = HEADS * HEAD_DIM
    bb = CTX_BATCH_PER_STEP
    blk = lambda c: pl.BlockSpec((bb, seq, width), lambda i: (i, 0, c))
    return pl.pallas_call(
        _ctx_attn_kernel,
        out_shape=jax.ShapeDtypeStruct((b, seq, width), BF16),
        grid=(b // bb,),
        in_specs=[blk(0), blk(1), blk(2), blk(3)],
        out_specs=pl.BlockSpec((bb, seq, width), lambda i: (i, 0, 0)),
        compiler_params=_params("parallel"),
        name="context_attention",
    )(p3, p3, p3, p3)


NA_TILE_Q = NA_Q_ROWS * GRID_W
NA_TILE_K = NA_K_ROWS * GRID_W
NA_TILES = GRID_ROWS // NA_Q_ROWS
NA_TILES_PER_ITER = 4
NA_CASES = 3
NA_DR = 2 * NA_WIN_R - 1


def _na_key_row_start(tile):
    return int(np.clip(NA_Q_ROWS * tile - NA_WIN_R // 2, 0, GRID_ROWS - NA_K_ROWS))


def _na_bias_layout():
    dr = np.full((NA_CASES, NA_Q_ROWS, NA_K_ROWS), NA_DR, np.int32)
    for case, tile in enumerate((0, 1, NA_TILES - 1)):
        ks = _na_key_row_start(tile)
        for qr_local in range(NA_Q_ROWS):
            qr = NA_Q_ROWS * tile + qr_local
            rs = int(np.clip(qr - NA_WIN_R // 2, 0, GRID_ROWS - NA_WIN_R))
            for kr_local in range(NA_K_ROWS):
                kr = ks + kr_local
                if rs <= kr < rs + NA_WIN_R:
                    dr[case, qr_local, kr_local] = kr - qr + NA_WIN_R - 1
    pairs = sorted({(int(a), int(b)) for a, b in zip(dr[..., 0::2].ravel(), dr[..., 1::2].ravel())})
    index = {p: i for i, p in enumerate(pairs)}
    block = np.array([[[index[(int(dr[c, q, 2 * p]), int(dr[c, q, 2 * p + 1]))]
                        for p in range(NA_K_ROWS // 2)] for q in range(NA_Q_ROWS)] for c in range(NA_CASES)])
    return pairs, block


def _na_bias_blocks(rpb):
    qc = np.arange(GRID_W)[:, None]
    kc = np.arange(GRID_W)[None, :]
    cstart = np.clip(qc - NA_WIN_C // 2, 0, GRID_W - NA_WIN_C)
    col_valid = (kc >= cstart) & (kc < cstart + NA_WIN_C)
    dc = np.clip(kc - qc + NA_WIN_C - 1, 0, 2 * NA_WIN_C - 2)
    onehot = (dc.reshape(-1)[None, :] == np.arange(2 * NA_WIN_C - 1)[:, None]).astype(np.float32)
    toeplitz = jnp.einsum('hrj,jn->hrn', rpb, jnp.asarray(onehot), precision=lax.Precision.HIGHEST)
    per_dr = jnp.where(jnp.asarray(col_valid), toeplitz.reshape(rpb.shape[:2] + dc.shape), NEG_INF)
    per_dr = jnp.concatenate([per_dr, jnp.full_like(per_dr[:, :1], NEG_INF)], axis=1)
    pairs, _ = _na_bias_layout()
    left = np.array([p[0] for p in pairs])
    right = np.array([p[1] for p in pairs])
    return jnp.concatenate([per_dr[:, left], per_dr[:, right]], axis=-1)


def _na_kernel(q_ref, k_ref, v_ref, g_ref, ck_ref, cv_ref, tab_ref, o_ref,
               bias_scr, ck_scr, cv_scr, v_scr):
    b = pl.program_id(1)
    seq = v_ref.shape[1]

    @pl.when(b == 0)
    def _build_bias():
        _, block = _na_bias_layout()
        for case in range(NA_CASES):
            for qr in range(NA_Q_ROWS):
                for p in range(NA_K_ROWS // 2):
                    bias_scr[case, qr * GRID_W:(qr + 1) * GRID_W, p * 128:(p + 1) * 128] = (
                        tab_ref[0, int(block[case, qr, p])] * LOG2E)

    lane = lax.broadcasted_iota(jnp.int32, (1, HEAD_DIM), 1)
    ones_col = jnp.broadcast_to(jnp.where(lane == 0, 1.0, 0.0).astype(BF16), (PREP_ROWS, HEAD_DIM))
    ck_scr[...] = ck_ref[0, 0].astype(BF16)
    cv_scr[:, 0:HEAD_DIM] = cv_ref[0, 0].astype(BF16)
    cv_scr[:, HEAD_DIM:] = ones_col[0:cv_scr.shape[0]]

    def stage(ci, carry):
        rows = pl.ds(pl.multiple_of(ci * PREP_ROWS, PREP_ROWS), PREP_ROWS)
        v_scr[rows, 0:HEAD_DIM] = v_ref[0, rows, :]
        v_scr[rows, HEAD_DIM:] = ones_col
        return carry

    lax.fori_loop(0, seq // PREP_ROWS, stage, 0)
    half_rows = NA_TILE_Q // 2

    def tiles(it, carry):
        k_ctx = ck_scr[...]
        v_ctx = cv_scr[...]
        q, bias, keys, rows = [], [], [], []
        for u in range(NA_TILES_PER_ITER):
            t = it * NA_TILES_PER_ITER + u
            case = jnp.where(t == 0, 0, jnp.where(t == NA_TILES - 1, 2, 1))
            key_row = jnp.clip(NA_Q_ROWS * t - NA_WIN_R // 2, 0, GRID_ROWS - NA_K_ROWS)
            for i in range(2):
                r = pl.ds(pl.multiple_of(t * NA_TILE_Q + i * half_rows, half_rows), half_rows)
                rows.append(r)
                q.append((q_ref[0, r, :].astype(F32) * (HEAD_DIM ** -0.5 * LOG2E)).astype(BF16))
                bias.append((case, slice(i * half_rows, (i + 1) * half_rows)))
                keys.append(pl.ds(pl.multiple_of(key_row * GRID_W, 256), NA_TILE_K))
        s_loc = [_dot_nt(qh, k_ref[0, kk, :]) + bias_scr[c, lr, :] for qh, kk, (c, lr) in zip(q, keys, bias)]
        s_ctx = [_dot_nt(qh, k_ctx) for qh in q]
        m = [jnp.maximum(jnp.max(sl, axis=-1, keepdims=True), jnp.max(sc, axis=-1, keepdims=True))
             for sl, sc in zip(s_loc, s_ctx)]
        e_loc = [jnp.exp2(sl - mh).astype(BF16) for sl, mh in zip(s_loc, m)]
        e_ctx = [jnp.exp2(sc - mh).astype(BF16) for sc, mh in zip(s_ctx, m)]
        res = [jnp.dot(el, v_scr[kk, :], preferred_element_type=F32)
               + jnp.dot(ec, v_ctx, preferred_element_type=F32)
               for el, ec, kk in zip(e_loc, e_ctx, keys)]
        for r, rh in zip(rows, res):
            o = rh[:, 0:HEAD_DIM] / rh[:, HEAD_DIM:HEAD_DIM + 1]
            o_ref[0, r, :] = (o * _silu(g_ref[0, r, :].astype(F32))).astype(o_ref.dtype)
        return carry

    lax.fori_loop(0, NA_TILES // NA_TILES_PER_ITER, tiles, 0)


def neighborhood_attention(p3, ctx_k, ctx_v, rpb):
    b, seq, _ = p3.shape
    ctx_len = ctx_k.shape[2]
    tab = _na_bias_blocks(rpb)
    n_pairs = tab.shape[1]
    ctx_spec = pl.BlockSpec((1, 1, ctx_len, HEAD_DIM), lambda h, i: (i, h, 0, 0))
    head_blk = lambda off: pl.BlockSpec((1, seq, HEAD_DIM), lambda h, i: (i, 0, off + h))
    return pl.pallas_call(
        _na_kernel,
        out_shape=jax.ShapeDtypeStruct((b, seq, HEADS * HEAD_DIM), BF16),
        grid=(HEADS, b),
        in_specs=[head_blk(0), head_blk(HEADS), head_blk(2 * HEADS), head_blk(3 * HEADS),
                  ctx_spec, ctx_spec,
                  pl.BlockSpec((1, n_pairs, GRID_W, 2 * GRID_W), lambda h, i: (h, 0, 0, 0))],
        out_specs=pl.BlockSpec((1, seq, HEAD_DIM), lambda h, i: (i, 0, h)),
        scratch_shapes=[pltpu.VMEM((NA_CASES, NA_TILE_Q, NA_TILE_K), F32),
                        pltpu.VMEM((ctx_len, HEAD_DIM), BF16), pltpu.VMEM((ctx_len, 2 * HEAD_DIM), BF16),
                        pltpu.VMEM((seq, 2 * HEAD_DIM), BF16)],
        compiler_params=_params("arbitrary", "arbitrary"),
        name="neighborhood_attention",
    )(p3, p3, p3, p3, ctx_k, ctx_v, tab)


def _sgu_kernel(u_ref, v_ref, g_ref, ln_ref, w_ref, b_ref, o_ref):
    v = v_ref[...].astype(F32)
    mu = jnp.mean(v, axis=-1, keepdims=True)
    vc = v - mu
    var = jnp.mean(vc * vc, axis=-1, keepdims=True)
    vn = (vc * lax.rsqrt(var + LN_EPS) * ln_ref[...]).astype(BF16)
    groups = w_ref.shape[0]
    for n in range(v.shape[0] // SGU_CHUNK):
        rows = slice(n * SGU_CHUNK, (n + 1) * SGU_CHUNK)
        for g in range(groups):
            cols = slice(g * SGU_GROUP_DIM, (g + 1) * SGU_GROUP_DIM)
            s = jnp.dot(w_ref[g], vn[rows, cols], preferred_element_type=F32) + b_ref[:, g:g + 1]
            y = u_ref[rows, cols].astype(F32) * s * _silu(g_ref[rows, cols].astype(F32))
            o_ref[rows, cols] = y.astype(o_ref.dtype)


def spatial_gating(p2, sgu_ln, w_s_bf16, b_s):
    m = p2.shape[0]
    groups = w_s_bf16.shape[0]
    width = groups * SGU_GROUP_DIM
    blk = lambda c: pl.BlockSpec((SGU_TM, width), lambda i: (i, c))
    return pl.pallas_call(
        _sgu_kernel,
        out_shape=jax.ShapeDtypeStruct((m, width), BF16),
        grid=(m // SGU_TM,),
        in_specs=[blk(4), blk(5), blk(6),
                  pl.BlockSpec((1, width), lambda i: (0, 0)),
                  pl.BlockSpec((groups, SGU_CHUNK, SGU_CHUNK), lambda i: (0, 0, 0)),
                  pl.BlockSpec((SGU_CHUNK, groups), lambda i: (0, 0))],
        out_specs=pl.BlockSpec((SGU_TM, width), lambda i: (i, 0)),
        compiler_params=_params("parallel"),
        name="spatial_gating",
    )(p2, p2, p2, sgu_ln.reshape(1, width), w_s_bf16, jnp.transpose(b_s))


def kernel(x_prompt, x_sample, cache_k_l0, cache_v_l0, cache_k_l1, cache_v_l1, cache_k_l2, cache_v_l2, cache_k_l3, cache_v_l3, c, c_ctx, w_mod_0, b_mod_0, w_in_0, w_out_0, ln_g_0, ln_b_0, pool_w_0, pool_scale_0, diff_lam_0, diff_subln_0, w_mod_1, b_mod_1, w_in_1, w_out_1, ln_g_1, ln_b_1, rpb_1, sgu_ln_1, sgu_w_1, sgu_b_1, w_mod_2, b_mod_2, w_in_2, w_out_2, ln_g_2, ln_b_2, pool_w_2, pool_scale_2, diff_lam_2, diff_subln_2, w_mod_3, b_mod_3, w_in_3, w_out_3, ln_g_3, ln_b_3, rpb_3, sgu_ln_3, sgu_w_3, sgu_b_3):
    cache_k = [cache_k_l0, cache_k_l1, cache_k_l2, cache_k_l3]
    cache_v = [cache_v_l0, cache_v_l1, cache_v_l2, cache_v_l3]
    w_mod = [w_mod_0, w_mod_1, w_mod_2, w_mod_3]
    b_mod = [b_mod_0, b_mod_1, b_mod_2, b_mod_3]
    w_in = [w_in_0, w_in_1, w_in_2, w_in_3]
    w_out = [w_out_0, w_out_1, w_out_2, w_out_3]
    ln_g = [ln_g_0, ln_g_1, ln_g_2, ln_g_3]
    ln_b = [ln_b_0, ln_b_1, ln_b_2, ln_b_3]
    even_p = {0: (pool_w_0, pool_scale_0, diff_lam_0, diff_subln_0),
              2: (pool_w_2, pool_scale_2, diff_lam_2, diff_subln_2)}
    odd_p = {1: (rpb_1, sgu_ln_1, sgu_w_1, sgu_b_1),
             3: (rpb_3, sgu_ln_3, sgu_w_3, sgu_b_3)}

    bp, lp, _ = x_prompt.shape
    bs, ls, _ = x_sample.shape
    xp = x_prompt.reshape(bp * lp, D_MODEL)
    xs = x_sample.reshape(bs * ls, D_MODEL)
    cond = jnp.concatenate([c, c_ctx[None, :], jnp.zeros((COND_ROWS - bs - 1, D_MODEL), F32)], axis=0)

    new_k, new_v = [], []
    for l in range(DEPTH):
        mod3 = ada_params(cond, w_mod[l], b_mod[l]).reshape(COND_ROWS, 1, 3 * D_MODEL)
        w_in_l = w_in[l].astype(BF16)
        n_in = w_in_l.shape[1]
        kv_block = (3 if l % 2 == 0 else 1) * HEADS * HEAD_DIM // PROJ_TN
        pp, nk, nv = in_projection(xp, mod3, w_in_l, None, kv_block=kv_block, cache_seq=lp)
        ps = in_projection(xs, mod3, w_in_l, ls)
        pp3 = pp.reshape(bp, lp, n_in)
        ps3 = ps.reshape(bs, ls, n_in)
        if l % 2 == 0:
            pool_w, pool_scale, diff_lam, diff_subln = even_p[l]
            pool_w = pool_w.astype(BF16)
            ya_p = pool_mixer(pp3, pool_w, pool_scale)
            ya_s = pool_mixer(ps3, pool_w, pool_scale)
            yb_p = diff_attention_context(pp3, diff_lam, diff_subln, l)
            yb_s = diff_attention_latent(ps3, diff_lam, diff_subln, l, cache_k[l], cache_v[l])
        else:
            rpb, sgu_ln, sgu_w, sgu_b = odd_p[l]
            sgu_w = sgu_w.astype(BF16)
            ya_p = context_attention(pp3)
            ya_s = neighborhood_attention(ps3, cache_k[l], cache_v[l], rpb)
            yb_p = spatial_gating(pp, sgu_ln, sgu_w, sgu_b)
            yb_s = spatial_gating(ps, sgu_ln, sgu_w, sgu_b)
        half = ya_p.shape[-1]
        w_out_l = w_out[l].astype(BF16)
        xp = out_projection(ya_p.reshape(bp * lp, half), yb_p.reshape(bp * lp, half), xp, mod3,
                            w_out_l, ln_g[l], ln_b[l], None)
        xs = out_projection(ya_s.reshape(bs * ls, half), yb_s.reshape(bs * ls, half), xs, mod3,
                            w_out_l, ln_g[l], ln_b[l], ls)
        new_k.append(nk)
        new_v.append(nv)

    return (xp.reshape(bp, lp, D_MODEL), xs.reshape(bs, ls, D_MODEL),
            new_k[0], new_v[0], new_k[1], new_v[1], new_k[2], new_v[2], new_k[3], new_v[3])
```

```python
import functools
import math

import numpy as np
import jax
import jax.numpy as jnp
from jax import lax
from jax.experimental import pallas as pl
from jax.experimental.pallas import tpu as pltpu

F32 = jnp.float32
BF16 = jnp.bfloat16

D_MODEL = 2048
DEPTH = 4
GRID_W = 64
GRID_ROWS = 64
HEADS = 8
HEAD_DIM = 128
QK_HALF = 64
POOL_WINDOWS = (2, 4, 8, 16)
POOL_GROUP_DIM = 256
SGU_CHUNK = 128
SGU_GROUP_DIM = 256
NA_WIN_R = 8
NA_WIN_C = 16
NA_Q_ROWS = 8
NA_K_ROWS = 16
ROPE_BASE = 10000.0
LN_EPS = 1e-5
NEG_INF = -1e30
DEEPNORM_ALPHA = (2 * DEPTH) ** 0.25
LOG2E = math.log2(math.e)

VMEM_LIMIT_BYTES = 56 * 1024 * 1024
COND_ROWS = 8
CTX_ROW = 4

PROJ_TM = 1024
PROJ_TN = 1024
PROJ_TN_WIDE = (2048, 1792)
OUT_TM = 1024
OUT_ROW_GROUP = 256
ADA_TN = 1536
POOL_ROWS = 256
POOL_HALO = 16
POOL_CHUNKS_PER_ITER = 8
SGU_TM = 1024
DIFF_TQ = 256
DIFF_TILES_PER_ITER = 4
PREP_ROWS = 512
CTX_BATCH_PER_STEP = 4
CTX_HEAD_GROUP = 2


def _params(*sem):
    return pltpu.CompilerParams(dimension_semantics=sem, vmem_limit_bytes=VMEM_LIMIT_BYTES)


def _silu(x):
    return x * jax.nn.sigmoid(x)


def _dot_nt(a, b):
    return lax.dot_general(a, b, (((1,), (1,)), ((), ())), preferred_element_type=F32)


def _ada_kernel(c_ref, w_ref, b_ref, o_ref):
    a = _silu(c_ref[...]).astype(BF16)
    o_ref[...] = jnp.dot(a, w_ref[...].astype(BF16), preferred_element_type=F32) + b_ref[...]


def ada_params(cond, w_mod, b_mod):
    n = w_mod.shape[1]
    return pl.pallas_call(
        _ada_kernel,
        out_shape=jax.ShapeDtypeStruct((COND_ROWS, n), F32),
        grid=(n // ADA_TN,),
        in_specs=[pl.BlockSpec((COND_ROWS, D_MODEL), lambda j: (0, 0)),
                  pl.BlockSpec((D_MODEL, ADA_TN), lambda j: (0, j)),
                  pl.BlockSpec((1, ADA_TN), lambda j: (0, j))],
        out_specs=pl.BlockSpec((COND_ROWS, ADA_TN), lambda j: (0, j)),
        compiler_params=_params("parallel"),
        name="ada_params",
    )(cond, w_mod, b_mod.reshape(1, n))


def _inproj_kernel(x_ref, sh_ref, sc_ref, w_ref, o_ref, *rest, kv_block):
    h_scr = rest[-1]
    j = pl.program_id(1)

    @pl.when(j == 0)
    def _():
        h = x_ref[...] * (1.0 + sc_ref[0]) + sh_ref[0]
        h_scr[...] = h.astype(BF16)

    acc = jnp.dot(h_scr[...], w_ref[...], preferred_element_type=F32)
    o_ref[...] = acc.astype(o_ref.dtype)
    if kv_block is not None:
        for which, cache_ref in enumerate(rest[:2]):
            @pl.when(j == kv_block + which)
            def _(cache_ref=cache_ref):
                n_batch, n_heads, seq, head_dim = cache_ref.shape
                for bi in range(n_batch):
                    for hd in range(n_heads):
                        cache_ref[bi, hd] = acc[bi * seq:(bi + 1) * seq, hd * head_dim:(hd + 1) * head_dim]


def in_projection(x2d, mod3, w_bf16, rows_per_cond, kv_block=None, cache_seq=None):
    m = x2d.shape[0]
    n = w_bf16.shape[1]
    if rows_per_cond is None:
        cond_row = lambda i: CTX_ROW
    else:
        blocks_per_cond = rows_per_cond // PROJ_TM
        cond_row = lambda i: i // blocks_per_cond
    tn = PROJ_TN
    if kv_block is None:
        tn = next((c for c in PROJ_TN_WIDE if n % c == 0), PROJ_TN)
    out_shape = [jax.ShapeDtypeStruct((m, n), BF16)]
    out_specs = [pl.BlockSpec((PROJ_TM, tn), lambda i, j: (i, j))]
    if kv_block is not None:
        assert PROJ_TN == HEADS * HEAD_DIM and PROJ_TM % cache_seq == 0
        per_block = PROJ_TM // cache_seq
        cache_spec = pl.BlockSpec((per_block, HEADS, cache_seq, HEAD_DIM), lambda i, j: (i, 0, 0, 0))
        out_shape += [jax.ShapeDtypeStruct((m // cache_seq, HEADS, cache_seq, HEAD_DIM), F32)] * 2
        out_specs += [cache_spec, cache_spec]
    outs = pl.pallas_call(
        functools.partial(_inproj_kernel, kv_block=kv_block),
        out_shape=out_shape,
        grid=(m // PROJ_TM, n // tn),
        in_specs=[pl.BlockSpec((PROJ_TM, D_MODEL), lambda i, j: (i, 0)),
                  pl.BlockSpec((1, 1, D_MODEL), lambda i, j: (cond_row(i), 0, 0)),
                  pl.BlockSpec((1, 1, D_MODEL), lambda i, j: (cond_row(i), 0, 1)),
                  pl.BlockSpec((D_MODEL, tn), lambda i, j: (0, j))],
        out_specs=out_specs,
        scratch_shapes=[pltpu.VMEM((PROJ_TM, D_MODEL), BF16)],
        compiler_params=_params("parallel", "arbitrary"),
        name="in_projection",
    )(x2d, mod3, mod3, w_bf16)
    return outs[0] if kv_block is None else outs


def _outproj_kernel(ya_ref, yb_ref, x_ref, g_ref, w_ref, lng_ref, lnb_ref, o_ref):
    half = ya_ref.shape[1]
    for r in range(OUT_TM // OUT_ROW_GROUP):
        rows = slice(r * OUT_ROW_GROUP, (r + 1) * OUT_ROW_GROUP)
        acc = jnp.dot(ya_ref[rows, :], w_ref[0:half, :], preferred_element_type=F32)
        acc = acc + jnp.dot(yb_ref[rows, :], w_ref[half:, :], preferred_element_type=F32)
        z = DEEPNORM_ALPHA * x_ref[rows, :] + g_ref[0] * acc
        mu = jnp.mean(z, axis=-1, keepdims=True)
        zc = z - mu
        var = jnp.mean(zc * zc, axis=-1, keepdims=True)
        o_ref[rows, :] = zc * lax.rsqrt(var + LN_EPS) * lng_ref[...] + lnb_ref[...]


def out_projection(ya, yb, x2d, mod3, w_bf16, ln_g, ln_b, rows_per_cond):
    m = x2d.shape[0]
    half = ya.shape[1]
    if rows_per_cond is None:
        cond_row = lambda i: CTX_ROW
    else:
        blocks_per_cond = rows_per_cond // OUT_TM
        cond_row = lambda i: i // blocks_per_cond
    return pl.pallas_call(
        _outproj_kernel,
        out_shape=jax.ShapeDtypeStruct((m, D_MODEL), F32),
        grid=(m // OUT_TM,),
        in_specs=[pl.BlockSpec((OUT_TM, half), lambda i: (i, 0)),
                  pl.BlockSpec((OUT_TM, half), lambda i: (i, 0)),
                  pl.BlockSpec((OUT_TM, D_MODEL), lambda i: (i, 0)),
                  pl.BlockSpec((1, 1, D_MODEL), lambda i: (cond_row(i), 0, 2)),
                  pl.BlockSpec((2 * half, D_MODEL), lambda i: (0, 0), pipeline_mode=pl.Buffered(1)),
                  pl.BlockSpec((1, D_MODEL), lambda i: (0, 0)),
                  pl.BlockSpec((1, D_MODEL), lambda i: (0, 0))],
        out_specs=pl.BlockSpec((OUT_TM, D_MODEL), lambda i: (i, 0)),
        compiler_params=_params("parallel"),
        name="out_projection",
    )(ya, yb, x2d, mod3, w_bf16, ln_g.reshape(1, D_MODEL), ln_b.reshape(1, D_MODEL))


def _pool_bands():
    assert max(POOL_WINDOWS) // 2 <= POOL_HALO
    i = np.arange(POOL_ROWS)[:, None]
    own = np.arange(POOL_ROWS)[None, :]
    mains = [(own >= i - w // 2) & (own < i + w // 2) for w in POOL_WINDOWS]
    top_i = np.arange(POOL_HALO)[:, None]
    bot_i = POOL_ROWS - POOL_HALO + top_i
    before = np.arange(POOL_HALO)[None, :] - POOL_HALO
    after = POOL_ROWS + np.arange(POOL_HALO)[None, :]
    halos = [np.stack([before >= top_i - w // 2, after < bot_i + w // 2]) for w in POOL_WINDOWS]
    return np.stack(mains).astype(np.float32), np.stack(halos).astype(np.float32)


def _pool_kernel(a_ref, g_ref, w_ref, s_ref, bm_ref, bh_ref, o_ref, *, seq):
    n_chunks = seq // POOL_ROWS
    group = pl.program_id(1)
    half = jnp.int32(0)
    for gi, window in enumerate(POOL_WINDOWS):
        half = jnp.where(group == gi, window // 2, half)

    def chunk(bi, ci):
        r0 = pl.multiple_of(ci * POOL_ROWS, POOL_ROWS)
        main = a_ref[bi, pl.ds(r0, POOL_ROWS), :]
        win = jnp.dot(bm_ref[0], main.astype(BF16), preferred_element_type=F32)
        if n_chunks > 1:
            p0 = pl.multiple_of(jnp.maximum(r0 - POOL_HALO, 0), POOL_HALO)
            n0 = pl.multiple_of(jnp.minimum(r0 + POOL_ROWS, seq - POOL_HALO), POOL_HALO)
            prev = jnp.where(ci > 0, a_ref[bi, pl.ds(p0, POOL_HALO), :].astype(F32), 0.0).astype(BF16)
            nxt = jnp.where(ci < n_chunks - 1, a_ref[bi, pl.ds(n0, POOL_HALO), :].astype(F32), 0.0).astype(BF16)
            top = win[0:POOL_HALO] + jnp.dot(bh_ref[0, 0], prev, preferred_element_type=F32)
            bot = win[POOL_ROWS - POOL_HALO:] + jnp.dot(bh_ref[0, 1], nxt, preferred_element_type=F32)
            win = jnp.concatenate([top, win[POOL_HALO:POOL_ROWS - POOL_HALO], bot], axis=0)
        t = r0 + lax.broadcasted_iota(jnp.int32, (POOL_ROWS, 1), 0)
        cnt = (jnp.minimum(t + half, seq) - jnp.maximum(t - half, 0)).astype(F32)
        pooled = win / cnt - main.astype(F32)
        mixed = jnp.dot(pooled.astype(BF16), w_ref[0], preferred_element_type=F32)
        gate = g_ref[bi, pl.ds(r0, POOL_ROWS), :].astype(F32)
        y = mixed * s_ref[...] * _silu(gate)
        o_ref[bi, pl.ds(r0, POOL_ROWS), :] = y.astype(o_ref.dtype)

    def group_of_chunks(it, carry):
        for u in range(POOL_CHUNKS_PER_ITER):
            idx = it * POOL_CHUNKS_PER_ITER + u
            chunk(idx // n_chunks, idx % n_chunks)
        return carry

    lax.fori_loop(0, a_ref.shape[0] * n_chunks // POOL_CHUNKS_PER_ITER, group_of_chunks, 0)


def pool_mixer(p3, w_pool_bf16, pool_scale):
    b, seq, _ = p3.shape
    groups = len(POOL_WINDOWS)
    band_main, band_halo = (jnp.asarray(m, BF16) for m in _pool_bands())
    bb = max(1, POOL_CHUNKS_PER_ITER * POOL_ROWS // seq)
    return pl.pallas_call(
        functools.partial(_pool_kernel, seq=seq),
        out_shape=jax.ShapeDtypeStruct((b, seq, groups * POOL_GROUP_DIM), BF16),
        grid=(b // bb, groups),
        in_specs=[pl.BlockSpec((bb, seq, POOL_GROUP_DIM), lambda i, g: (i, 0, g)),
                  pl.BlockSpec((bb, seq, POOL_GROUP_DIM), lambda i, g: (i, 0, groups + g)),
                  pl.BlockSpec((1, POOL_GROUP_DIM, POOL_GROUP_DIM), lambda i, g: (g, 0, 0)),
                  pl.BlockSpec((1, POOL_GROUP_DIM), lambda i, g: (0, g)),
                  pl.BlockSpec((1, POOL_ROWS, POOL_ROWS), lambda i, g: (g, 0, 0)),
                  pl.BlockSpec((1, 2, POOL_HALO, POOL_HALO), lambda i, g: (g, 0, 0, 0))],
        out_specs=pl.BlockSpec((bb, seq, POOL_GROUP_DIM), lambda i, g: (i, 0, g)),
        compiler_params=_params("parallel", "parallel"),
        name="pool_mixer",
    )(p3, p3, w_pool_bf16, pool_scale.reshape(1, groups * POOL_GROUP_DIM), band_main, band_halo)


def _rope_tables():
    t = np.arange(GRID_ROWS * GRID_W)
    row = (t // GRID_W).astype(np.float64)[:, None]
    col = (t % GRID_W).astype(np.float64)[:, None]
    axis_dim = QK_HALF // 2
    inv = 1.0 / (ROPE_BASE ** (np.arange(0, axis_dim, 2, dtype=np.float64) / axis_dim))
    lane = np.arange(HEAD_DIM)
    within = lane % QK_HALF
    pos = np.where((within < axis_dim)[None, :], row, col)
    ang = pos * inv[within % (axis_dim // 2)][None, :]
    is_a = ((within % axis_dim) < axis_dim // 2)[None, :]
    cos = np.cos(ang)
    sin = np.sin(ang)
    return (cos.astype(np.float32),
            np.where(is_a, -sin, 0.0).astype(np.float32),
            np.where(is_a, 0.0, sin).astype(np.float32))


def _rotate_lanes(x, cos, sin_a, sin_b):
    shift = QK_HALF // 4
    return (x * cos + pltpu.roll(x, HEAD_DIM - shift, 1) * sin_a
            + pltpu.roll(x, shift, 1) * sin_b)


def _diff_lambda(lam_ref, lam_init):
    dl = lam_ref[...]
    return (jnp.exp(jnp.sum(dl[0:1] * dl[1:2], axis=-1, keepdims=True))
            - jnp.exp(jnp.sum(dl[2:3] * dl[3:4], axis=-1, keepdims=True)) + lam_init)


def _split_components(q):
    lane = lax.broadcasted_iota(jnp.int32, (1, HEAD_DIM), 1)
    return (jnp.where(lane < QK_HALF, q, 0.0).astype(BF16),
            jnp.where(lane >= QK_HALF, q, 0.0).astype(BF16))


def _diff_combine(e1, e2, lam, v, sub, lam_init, gate):
    w1 = 1.0 / jnp.sum(e1, axis=-1, keepdims=True)
    w2 = lam / jnp.sum(e2, axis=-1, keepdims=True)
    p = (e1 * w1 - e2 * w2).astype(BF16)
    o = jnp.dot(p, v, preferred_element_type=F32)
    o = o * lax.rsqrt(jnp.mean(o * o, axis=-1, keepdims=True) + LN_EPS)
    o = o * sub * (1.0 - lam_init)
    return o * _silu(gate)


def _diff_latent_kernel(q_ref, k_ref, v_ref, g_ref, ck_ref, cv_ref, cos_ref, sa_ref, sb_ref,
                        lam_ref, sub_ref, o_ref, kt_scr, v_scr, *, seq, tq, lam_init):
    ones_col = jnp.ones((PREP_ROWS, HEAD_DIM), BF16)

    def stage(ci, carry):
        rows = pl.ds(pl.multiple_of(ci * PREP_ROWS, PREP_ROWS), PREP_ROWS)
        k = _rotate_lanes(k_ref[0, rows, :].astype(F32), cos_ref[rows, :], sa_ref[rows, :], sb_ref[rows, :])
        kt_scr[rows, :] = k.astype(BF16)
        v_scr[rows, 0:HEAD_DIM] = v_ref[0, rows, :].astype(BF16)
        v_scr[rows, HEAD_DIM:] = ones_col
        return carry

    lax.fori_loop(0, seq // PREP_ROWS, stage, 0)
    kt_scr[seq:, :] = ck_ref[0, 0].astype(BF16)
    v_scr[seq:, 0:HEAD_DIM] = cv_ref[0, 0].astype(BF16)
    v_scr[seq:, HEAD_DIM:] = ones_col[0:v_scr.shape[0] - seq]
    lam = _diff_lambda(lam_ref, lam_init)

    def tiles(it, carry):
        keys_t = kt_scr[...]
        values = v_scr[...]
        def rows_of(u):
            return pl.ds(pl.multiple_of((it * DIFF_TILES_PER_ITER + u) * tq, tq), tq)

        def scores(u):
            r = rows_of(u)
            pair = _split_components(
                _rotate_lanes(q_ref[0, r, :].astype(F32), cos_ref[r, :], sa_ref[r, :], sb_ref[r, :])
                * (QK_HALF ** -0.5 * LOG2E))
            return [_dot_nt(qc, keys_t) for qc in pair]

        def finish(u, s):
            r = rows_of(u)
            e = [jnp.exp2(sc - jnp.max(sc, axis=-1, keepdims=True)).astype(BF16) for sc in s]
            r1, r2 = [jnp.dot(ec, values, preferred_element_type=F32) for ec in e]
            o = (r1[:, 0:HEAD_DIM] * (1.0 / r1[:, HEAD_DIM:])
                 - r2[:, 0:HEAD_DIM] * (lam / r2[:, HEAD_DIM:]))
            o = o * lax.rsqrt(jnp.mean(o * o, axis=-1, keepdims=True) + LN_EPS)
            o = o * sub_ref[...] * (1.0 - lam_init)
            o_ref[0, r, :] = (o * _silu(g_ref[0, r, :].astype(F32))).astype(o_ref.dtype)

        pending = scores(0)
        for u in range(DIFF_TILES_PER_ITER):
            upcoming = scores(u + 1) if u + 1 < DIFF_TILES_PER_ITER else None
            finish(u, pending)
            pending = upcoming
        return carry

    lax.fori_loop(0, seq // (tq * DIFF_TILES_PER_ITER), tiles, 0)


def diff_attention_latent(p3, diff_lam, diff_subln, layer_idx, ctx_k, ctx_v):
    b, seq, _ = p3.shape
    ctx_len = ctx_k.shape[2]
    tq = DIFF_TQ
    nk = seq + ctx_len
    lam_init = 0.8 - 0.6 * math.exp(-0.3 * layer_idx)
    q0, k0, v0, g0 = 2 * HEADS, 3 * HEADS, 4 * HEADS, 5 * HEADS
    const = lambda i, h: (0, 0)
    head_blk = lambda off: pl.BlockSpec((1, seq, HEAD_DIM), lambda i, h: (i, 0, off + h))
    ctx_spec = pl.BlockSpec((1, 1, ctx_len, HEAD_DIM), lambda i, h: (i, h, 0, 0))
    tab_spec = pl.BlockSpec((seq, HEAD_DIM), const, pipeline_mode=pl.Buffered(1))
    return pl.pallas_call(
        functools.partial(_diff_latent_kernel, seq=seq, tq=tq, lam_init=lam_init),
        out_shape=jax.ShapeDtypeStruct((b, seq, HEADS * HEAD_DIM), BF16),
        grid=(b, HEADS),
        in_specs=[head_blk(q0), head_blk(k0), head_blk(v0), head_blk(g0),
                  ctx_spec, ctx_spec, tab_spec, tab_spec, tab_spec,
                  pl.BlockSpec(diff_lam.shape, const),
                  pl.BlockSpec((1, HEAD_DIM), const)],
        out_specs=pl.BlockSpec((1, seq, HEAD_DIM), lambda i, h: (i, 0, h)),
        scratch_shapes=[pltpu.VMEM((nk, HEAD_DIM), BF16), pltpu.VMEM((nk, 2 * HEAD_DIM), BF16)],
        compiler_params=_params("parallel", "parallel"),
        name="diff_attention_latent",
    )(p3, p3, p3, p3, ctx_k, ctx_v, *[jnp.asarray(tab) for tab in _rope_tables()],
      diff_lam, diff_subln.reshape(1, HEAD_DIM))


def _diff_context_kernel(q_ref, k_ref, v_ref, g_ref, lam_ref, sub_ref, o_ref, *, lam_init):
    lam = _diff_lambda(lam_ref, lam_init)
    heads = [slice(h * HEAD_DIM, (h + 1) * HEAD_DIM) for h in range(HEADS)]
    seq = q_ref.shape[1]
    ones_col = jnp.ones((seq, HEAD_DIM), BF16)

    def one(bi, carry):
        def scores(group):
            qs = [_split_components(q_ref[bi, :, c].astype(F32) * (QK_HALF ** -0.5 * LOG2E)) for c in group]
            return [[_dot_nt(qc, k_ref[bi, :, c]) for qc in pair] for pair, c in zip(qs, group)]

        def finish(group, s):
            e = [[jnp.exp2(sc - jnp.max(sc, axis=-1, keepdims=True)).astype(BF16) for sc in sh] for sh in s]
            v_aug = [jnp.concatenate([v_ref[bi, :, c], ones_col], axis=-1) for c in group]
            r = [[jnp.dot(ec, va, preferred_element_type=F32) for ec in eh] for eh, va in zip(e, v_aug)]
            for (r1, r2), c in zip(r, group):
                o = (r1[:, 0:HEAD_DIM] * (1.0 / r1[:, HEAD_DIM:])
                     - r2[:, 0:HEAD_DIM] * (lam / r2[:, HEAD_DIM:]))
                o = o * lax.rsqrt(jnp.mean(o * o, axis=-1, keepdims=True) + LN_EPS)
                o = o * sub_ref[...] * (1.0 - lam_init)
                o_ref[bi, :, c] = (o * _silu(g_ref[bi, :, c].astype(F32))).astype(o_ref.dtype)

        groups = [heads[i:i + CTX_HEAD_GROUP] for i in range(0, HEADS, CTX_HEAD_GROUP)]
        pending = scores(groups[0])
        for n, group in enumerate(groups):
            upcoming = scores(groups[n + 1]) if n + 1 < len(groups) else None
            finish(group, pending)
            pending = upcoming
        return carry

    lax.fori_loop(0, q_ref.shape[0], one, 0)


def diff_attention_context(p3, diff_lam, diff_subln, layer_idx):
    b, seq, _ = p3.shape
    width = HEADS * HEAD_DIM
    lam_init = 0.8 - 0.6 * math.exp(-0.3 * layer_idx)
    bb = CTX_BATCH_PER_STEP
    blk = lambda c: pl.BlockSpec((bb, seq, width), lambda i: (i, 0, c))
    return pl.pallas_call(
        functools.partial(_diff_context_kernel, lam_init=lam_init),
        out_shape=jax.ShapeDtypeStruct((b, seq, width), BF16),
        grid=(b // bb,),
        in_specs=[blk(2), blk(3), blk(4), blk(5),
                  pl.BlockSpec(diff_lam.shape, lambda i: (0, 0)),
                  pl.BlockSpec((1, HEAD_DIM), lambda i: (0, 0))],
        out_specs=pl.BlockSpec((bb, seq, width), lambda i: (i, 0, 0)),
        compiler_params=_params("parallel"),
        name="diff_attention_context",
    )(p3, p3, p3, p3, diff_lam, diff_subln.reshape(1, HEAD_DIM))


def _ctx_attn_kernel(q_ref, k_ref, v_ref, g_ref, o_ref):
    heads = [slice(h * HEAD_DIM, (h + 1) * HEAD_DIM) for h in range(HEADS)]
    seq = q_ref.shape[1]
    ones_col = jnp.ones((seq, HEAD_DIM), BF16)

    def one(bi, carry):
        q = [(q_ref[bi, :, c].astype(F32) * (HEAD_DIM ** -0.5 * LOG2E)).astype(BF16) for c in heads]
        s = [_dot_nt(q[h], k_ref[bi, :, c]) for h, c in enumerate(heads)]
        e = [jnp.exp2(sh - jnp.max(sh, axis=-1, keepdims=True)).astype(BF16) for sh in s]
        r = [jnp.dot(e[h], jnp.concatenate([v_ref[bi, :, c], ones_col], axis=-1), preferred_element_type=F32)
             for h, c in enumerate(heads)]
        for h, c in enumerate(heads):
            o = r[h][:, 0:HEAD_DIM] / r[h][:, HEAD_DIM:]
            o_ref[bi, :, c] = (o * _silu(g_ref[bi, :, c].astype(F32))).astype(o_ref.dtype)
        return carry

    lax.fori_loop(0, q_ref.shape[0], one, 0)


def context_attention(p3):
    b, seq, _ = p3.shape
    width = HEADS * HEAD_DIM
    bb = CTX_BATCH_PER_STEP
    blk = lambda c: pl.BlockSpec((bb, seq, width), lambda i: (i, 0, c))
    return pl.pallas_call(
        _ctx_attn_kernel,
        out_shape=jax.ShapeDtypeStruct((b, seq, width), BF16),
        grid=(b // bb,),
        in_specs=[blk(0), blk(1), blk(2), blk(3)],
        out_specs=pl.BlockSpec((bb, seq, width), lambda i: (i, 0, 0)),
        compiler_params=_params("parallel"),
        name="context_attention",
    )(p3, p3, p3, p3)


NA_TILE_Q = NA_Q_ROWS * GRID_W
NA_TILE_K = NA_K_ROWS * GRID_W
NA_TILES = GRID_ROWS // NA_Q_ROWS
NA_TILES_PER_ITER = 4
NA_CASES = 3
NA_DR = 2 * NA_WIN_R - 1


def _na_key_row_start(tile):
    return int(np.clip(NA_Q_ROWS * tile - NA_WIN_R // 2, 0, GRID_ROWS - NA_K_ROWS))


def _na_bias_layout():
    dr = np.full((NA_CASES, NA_Q_ROWS, NA_K_ROWS), NA_DR, np.int32)
    for case, tile in enumerate((0, 1, NA_TILES - 1)):
        ks = _na_key_row_start(tile)
        for qr_local in range(NA_Q_ROWS):
            qr = NA_Q_ROWS * tile + qr_local
            rs = int(np.clip(qr - NA_WIN_R // 2, 0, GRID_ROWS - NA_WIN_R))
            for kr_local in range(NA_K_ROWS):
                kr = ks + kr_local
                if rs <= kr < rs + NA_WIN_R:
                    dr[case, qr_local, kr_local] = kr - qr + NA_WIN_R - 1
    pairs = sorted({(int(a), int(b)) for a, b in zip(dr[..., 0::2].ravel(), dr[..., 1::2].ravel())})
    index = {p: i for i, p in enumerate(pairs)}
    block = np.array([[[index[(int(dr[c, q, 2 * p]), int(dr[c, q, 2 * p + 1]))]
                        for p in range(NA_K_ROWS // 2)] for q in range(NA_Q_ROWS)] for c in range(NA_CASES)])
    return pairs, block


def _na_bias_blocks(rpb):
    qc = np.arange(GRID_W)[:, None]
    kc = np.arange(GRID_W)[None, :]
    cstart = np.clip(qc - NA_WIN_C // 2, 0, GRID_W - NA_WIN_C)
    col_valid = (kc >= cstart) & (kc < cstart + NA_WIN_C)
    dc = np.clip(kc - qc + NA_WIN_C - 1, 0, 2 * NA_WIN_C - 2)
    onehot = (dc.reshape(-1)[None, :] == np.arange(2 * NA_WIN_C - 1)[:, None]).astype(np.float32)
    toeplitz = jnp.einsum('hrj,jn->hrn', rpb, jnp.asarray(onehot), precision=lax.Precision.HIGHEST)
    per_dr = jnp.where(jnp.asarray(col_valid), toeplitz.reshape(rpb.shape[:2] + dc.shape), NEG_INF)
    per_dr = jnp.concatenate([per_dr, jnp.full_like(per_dr[:, :1], NEG_INF)], axis=1)
    pairs, _ = _na_bias_layout()
    left = np.array([p[0] for p in pairs])
    right = np.array([p[1] for p in pairs])
    return jnp.concatenate([per_dr[:, left], per_dr[:, right]], axis=-1)


def _na_kernel(q_ref, k_ref, v_ref, g_ref, ck_ref, cv_ref, tab_ref, o_ref,
               bias_scr, ck_scr, cv_scr, v_scr):
    b = pl.program_id(1)
    seq = v_ref.shape[1]

    @pl.when(b == 0)
    def _build_bias():
        _, block = _na_bias_layout()
        for case in range(NA_CASES):
            for qr in range(NA_Q_ROWS):
                for p in range(NA_K_ROWS // 2):
                    bias_scr[case, qr * GRID_W:(qr + 1) * GRID_W, p * 128:(p + 1) * 128] = (
                        tab_ref[0, int(block[case, qr, p])] * LOG2E)

    ones_col = jnp.ones((PREP_ROWS, HEAD_DIM), BF16)
    ck_scr[...] = ck_ref[0, 0].astype(BF16)
    cv_scr[:, 0:HEAD_DIM] = cv_ref[0, 0].astype(BF16)
    cv_scr[:, HEAD_DIM:] = ones_col[0:cv_scr.shape[0]]

    def stage(ci, carry):
        rows = pl.ds(pl.multiple_of(ci * PREP_ROWS, PREP_ROWS), PREP_ROWS)
        v_scr[rows, 0:HEAD_DIM] = v_ref[0, rows, :]
        v_scr[rows, HEAD_DIM:] = ones_col
        return carry

    lax.fori_loop(0, seq // PREP_ROWS, stage, 0)
    half_rows = NA_TILE_Q // 2

    def tiles(it, carry):
        k_ctx = ck_scr[...]
        v_ctx = cv_scr[...]

        def window(u):
            t = it * NA_TILES_PER_ITER + u
            case = jnp.where(t == 0, 0, jnp.where(t == NA_TILES - 1, 2, 1))
            key_row = jnp.clip(NA_Q_ROWS * t - NA_WIN_R // 2, 0, GRID_ROWS - NA_K_ROWS)
            keys = pl.ds(pl.multiple_of(key_row * GRID_W, 256), NA_TILE_K)
            rows = [pl.ds(pl.multiple_of(t * NA_TILE_Q + i * half_rows, half_rows), half_rows) for i in range(2)]
            return case, keys, rows

        def scores(u):
            case, keys, rows = window(u)
            k_loc = k_ref[0, keys, :]
            q = [(q_ref[0, r, :].astype(F32) * (HEAD_DIM ** -0.5 * LOG2E)).astype(BF16) for r in rows]
            s_loc = [_dot_nt(qh, k_loc) + bias_scr[case, i * half_rows:(i + 1) * half_rows, :]
                     for i, qh in enumerate(q)]
            s_ctx = [_dot_nt(qh, k_ctx) for qh in q]
            return s_loc, s_ctx

        def finish(u, s):
            _, keys, rows = window(u)
            s_loc, s_ctx = s
            m = [jnp.maximum(jnp.max(sl, axis=-1, keepdims=True), jnp.max(sc, axis=-1, keepdims=True))
                 for sl, sc in zip(s_loc, s_ctx)]
            e_loc = [jnp.exp2(sl - mh).astype(BF16) for sl, mh in zip(s_loc, m)]
            e_ctx = [jnp.exp2(sc - mh).astype(BF16) for sc, mh in zip(s_ctx, m)]
            v_loc = v_scr[keys, :]
            res = [jnp.dot(el, v_loc, preferred_element_type=F32) + jnp.dot(ec, v_ctx, preferred_element_type=F32)
                   for el, ec in zip(e_loc, e_ctx)]
            for r, rh in zip(rows, res):
                o = rh[:, 0:HEAD_DIM] / rh[:, HEAD_DIM:]
                o_ref[0, r, :] = (o * _silu(g_ref[0, r, :].astype(F32))).astype(o_ref.dtype)

        pending = scores(0)
        for u in range(NA_TILES_PER_ITER):
            upcoming = scores(u + 1) if u + 1 < NA_TILES_PER_ITER else None
            finish(u, pending)
            pending = upcoming
        return carry

    lax.fori_loop(0, NA_TILES // NA_TILES_PER_ITER, tiles, 0)


def neighborhood_attention(p3, ctx_k, ctx_v, rpb):
    b, seq, _ = p3.shape
    ctx_len = ctx_k.shape[2]
    tab = _na_bias_blocks(rpb)
    n_pairs = tab.shape[1]
    ctx_spec = pl.BlockSpec((1, 1, ctx_len, HEAD_DIM), lambda h, i: (i, h, 0, 0))
    head_blk = lambda off: pl.BlockSpec((1, seq, HEAD_DIM), lambda h, i: (i, 0, off + h))
    return pl.pallas_call(
        _na_kernel,
        out_shape=jax.ShapeDtypeStruct((b, seq, HEADS * HEAD_DIM), BF16),
        grid=(HEADS, b),
        in_specs=[head_blk(0), head_blk(HEADS), head_blk(2 * HEADS), head_blk(3 * HEADS),
                  ctx_spec, ctx_spec,
                  pl.BlockSpec((1, n_pairs, GRID_W, 2 * GRID_W), lambda h, i: (h, 0, 0, 0))],
        out_specs=pl.BlockSpec((1, seq, HEAD_DIM), lambda h, i: (i, 0, h)),
        scratch_shapes=[pltpu.VMEM((NA_CASES, NA_TILE_Q, NA_TILE_K), F32),
                        pltpu.VMEM((ctx_len, HEAD_DIM), BF16), pltpu.VMEM((ctx_len, 2 * HEAD_DIM), BF16),
                        pltpu.VMEM((seq, 2 * HEAD_DIM), BF16)],
        compiler_params=_params("arbitrary", "arbitrary"),
        name="neighborhood_attention",
    )(p3, p3, p3, p3, ctx_k, ctx_v, tab)


def _sgu_kernel(u_ref, v_ref, g_ref, ln_ref, w_ref, b_ref, o_ref):
    v = v_ref[...].astype(F32)
    mu = jnp.mean(v, axis=-1, keepdims=True)
    vc = v - mu
    var = jnp.mean(vc * vc, axis=-1, keepdims=True)
    vn = (vc * lax.rsqrt(var + LN_EPS) * ln_ref[...]).astype(BF16)
    groups = w_ref.shape[0]
    for n in range(v.shape[0] // SGU_CHUNK):
        rows = slice(n * SGU_CHUNK, (n + 1) * SGU_CHUNK)
        for g in range(groups):
            cols = slice(g * SGU_GROUP_DIM, (g + 1) * SGU_GROUP_DIM)
            s = jnp.dot(w_ref[g], vn[rows, cols], preferred_element_type=F32) + b_ref[:, g:g + 1]
            y = u_ref[rows, cols].astype(F32) * s * _silu(g_ref[rows, cols].astype(F32))
            o_ref[rows, cols] = y.astype(o_ref.dtype)


def spatial_gating(p2, sgu_ln, w_s_bf16, b_s):
    m = p2.shape[0]
    groups = w_s_bf16.shape[0]
    width = groups * SGU_GROUP_DIM
    blk = lambda c: pl.BlockSpec((SGU_TM, width), lambda i: (i, c))
    return pl.pallas_call(
        _sgu_kernel,
        out_shape=jax.ShapeDtypeStruct((m, width), BF16),
        grid=(m // SGU_TM,),
        in_specs=[blk(4), blk(5), blk(6),
                  pl.BlockSpec((1, width), lambda i: (0, 0)),
                  pl.BlockSpec((groups, SGU_CHUNK, SGU_CHUNK), lambda i: (0, 0, 0)),
                  pl.BlockSpec((SGU_CHUNK, groups), lambda i: (0, 0))],
        out_specs=pl.BlockSpec((SGU_TM, width), lambda i: (i, 0)),
        compiler_params=_params("parallel"),
        name="spatial_gating",
    )(p2, p2, p2, sgu_ln.reshape(1, width), w_s_bf16, jnp.transpose(b_s))


def kernel(x_prompt, x_sample, cache_k_l0, cache_v_l0, cache_k_l1, cache_v_l1, cache_k_l2, cache_v_l2, cache_k_l3, cache_v_l3, c, c_ctx, w_mod_0, b_mod_0, w_in_0, w_out_0, ln_g_0, ln_b_0, pool_w_0, pool_scale_0, diff_lam_0, diff_subln_0, w_mod_1, b_mod_1, w_in_1, w_out_1, ln_g_1, ln_b_1, rpb_1, sgu_ln_1, sgu_w_1, sgu_b_1, w_mod_2, b_mod_2, w_in_2, w_out_2, ln_g_2, ln_b_2, pool_w_2, pool_scale_2, diff_lam_2, diff_subln_2, w_mod_3, b_mod_3, w_in_3, w_out_3, ln_g_3, ln_b_3, rpb_3, sgu_ln_3, sgu_w_3, sgu_b_3):
    cache_k = [cache_k_l0, cache_k_l1, cache_k_l2, cache_k_l3]
    cache_v = [cache_v_l0, cache_v_l1, cache_v_l2, cache_v_l3]
    w_mod = [w_mod_0, w_mod_1, w_mod_2, w_mod_3]
    b_mod = [b_mod_0, b_mod_1, b_mod_2, b_mod_3]
    w_in = [w_in_0, w_in_1, w_in_2, w_in_3]
    w_out = [w_out_0, w_out_1, w_out_2, w_out_3]
    ln_g = [ln_g_0, ln_g_1, ln_g_2, ln_g_3]
    ln_b = [ln_b_0, ln_b_1, ln_b_2, ln_b_3]
    even_p = {0: (pool_w_0, pool_scale_0, diff_lam_0, diff_subln_0),
              2: (pool_w_2, pool_scale_2, diff_lam_2, diff_subln_2)}
    odd_p = {1: (rpb_1, sgu_ln_1, sgu_w_1, sgu_b_1),
             3: (rpb_3, sgu_ln_3, sgu_w_3, sgu_b_3)}

    bp, lp, _ = x_prompt.shape
    bs, ls, _ = x_sample.shape
    xp = x_prompt.reshape(bp * lp, D_MODEL)
    xs = x_sample.reshape(bs * ls, D_MODEL)
    cond = jnp.concatenate([c, c_ctx[None, :], jnp.zeros((COND_ROWS - bs - 1, D_MODEL), F32)], axis=0)

    new_k, new_v = [], []
    for l in range(DEPTH):
        mod3 = ada_params(cond, w_mod[l], b_mod[l]).reshape(COND_ROWS, 1, 3 * D_MODEL)
        w_in_l = w_in[l].astype(BF16)
        n_in = w_in_l.shape[1]
        kv_block = (3 if l % 2 == 0 else 1) * HEADS * HEAD_DIM // PROJ_TN
        pp, nk, nv = in_projection(xp, mod3, w_in_l, None, kv_block=kv_block, cache_seq=lp)
        ps = in_projection(xs, mod3, w_in_l, ls)
        pp3 = pp.reshape(bp, lp, n_in)
        ps3 = ps.reshape(bs, ls, n_in)
        if l % 2 == 0:
            pool_w, pool_scale, diff_lam, diff_subln = even_p[l]
            pool_w = pool_w.astype(BF16)
            ya_p = pool_mixer(pp3, pool_w, pool_scale)
            ya_s = pool_mixer(ps3, pool_w, pool_scale)
            yb_p = diff_attention_context(pp3, diff_lam, diff_subln, l)
            yb_s = diff_attention_latent(ps3, diff_lam, diff_subln, l, cache_k[l], cache_v[l])
        else:
            rpb, sgu_ln, sgu_w, sgu_b = odd_p[l]
            sgu_w = sgu_w.astype(BF16)
            ya_p = context_attention(pp3)
            ya_s = neighborhood_attention(ps3, cache_k[l], cache_v[l], rpb)
            yb_p = spatial_gating(pp, sgu_ln, sgu_w, sgu_b)
            yb_s = spatial_gating(ps, sgu_ln, sgu_w, sgu_b)
        half = ya_p.shape[-1]
        w_out_l = w_out[l].astype(BF16)
        xp = out_projection(ya_p.reshape(bp * lp, half), yb_p.reshape(bp * lp, half), xp, mod3,
                            w_out_l, ln_g[l], ln_b[l], None)
        xs = out_projection(ya_s.reshape(bs * ls, half), yb_s.reshape(bs * ls, half), xs, mod3,
                            w_out_l, ln_g[l], ln_b[l], ls)
        new_k.append(nk)
        new_v.append(nv)

    return (xp.reshape(bp, lp, D_MODEL), xs.reshape(bs, ls, D_MODEL),
            new_k[0], new_v[0], new_k[1], new_v[1], new_k[2], new_v[2], new_k[3], new_v[3])
```

```python
import functools
import math

import numpy as np
import jax
import jax.numpy as jnp
from jax import lax
from jax.experimental import pallas as pl
from jax.experimental.pallas import tpu as pltpu

F32 = jnp.float32
BF16 = jnp.bfloat16

D_MODEL = 2048
DEPTH = 4
GRID_W = 64
GRID_ROWS = 64
HEADS = 8
HEAD_DIM = 128
QK_HALF = 64
POOL_WINDOWS = (2, 4, 8, 16)
POOL_GROUP_DIM = 256
SGU_CHUNK = 128
SGU_GROUP_DIM = 256
NA_WIN_R = 8
NA_WIN_C = 16
NA_Q_ROWS = 8
NA_K_ROWS = 16
ROPE_BASE = 10000.0
LN_EPS = 1e-5
NEG_INF = -1e30
DEEPNORM_ALPHA = (2 * DEPTH) ** 0.25
LOG2E = math.log2(math.e)

VMEM_LIMIT_BYTES = 56 * 1024 * 1024
COND_ROWS = 8
CTX_ROW = 4

PROJ_TM = 1024
PROJ_TN = 1024
PROJ_TN_WIDE = (2048, 1792)
OUT_TM = 1024
OUT_ROW_GROUP = 256
ADA_TN = 1536
POOL_ROWS = 256
POOL_HALO = 16
POOL_CHUNKS_PER_ITER = 8
SGU_TM = 1024
DIFF_TQ = 256
DIFF_TILES_PER_ITER = 4
PREP_ROWS = 512
CTX_BATCH_PER_STEP = 4
CTX_HEAD_GROUP = 2


def _params(*sem):
    return pltpu.CompilerParams(dimension_semantics=sem, vmem_limit_bytes=VMEM_LIMIT_BYTES)


def _silu(x):
    half_x = 0.5 * x
    return half_x * jnp.tanh(half_x) + half_x


def _dot_nt(a, b):
    return lax.dot_general(a, b, (((1,), (1,)), ((), ())), preferred_element_type=F32)


def _ada_kernel(c_ref, w_ref, b_ref, o_ref):
    a = _silu(c_ref[...]).astype(BF16)
    o_ref[...] = jnp.dot(a, w_ref[...].astype(BF16), preferred_element_type=F32) + b_ref[...]


def ada_params(cond, w_mod, b_mod):
    n = w_mod.shape[1]
    return pl.pallas_call(
        _ada_kernel,
        out_shape=jax.ShapeDtypeStruct((COND_ROWS, n), F32),
        grid=(n // ADA_TN,),
        in_specs=[pl.BlockSpec((COND_ROWS, D_MODEL), lambda j: (0, 0)),
                  pl.BlockSpec((D_MODEL, ADA_TN), lambda j: (0, j)),
                  pl.BlockSpec((1, ADA_TN), lambda j: (0, j))],
        out_specs=pl.BlockSpec((COND_ROWS, ADA_TN), lambda j: (0, j)),
        compiler_params=_params("parallel"),
        name="ada_params",
    )(cond, w_mod, b_mod.reshape(1, n))


def _inproj_kernel(x_ref, sh_ref, sc_ref, w_ref, o_ref, *rest, kv_block):
    h_scr = rest[-1]
    j = pl.program_id(1)

    @pl.when(j == 0)
    def _():
        h = x_ref[...] * (1.0 + sc_ref[0]) + sh_ref[0]
        h_scr[...] = h.astype(BF16)

    acc = jnp.dot(h_scr[...], w_ref[...], preferred_element_type=F32)
    o_ref[...] = acc.astype(o_ref.dtype)
    if kv_block is not None:
        for which, cache_ref in enumerate(rest[:2]):
            @pl.when(j == kv_block + which)
            def _(cache_ref=cache_ref):
                n_batch, n_heads, seq, head_dim = cache_ref.shape
                for bi in range(n_batch):
                    for hd in range(n_heads):
                        cache_ref[bi, hd] = acc[bi * seq:(bi + 1) * seq, hd * head_dim:(hd + 1) * head_dim]


def in_projection(x2d, mod3, w_bf16, rows_per_cond, kv_block=None, cache_seq=None):
    m = x2d.shape[0]
    n = w_bf16.shape[1]
    if rows_per_cond is None:
        cond_row = lambda i: CTX_ROW
    else:
        blocks_per_cond = rows_per_cond // PROJ_TM
        cond_row = lambda i: i // blocks_per_cond
    tn = PROJ_TN
    if kv_block is None:
        tn = next((c for c in PROJ_TN_WIDE if n % c == 0), PROJ_TN)
    out_shape = [jax.ShapeDtypeStruct((m, n), BF16)]
    out_specs = [pl.BlockSpec((PROJ_TM, tn), lambda i, j: (i, j))]
    if kv_block is not None:
        assert PROJ_TN == HEADS * HEAD_DIM and PROJ_TM % cache_seq == 0
        per_block = PROJ_TM // cache_seq
        cache_spec = pl.BlockSpec((per_block, HEADS, cache_seq, HEAD_DIM), lambda i, j: (i, 0, 0, 0))
        out_shape += [jax.ShapeDtypeStruct((m // cache_seq, HEADS, cache_seq, HEAD_DIM), F32)] * 2
        out_specs += [cache_spec, cache_spec]
    outs = pl.pallas_call(
        functools.partial(_inproj_kernel, kv_block=kv_block),
        out_shape=out_shape,
        grid=(m // PROJ_TM, n // tn),
        in_specs=[pl.BlockSpec((PROJ_TM, D_MODEL), lambda i, j: (i, 0)),
                  pl.BlockSpec((1, 1, D_MODEL), lambda i, j: (cond_row(i), 0, 0)),
                  pl.BlockSpec((1, 1, D_MODEL), lambda i, j: (cond_row(i), 0, 1)),
                  pl.BlockSpec((D_MODEL, tn), lambda i, j: (0, j))],
        out_specs=out_specs,
        scratch_shapes=[pltpu.VMEM((PROJ_TM, D_MODEL), BF16)],
        compiler_params=_params("parallel", "arbitrary"),
        name="in_projection",
    )(x2d, mod3, mod3, w_bf16)
    return outs[0] if kv_block is None else outs


def _outproj_kernel(ya_ref, yb_ref, x_ref, g_ref, w_ref, lng_ref, lnb_ref, o_ref):
    half = ya_ref.shape[1]
    for r in range(OUT_TM // OUT_ROW_GROUP):
        rows = slice(r * OUT_ROW_GROUP, (r + 1) * OUT_ROW_GROUP)
        acc = jnp.dot(ya_ref[rows, :], w_ref[0:half, :], preferred_element_type=F32)
        acc = acc + jnp.dot(yb_ref[rows, :], w_ref[half:, :], preferred_element_type=F32)
        z = DEEPNORM_ALPHA * x_ref[rows, :] + g_ref[0] * acc
        mu = jnp.mean(z, axis=-1, keepdims=True)
        zc = z - mu
        var = jnp.mean(zc * zc, axis=-1, keepdims=True)
        o_ref[rows, :] = zc * lax.rsqrt(var + LN_EPS) * lng_ref[...] + lnb_ref[...]


def out_projection(ya, yb, x2d, mod3, w_bf16, ln_g, ln_b, rows_per_cond):
    m = x2d.shape[0]
    half = ya.shape[1]
    if rows_per_cond is None:
        cond_row = lambda i: CTX_ROW
    else:
        blocks_per_cond = rows_per_cond // OUT_TM
        cond_row = lambda i: i // blocks_per_cond
    return pl.pallas_call(
        _outproj_kernel,
        out_shape=jax.ShapeDtypeStruct((m, D_MODEL), F32),
        grid=(m // OUT_TM,),
        in_specs=[pl.BlockSpec((OUT_TM, half), lambda i: (i, 0)),
                  pl.BlockSpec((OUT_TM, half), lambda i: (i, 0)),
                  pl.BlockSpec((OUT_TM, D_MODEL), lambda i: (i, 0)),
                  pl.BlockSpec((1, 1, D_MODEL), lambda i: (cond_row(i), 0, 2)),
                  pl.BlockSpec((2 * half, D_MODEL), lambda i: (0, 0), pipeline_mode=pl.Buffered(1)),
                  pl.BlockSpec((1, D_MODEL), lambda i: (0, 0)),
                  pl.BlockSpec((1, D_MODEL), lambda i: (0, 0))],
        out_specs=pl.BlockSpec((OUT_TM, D_MODEL), lambda i: (i, 0)),
        compiler_params=_params("parallel"),
        name="out_projection",
    )(ya, yb, x2d, mod3, w_bf16, ln_g.reshape(1, D_MODEL), ln_b.reshape(1, D_MODEL))


def _pool_bands():
    assert max(POOL_WINDOWS) // 2 <= POOL_HALO
    i = np.arange(POOL_ROWS)[:, None]
    own = np.arange(POOL_ROWS)[None, :]
    mains = [(own >= i - w // 2) & (own < i + w // 2) for w in POOL_WINDOWS]
    top_i = np.arange(POOL_HALO)[:, None]
    bot_i = POOL_ROWS - POOL_HALO + top_i
    before = np.arange(POOL_HALO)[None, :] - POOL_HALO
    after = POOL_ROWS + np.arange(POOL_HALO)[None, :]
    halos = [np.stack([before >= top_i - w // 2, after < bot_i + w // 2]) for w in POOL_WINDOWS]
    return np.stack(mains).astype(np.float32), np.stack(halos).astype(np.float32)


def _pool_kernel(a_ref, g_ref, w_ref, s_ref, bm_ref, bh_ref, o_ref, *, seq):
    n_chunks = seq // POOL_ROWS
    group = pl.program_id(1)
    half = jnp.int32(0)
    for gi, window in enumerate(POOL_WINDOWS):
        half = jnp.where(group == gi, window // 2, half)

    def chunk(bi, ci):
        r0 = pl.multiple_of(ci * POOL_ROWS, POOL_ROWS)
        main = a_ref[bi, pl.ds(r0, POOL_ROWS), :]
        win = jnp.dot(bm_ref[0], main.astype(BF16), preferred_element_type=F32)
        if n_chunks > 1:
            p0 = pl.multiple_of(jnp.maximum(r0 - POOL_HALO, 0), POOL_HALO)
            n0 = pl.multiple_of(jnp.minimum(r0 + POOL_ROWS, seq - POOL_HALO), POOL_HALO)
            prev = jnp.where(ci > 0, a_ref[bi, pl.ds(p0, POOL_HALO), :].astype(F32), 0.0).astype(BF16)
            nxt = jnp.where(ci < n_chunks - 1, a_ref[bi, pl.ds(n0, POOL_HALO), :].astype(F32), 0.0).astype(BF16)
            top = win[0:POOL_HALO] + jnp.dot(bh_ref[0, 0], prev, preferred_element_type=F32)
            bot = win[POOL_ROWS - POOL_HALO:] + jnp.dot(bh_ref[0, 1], nxt, preferred_element_type=F32)
            win = jnp.concatenate([top, win[POOL_HALO:POOL_ROWS - POOL_HALO], bot], axis=0)
        t = r0 + lax.broadcasted_iota(jnp.int32, (POOL_ROWS, 1), 0)
        cnt = (jnp.minimum(t + half, seq) - jnp.maximum(t - half, 0)).astype(F32)
        pooled = win / cnt - main.astype(F32)
        mixed = jnp.dot(pooled.astype(BF16), w_ref[0], preferred_element_type=F32)
        gate = g_ref[bi, pl.ds(r0, POOL_ROWS), :].astype(F32)
        y = mixed * s_ref[...] * _silu(gate)
        o_ref[bi, pl.ds(r0, POOL_ROWS), :] = y.astype(o_ref.dtype)

    def group_of_chunks(it, carry):
        for u in range(POOL_CHUNKS_PER_ITER):
            idx = it * POOL_CHUNKS_PER_ITER + u
            chunk(idx // n_chunks, idx % n_chunks)
        return carry

    lax.fori_loop(0, a_ref.shape[0] * n_chunks // POOL_CHUNKS_PER_ITER, group_of_chunks, 0)


def pool_mixer(p3, w_pool_bf16, pool_scale):
    b, seq, _ = p3.shape
    groups = len(POOL_WINDOWS)
    band_main, band_halo = (jnp.asarray(m, BF16) for m in _pool_bands())
    bb = max(1, POOL_CHUNKS_PER_ITER * POOL_ROWS // seq)
    return pl.pallas_call(
        functools.partial(_pool_kernel, seq=seq),
        out_shape=jax.ShapeDtypeStruct((b, seq, groups * POOL_GROUP_DIM), BF16),
        grid=(b // bb, groups),
        in_specs=[pl.BlockSpec((bb, seq, POOL_GROUP_DIM), lambda i, g: (i, 0, g)),
                  pl.BlockSpec((bb, seq, POOL_GROUP_DIM), lambda i, g: (i, 0, groups + g)),
                  pl.BlockSpec((1, POOL_GROUP_DIM, POOL_GROUP_DIM), lambda i, g: (g, 0, 0)),
                  pl.BlockSpec((1, POOL_GROUP_DIM), lambda i, g: (0, g)),
                  pl.BlockSpec((1, POOL_ROWS, POOL_ROWS), lambda i, g: (g, 0, 0)),
                  pl.BlockSpec((1, 2, POOL_HALO, POOL_HALO), lambda i, g: (g, 0, 0, 0))],
        out_specs=pl.BlockSpec((bb, seq, POOL_GROUP_DIM), lambda i, g: (i, 0, g)),
        compiler_params=_params("parallel", "parallel"),
        name="pool_mixer",
    )(p3, p3, w_pool_bf16, pool_scale.reshape(1, groups * POOL_GROUP_DIM), band_main, band_halo)


def _rope_tables():
    t = np.arange(GRID_ROWS * GRID_W)
    row = (t // GRID_W).astype(np.float64)[:, None]
    col = (t % GRID_W).astype(np.float64)[:, None]
    axis_dim = QK_HALF // 2
    inv = 1.0 / (ROPE_BASE ** (np.arange(0, axis_dim, 2, dtype=np.float64) / axis_dim))
    lane = np.arange(HEAD_DIM)
    within = lane % QK_HALF
    pos = np.where((within < axis_dim)[None, :], row, col)
    ang = pos * inv[within % (axis_dim // 2)][None, :]
    is_a = ((within % axis_dim) < axis_dim // 2)[None, :]
    cos = np.cos(ang)
    sin = np.sin(ang)
    return (cos.astype(np.float32),
            np.where(is_a, -sin, 0.0).astype(np.float32),
            np.where(is_a, 0.0, sin).astype(np.float32))


def _rotate_lanes(x, cos, sin_a, sin_b):
    shift = QK_HALF // 4
    return (x * cos + pltpu.roll(x, HEAD_DIM - shift, 1) * sin_a
            + pltpu.roll(x, shift, 1) * sin_b)


def _diff_lambda(lam_ref, lam_init):
    dl = lam_ref[...]
    return (jnp.exp(jnp.sum(dl[0:1] * dl[1:2], axis=-1, keepdims=True))
            - jnp.exp(jnp.sum(dl[2:3] * dl[3:4], axis=-1, keepdims=True)) + lam_init)


def _split_components(q):
    lane = lax.broadcasted_iota(jnp.int32, (1, HEAD_DIM), 1)
    return (jnp.where(lane < QK_HALF, q, 0.0).astype(BF16),
            jnp.where(lane >= QK_HALF, q, 0.0).astype(BF16))


def _diff_output(r1, r2, lam, sub, lam_init, gate):
    o = r1[:, 0:HEAD_DIM] * (1.0 / r1[:, HEAD_DIM:]) - r2[:, 0:HEAD_DIM] * (lam / r2[:, HEAD_DIM:])
    o = o * lax.rsqrt(jnp.mean(o * o, axis=-1, keepdims=True) + LN_EPS)
    o = o * sub * (1.0 - lam_init)
    return o * _silu(gate)


def _diff_latent_kernel(q_ref, k_ref, v_ref, g_ref, ck_ref, cv_ref, cos_ref, sa_ref, sb_ref,
                        lam_ref, sub_ref, o_ref, k_scr, v_scr, *, seq, tq, lam_init):
    ones_col = jnp.ones((PREP_ROWS, HEAD_DIM), BF16)

    def stage(ci, carry):
        rows = pl.ds(pl.multiple_of(ci * PREP_ROWS, PREP_ROWS), PREP_ROWS)
        k = _rotate_lanes(k_ref[0, rows, :].astype(F32), cos_ref[rows, :], sa_ref[rows, :], sb_ref[rows, :])
        k_scr[rows, :] = k.astype(BF16)
        v_scr[rows, 0:HEAD_DIM] = v_ref[0, rows, :].astype(BF16)
        v_scr[rows, HEAD_DIM:] = ones_col
        return carry

    lax.fori_loop(0, seq // PREP_ROWS, stage, 0)
    k_scr[seq:, :] = ck_ref[0, 0].astype(BF16)
    v_scr[seq:, 0:HEAD_DIM] = cv_ref[0, 0].astype(BF16)
    v_scr[seq:, HEAD_DIM:] = ones_col[0:v_scr.shape[0] - seq]
    lam = _diff_lambda(lam_ref, lam_init)

    def tiles(it, carry):
        keys_all = k_scr[...]
        values = v_scr[...]
        def rows_of(u):
            return pl.ds(pl.multiple_of((it * DIFF_TILES_PER_ITER + u) * tq, tq), tq)

        def scores(u):
            r = rows_of(u)
            pair = _split_components(
                _rotate_lanes(q_ref[0, r, :].astype(F32), cos_ref[r, :], sa_ref[r, :], sb_ref[r, :])
                * (QK_HALF ** -0.5 * LOG2E))
            return [_dot_nt(qc, keys_all) for qc in pair]

        def finish(u, s):
            r = rows_of(u)
            e = [jnp.exp2(sc - jnp.max(sc, axis=-1, keepdims=True)).astype(BF16) for sc in s]
            r1, r2 = [jnp.dot(ec, values, preferred_element_type=F32) for ec in e]
            y = _diff_output(r1, r2, lam, sub_ref[...], lam_init, g_ref[0, r, :].astype(F32))
            o_ref[0, r, :] = y.astype(o_ref.dtype)

        pending = scores(0)
        for u in range(DIFF_TILES_PER_ITER):
            upcoming = scores(u + 1) if u + 1 < DIFF_TILES_PER_ITER else None
            finish(u, pending)
            pending = upcoming
        return carry

    lax.fori_loop(0, seq // (tq * DIFF_TILES_PER_ITER), tiles, 0)


def diff_attention_latent(p3, diff_lam, diff_subln, layer_idx, ctx_k, ctx_v):
    b, seq, _ = p3.shape
    ctx_len = ctx_k.shape[2]
    tq = DIFF_TQ
    nk = seq + ctx_len
    lam_init = 0.8 - 0.6 * math.exp(-0.3 * layer_idx)
    q0, k0, v0, g0 = 2 * HEADS, 3 * HEADS, 4 * HEADS, 5 * HEADS
    const = lambda i, h: (0, 0)
    head_blk = lambda off: pl.BlockSpec((1, seq, HEAD_DIM), lambda i, h: (i, 0, off + h))
    ctx_spec = pl.BlockSpec((1, 1, ctx_len, HEAD_DIM), lambda i, h: (i, h, 0, 0))
    tab_spec = pl.BlockSpec((seq, HEAD_DIM), const, pipeline_mode=pl.Buffered(1))
    return pl.pallas_call(
        functools.partial(_diff_latent_kernel, seq=seq, tq=tq, lam_init=lam_init),
        out_shape=jax.ShapeDtypeStruct((b, seq, HEADS * HEAD_DIM), BF16),
        grid=(b, HEADS),
        in_specs=[head_blk(q0), head_blk(k0), head_blk(v0), head_blk(g0),
                  ctx_spec, ctx_spec, tab_spec, tab_spec, tab_spec,
                  pl.BlockSpec(diff_lam.shape, const),
                  pl.BlockSpec((1, HEAD_DIM), const)],
        out_specs=pl.BlockSpec((1, seq, HEAD_DIM), lambda i, h: (i, 0, h)),
        scratch_shapes=[pltpu.VMEM((nk, HEAD_DIM), BF16), pltpu.VMEM((nk, 2 * HEAD_DIM), BF16)],
        compiler_params=_params("parallel", "parallel"),
        name="diff_attention_latent",
    )(p3, p3, p3, p3, ctx_k, ctx_v, *[jnp.asarray(tab) for tab in _rope_tables()],
      diff_lam, diff_subln.reshape(1, HEAD_DIM))


def _diff_context_kernel(q_ref, k_ref, v_ref, g_ref, lam_ref, sub_ref, o_ref, *, lam_init):
    lam = _diff_lambda(lam_ref, lam_init)
    heads = [slice(h * HEAD_DIM, (h + 1) * HEAD_DIM) for h in range(HEADS)]
    seq = q_ref.shape[1]
    ones_col = jnp.ones((seq, HEAD_DIM), BF16)

    def one(bi, carry):
        def scores(group):
            qs = [_split_components(q_ref[bi, :, c].astype(F32) * (QK_HALF ** -0.5 * LOG2E)) for c in group]
            return [[_dot_nt(qc, k_ref[bi, :, c]) for qc in pair] for pair, c in zip(qs, group)]

        def finish(group, s):
            e = [[jnp.exp2(sc - jnp.max(sc, axis=-1, keepdims=True)).astype(BF16) for sc in sh] for sh in s]
            v_aug = [jnp.concatenate([v_ref[bi, :, c], ones_col], axis=-1) for c in group]
            r = [[jnp.dot(ec, va, preferred_element_type=F32) for ec in eh] for eh, va in zip(e, v_aug)]
            for (r1, r2), c in zip(r, group):
                y = _diff_output(r1, r2, lam, sub_ref[...], lam_init, g_ref[bi, :, c].astype(F32))
                o_ref[bi, :, c] = y.astype(o_ref.dtype)

        groups = [heads[i:i + CTX_HEAD_GROUP] for i in range(0, HEADS, CTX_HEAD_GROUP)]
        pending = scores(groups[0])
        for n, group in enumerate(groups):
            upcoming = scores(groups[n + 1]) if n + 1 < len(groups) else None
            finish(group, pending)
            pending = upcoming
        return carry

    lax.fori_loop(0, q_ref.shape[0], one, 0)


def diff_attention_context(p3, diff_lam, diff_subln, layer_idx):
    b, seq, _ = p3.shape
    width = HEADS * HEAD_DIM
    lam_init = 0.8 - 0.6 * math.exp(-0.3 * layer_idx)
    bb = CTX_BATCH_PER_STEP
    blk = lambda c: pl.BlockSpec((bb, seq, width), lambda i: (i, 0, c))
    return pl.pallas_call(
        functools.partial(_diff_context_kernel, lam_init=lam_init),
        out_shape=jax.ShapeDtypeStruct((b, seq, width), BF16),
        grid=(b // bb,),
        in_specs=[blk(2), blk(3), blk(4), blk(5),
                  pl.BlockSpec(diff_lam.shape, lambda i: (0, 0)),
                  pl.BlockSpec((1, HEAD_DIM), lambda i: (0, 0))],
        out_specs=pl.BlockSpec((bb, seq, width), lambda i: (i, 0, 0)),
        compiler_params=_params("parallel"),
        name="diff_attention_context",
    )(p3, p3, p3, p3, diff_lam, diff_subln.reshape(1, HEAD_DIM))


def _ctx_attn_kernel(q_ref, k_ref, v_ref, g_ref, o_ref):
    heads = [slice(h * HEAD_DIM, (h + 1) * HEAD_DIM) for h in range(HEADS)]
    seq = q_ref.shape[1]
    ones_col = jnp.ones((seq, HEAD_DIM), BF16)

    def one(bi, carry):
        q = [(q_ref[bi, :, c].astype(F32) * (HEAD_DIM ** -0.5 * LOG2E)).astype(BF16) for c in heads]
        s = [_dot_nt(q[h], k_ref[bi, :, c]) for h, c in enumerate(heads)]
        e = [jnp.exp2(sh - jnp.max(sh, axis=-1, keepdims=True)).astype(BF16) for sh in s]
        r = [jnp.dot(e[h], jnp.concatenate([v_ref[bi, :, c], ones_col], axis=-1), preferred_element_type=F32)
             for h, c in enumerate(heads)]
        for h, c in enumerate(heads):
            o = r[h][:, 0:HEAD_DIM] / r[h][:, HEAD_DIM:]
            o_ref[bi, :, c] = (o * _silu(g_ref[bi, :, c].astype(F32))).astype(o_ref.dtype)
        return carry

    lax.fori_loop(0, q_ref.shape[0], one, 0)


def context_attention(p3):
    b, seq, _ = p3.shape
    width = HEADS * HEAD_DIM
    bb = CTX_BATCH_PER_STEP
    blk = lambda c: pl.BlockSpec((bb, seq, width), lambda i: (i, 0, c))
    return pl.pallas_call(
        _ctx_attn_kernel,
        out_shape=jax.ShapeDtypeStruct((b, seq, width), BF16),
        grid=(b // bb,),
        in_specs=[blk(0), blk(1), blk(2), blk(3)],
        out_specs=pl.BlockSpec((bb, seq, width), lambda i: (i, 0, 0)),
        compiler_params=_params("parallel"),
        name="context_attention",
    )(p3, p3, p3, p3)


NA_TILE_Q = NA_Q_ROWS * GRID_W
NA_TILE_K = NA_K_ROWS * GRID_W
NA_TILES = GRID_ROWS // NA_Q_ROWS
NA_TILES_PER_ITER = 4
NA_CASES = 3
NA_KEY_ALIGN = (NA_WIN_R // 2) * GRID_W
NA_DR = 2 * NA_WIN_R - 1


def _na_key_row_start(tile):
    return int(np.clip(NA_Q_ROWS * tile - NA_WIN_R // 2, 0, GRID_ROWS - NA_K_ROWS))


def _na_bias_layout():
    dr = np.full((NA_CASES, NA_Q_ROWS, NA_K_ROWS), NA_DR, np.int32)
    for case, tile in enumerate((0, 1, NA_TILES - 1)):
        ks = _na_key_row_start(tile)
        for qr_local in range(NA_Q_ROWS):
            qr = NA_Q_ROWS * tile + qr_local
            rs = int(np.clip(qr - NA_WIN_R // 2, 0, GRID_ROWS - NA_WIN_R))
            for kr_local in range(NA_K_ROWS):
                kr = ks + kr_local
                if rs <= kr < rs + NA_WIN_R:
                    dr[case, qr_local, kr_local] = kr - qr + NA_WIN_R - 1
    pairs = sorted({(int(a), int(b)) for a, b in zip(dr[..., 0::2].ravel(), dr[..., 1::2].ravel())})
    index = {p: i for i, p in enumerate(pairs)}
    block = np.array([[[index[(int(dr[c, q, 2 * p]), int(dr[c, q, 2 * p + 1]))]
                        for p in range(NA_K_ROWS // 2)] for q in range(NA_Q_ROWS)] for c in range(NA_CASES)])
    return pairs, block


def _na_bias_blocks(rpb):
    qc = np.arange(GRID_W)[:, None]
    kc = np.arange(GRID_W)[None, :]
    cstart = np.clip(qc - NA_WIN_C // 2, 0, GRID_W - NA_WIN_C)
    col_valid = (kc >= cstart) & (kc < cstart + NA_WIN_C)
    dc = np.clip(kc - qc + NA_WIN_C - 1, 0, 2 * NA_WIN_C - 2)
    onehot = (dc.reshape(-1)[None, :] == np.arange(2 * NA_WIN_C - 1)[:, None]).astype(np.float32)
    toeplitz = jnp.einsum('hrj,jn->hrn', rpb, jnp.asarray(onehot), precision=lax.Precision.HIGHEST)
    per_dr = jnp.where(jnp.asarray(col_valid), toeplitz.reshape(rpb.shape[:2] + dc.shape), NEG_INF)
    per_dr = jnp.concatenate([per_dr, jnp.full_like(per_dr[:, :1], NEG_INF)], axis=1)
    pairs, _ = _na_bias_layout()
    left = np.array([p[0] for p in pairs])
    right = np.array([p[1] for p in pairs])
    return jnp.concatenate([per_dr[:, left], per_dr[:, right]], axis=-1)


def _na_kernel(q_ref, k_ref, v_ref, g_ref, ck_ref, cv_ref, tab_ref, o_ref,
               bias_scr, ck_scr, cv_scr, v_scr):
    b = pl.program_id(1)
    seq = v_ref.shape[1]

    @pl.when(b == 0)
    def _build_bias():
        _, block = _na_bias_layout()
        for case in range(NA_CASES):
            for qr in range(NA_Q_ROWS):
                for p in range(NA_K_ROWS // 2):
                    bias_scr[case, qr * GRID_W:(qr + 1) * GRID_W, p * 2 * GRID_W:(p + 1) * 2 * GRID_W] = (
                        tab_ref[0, int(block[case, qr, p])] * LOG2E)

    ones_col = jnp.ones((PREP_ROWS, HEAD_DIM), BF16)
    ck_scr[...] = ck_ref[0, 0].astype(BF16)
    cv_scr[:, 0:HEAD_DIM] = cv_ref[0, 0].astype(BF16)
    cv_scr[:, HEAD_DIM:] = ones_col[0:cv_scr.shape[0]]

    def stage(ci, carry):
        rows = pl.ds(pl.multiple_of(ci * PREP_ROWS, PREP_ROWS), PREP_ROWS)
        v_scr[rows, 0:HEAD_DIM] = v_ref[0, rows, :]
        v_scr[rows, HEAD_DIM:] = ones_col
        return carry

    lax.fori_loop(0, seq // PREP_ROWS, stage, 0)
    half_rows = NA_TILE_Q // 2

    def tiles(it, carry):
        k_ctx = ck_scr[...]
        v_ctx = cv_scr[...]

        def window(u):
            t = it * NA_TILES_PER_ITER + u
            case = jnp.where(t == 0, 0, jnp.where(t == NA_TILES - 1, 2, 1))
            key_row = jnp.clip(NA_Q_ROWS * t - NA_WIN_R // 2, 0, GRID_ROWS - NA_K_ROWS)
            keys = pl.ds(pl.multiple_of(key_row * GRID_W, NA_KEY_ALIGN), NA_TILE_K)
            rows = [pl.ds(pl.multiple_of(t * NA_TILE_Q + i * half_rows, half_rows), half_rows) for i in range(2)]
            return case, keys, rows

        def scores(u):
            case, keys, rows = window(u)
            k_loc = k_ref[0, keys, :]
            q = [(q_ref[0, r, :].astype(F32) * (HEAD_DIM ** -0.5 * LOG2E)).astype(BF16) for r in rows]
            s_loc = [_dot_nt(qh, k_loc) + bias_scr[case, i * half_rows:(i + 1) * half_rows, :]
                     for i, qh in enumerate(q)]
            s_ctx = [_dot_nt(qh, k_ctx) for qh in q]
            return s_loc, s_ctx

        def finish(u, s):
            _, keys, rows = window(u)
            s_loc, s_ctx = s
            m = [jnp.maximum(jnp.max(sl, axis=-1, keepdims=True), jnp.max(sc, axis=-1, keepdims=True))
                 for sl, sc in zip(s_loc, s_ctx)]
            e_loc = [jnp.exp2(sl - mh).astype(BF16) for sl, mh in zip(s_loc, m)]
            e_ctx = [jnp.exp2(sc - mh).astype(BF16) for sc, mh in zip(s_ctx, m)]
            v_loc = v_scr[keys, :]
            res = [jnp.dot(el, v_loc, preferred_element_type=F32) + jnp.dot(ec, v_ctx, preferred_element_type=F32)
                   for el, ec in zip(e_loc, e_ctx)]
            for r, rh in zip(rows, res):
                o = rh[:, 0:HEAD_DIM] / rh[:, HEAD_DIM:]
                o_ref[0, r, :] = (o * _silu(g_ref[0, r, :].astype(F32))).astype(o_ref.dtype)

        pending = scores(0)
        for u in range(NA_TILES_PER_ITER):
            upcoming = scores(u + 1) if u + 1 < NA_TILES_PER_ITER else None
            finish(u, pending)
            pending = upcoming
        return carry

    lax.fori_loop(0, NA_TILES // NA_TILES_PER_ITER, tiles, 0)


def neighborhood_attention(p3, ctx_k, ctx_v, rpb):
    b, seq, _ = p3.shape
    ctx_len = ctx_k.shape[2]
    tab = _na_bias_blocks(rpb)
    n_pairs = tab.shape[1]
    ctx_spec = pl.BlockSpec((1, 1, ctx_len, HEAD_DIM), lambda h, i: (i, h, 0, 0))
    head_blk = lambda off: pl.BlockSpec((1, seq, HEAD_DIM), lambda h, i: (i, 0, off + h))
    return pl.pallas_call(
        _na_kernel,
        out_shape=jax.ShapeDtypeStruct((b, seq, HEADS * HEAD_DIM), BF16),
        grid=(HEADS, b),
        in_specs=[head_blk(0), head_blk(HEADS), head_blk(2 * HEADS), head_blk(3 * HEADS),
                  ctx_spec, ctx_spec,
                  pl.BlockSpec((1, n_pairs, GRID_W, 2 * GRID_W), lambda h, i: (h, 0, 0, 0))],
        out_specs=pl.BlockSpec((1, seq, HEAD_DIM), lambda h, i: (i, 0, h)),
        scratch_shapes=[pltpu.VMEM((NA_CASES, NA_TILE_Q, NA_TILE_K), F32),
                        pltpu.VMEM((ctx_len, HEAD_DIM), BF16), pltpu.VMEM((ctx_len, 2 * HEAD_DIM), BF16),
                        pltpu.VMEM((seq, 2 * HEAD_DIM), BF16)],
        compiler_params=_params("arbitrary", "arbitrary"),
        name="neighborhood_attention",
    )(p3, p3, p3, p3, ctx_k, ctx_v, tab)


def _sgu_kernel(u_ref, v_ref, g_ref, ln_ref, w_ref, b_ref, o_ref):
    v = v_ref[...].astype(F32)
    mu = jnp.mean(v, axis=-1, keepdims=True)
    vc = v - mu
    var = jnp.mean(vc * vc, axis=-1, keepdims=True)
    vn = (vc * lax.rsqrt(var + LN_EPS) * ln_ref[...]).astype(BF16)
    groups = w_ref.shape[0]
    for n in range(v.shape[0] // SGU_CHUNK):
        rows = slice(n * SGU_CHUNK, (n + 1) * SGU_CHUNK)
        for g in range(groups):
            cols = slice(g * SGU_GROUP_DIM, (g + 1) * SGU_GROUP_DIM)
            s = jnp.dot(w_ref[g], vn[rows, cols], preferred_element_type=F32) + b_ref[:, g:g + 1]
            y = u_ref[rows, cols].astype(F32) * s * _silu(g_ref[rows, cols].astype(F32))
            o_ref[rows, cols] = y.astype(o_ref.dtype)


def spatial_gating(p2, sgu_ln, w_s_bf16, b_s):
    m = p2.shape[0]
    groups = w_s_bf16.shape[0]
    width = groups * SGU_GROUP_DIM
    blk = lambda c: pl.BlockSpec((SGU_TM, width), lambda i: (i, c))
    return pl.pallas_call(
        _sgu_kernel,
        out_shape=jax.ShapeDtypeStruct((m, width), BF16),
        grid=(m // SGU_TM,),
        in_specs=[blk(4), blk(5), blk(6),
                  pl.BlockSpec((1, width), lambda i: (0, 0)),
                  pl.BlockSpec((groups, SGU_CHUNK, SGU_CHUNK), lambda i: (0, 0, 0)),
                  pl.BlockSpec((SGU_CHUNK, groups), lambda i: (0, 0))],
        out_specs=pl.BlockSpec((SGU_TM, width), lambda i: (i, 0)),
        compiler_params=_params("parallel"),
        name="spatial_gating",
    )(p2, p2, p2, sgu_ln.reshape(1, width), w_s_bf16, jnp.transpose(b_s))


def kernel(x_prompt, x_sample, cache_k_l0, cache_v_l0, cache_k_l1, cache_v_l1, cache_k_l2, cache_v_l2, cache_k_l3, cache_v_l3, c, c_ctx, w_mod_0, b_mod_0, w_in_0, w_out_0, ln_g_0, ln_b_0, pool_w_0, pool_scale_0, diff_lam_0, diff_subln_0, w_mod_1, b_mod_1, w_in_1, w_out_1, ln_g_1, ln_b_1, rpb_1, sgu_ln_1, sgu_w_1, sgu_b_1, w_mod_2, b_mod_2, w_in_2, w_out_2, ln_g_2, ln_b_2, pool_w_2, pool_scale_2, diff_lam_2, diff_subln_2, w_mod_3, b_mod_3, w_in_3, w_out_3, ln_g_3, ln_b_3, rpb_3, sgu_ln_3, sgu_w_3, sgu_b_3):
    cache_k = [cache_k_l0, cache_k_l1, cache_k_l2, cache_k_l3]
    cache_v = [cache_v_l0, cache_v_l1, cache_v_l2, cache_v_l3]
    w_mod = [w_mod_0, w_mod_1, w_mod_2, w_mod_3]
    b_mod = [b_mod_0, b_mod_1, b_mod_2, b_mod_3]
    w_in = [w_in_0, w_in_1, w_in_2, w_in_3]
    w_out = [w_out_0, w_out_1, w_out_2, w_out_3]
    ln_g = [ln_g_0, ln_g_1, ln_g_2, ln_g_3]
    ln_b = [ln_b_0, ln_b_1, ln_b_2, ln_b_3]
    even_p = {0: (pool_w_0, pool_scale_0, diff_lam_0, diff_subln_0),
              2: (pool_w_2, pool_scale_2, diff_lam_2, diff_subln_2)}
    odd_p = {1: (rpb_1, sgu_ln_1, sgu_w_1, sgu_b_1),
             3: (rpb_3, sgu_ln_3, sgu_w_3, sgu_b_3)}

    bp, lp, _ = x_prompt.shape
    bs, ls, _ = x_sample.shape
    xp = x_prompt.reshape(bp * lp, D_MODEL)
    xs = x_sample.reshape(bs * ls, D_MODEL)
    cond = jnp.concatenate([c, c_ctx[None, :], jnp.zeros((COND_ROWS - bs - 1, D_MODEL), F32)], axis=0)

    new_k, new_v = [], []
    for l in range(DEPTH):
        mod3 = ada_params(cond, w_mod[l], b_mod[l]).reshape(COND_ROWS, 1, 3 * D_MODEL)
        w_in_l = w_in[l].astype(BF16)
        n_in = w_in_l.shape[1]
        kv_block = (3 if l % 2 == 0 else 1) * HEADS * HEAD_DIM // PROJ_TN
        pp, nk, nv = in_projection(xp, mod3, w_in_l, None, kv_block=kv_block, cache_seq=lp)
        ps = in_projection(xs, mod3, w_in_l, ls)
        pp3 = pp.reshape(bp, lp, n_in)
        ps3 = ps.reshape(bs, ls, n_in)
        if l % 2 == 0:
            pool_w, pool_scale, diff_lam, diff_subln = even_p[l]
            pool_w = pool_w.astype(BF16)
            ya_p = pool_mixer(pp3, pool_w, pool_scale)
            ya_s = pool_mixer(ps3, pool_w, pool_scale)
            yb_p = diff_attention_context(pp3, diff_lam, diff_subln, l)
            yb_s = diff_attention_latent(ps3, diff_lam, diff_subln, l, cache_k[l], cache_v[l])
        else:
            rpb, sgu_ln, sgu_w, sgu_b = odd_p[l]
            sgu_w = sgu_w.astype(BF16)
            ya_p = context_attention(pp3)
            ya_s = neighborhood_attention(ps3, cache_k[l], cache_v[l], rpb)
            yb_p = spatial_gating(pp, sgu_ln, sgu_w, sgu_b)
            yb_s = spatial_gating(ps, sgu_ln, sgu_w, sgu_b)
        half = ya_p.shape[-1]
        w_out_l = w_out[l].astype(BF16)
        xp = out_projection(ya_p.reshape(bp * lp, half), yb_p.reshape(bp * lp, half), xp, mod3,
                            w_out_l, ln_g[l], ln_b[l], None)
        xs = out_projection(ya_s.reshape(bs * ls, half), yb_s.reshape(bs * ls, half), xs, mod3,
                            w_out_l, ln_g[l], ln_b[l], ls)
        new_k.append(nk)
        new_v.append(nv)

    return (xp.reshape(bp, lp, D_MODEL), xs.reshape(bs, ls, D_MODEL),
            new_k[0], new_v[0], new_k[1], new_v[1], new_k[2], new_v[2], new_k[3], new_v[3])
```

```python
import functools
import math

import numpy as np
import jax
import jax.numpy as jnp
from jax import lax
from jax.experimental import pallas as pl
from jax.experimental.pallas import tpu as pltpu

F32 = jnp.float32
BF16 = jnp.bfloat16

D_MODEL = 2048
DEPTH = 4
GRID_W = 64
GRID_ROWS = 64
HEADS = 8
HEAD_DIM = 128
QK_HALF = 64
POOL_WINDOWS = (2, 4, 8, 16)
POOL_GROUP_DIM = 256
SGU_CHUNK = 128
SGU_GROUP_DIM = 256
NA_WIN_R = 8
NA_WIN_C = 16
NA_Q_ROWS = 8
NA_K_ROWS = 16
ROPE_BASE = 10000.0
LN_EPS = 1e-5
NEG_INF = -1e30
DEEPNORM_ALPHA = (2 * DEPTH) ** 0.25
LOG2E = math.log2(math.e)

VMEM_LIMIT_BYTES = 56 * 1024 * 1024
COND_ROWS = 8
CTX_ROW = 4

PROJ_TM = 1024
PROJ_TN = 1024
PROJ_TN_WIDE = (2048, 1792)
OUT_TM = 1024
OUT_ROW_GROUP = 256
ADA_TN = 1536
POOL_ROWS = 256
POOL_HALO = 16
POOL_CHUNKS_PER_ITER = 8
SGU_TM = 1024
DIFF_TQ = 256
DIFF_TILES_PER_ITER = 8
PREP_ROWS = 512
CTX_BATCH_PER_STEP = 4
CTX_HEAD_GROUP = 2


def _params(*sem):
    return pltpu.CompilerParams(dimension_semantics=sem, vmem_limit_bytes=VMEM_LIMIT_BYTES)


def _silu(x):
    half_x = 0.5 * x
    return half_x * jnp.tanh(half_x) + half_x


def _dot_nt(a, b):
    return lax.dot_general(a, b, (((1,), (1,)), ((), ())), preferred_element_type=F32)


def _ada_kernel(c_ref, w_ref, b_ref, o_ref):
    a = _silu(c_ref[...]).astype(BF16)
    o_ref[...] = jnp.dot(a, w_ref[...].astype(BF16), preferred_element_type=F32) + b_ref[...]


def ada_params(cond, w_mod, b_mod):
    n = w_mod.shape[1]
    return pl.pallas_call(
        _ada_kernel,
        out_shape=jax.ShapeDtypeStruct((COND_ROWS, n), F32),
        grid=(n // ADA_TN,),
        in_specs=[pl.BlockSpec((COND_ROWS, D_MODEL), lambda j: (0, 0)),
                  pl.BlockSpec((D_MODEL, ADA_TN), lambda j: (0, j)),
                  pl.BlockSpec((1, ADA_TN), lambda j: (0, j))],
        out_specs=pl.BlockSpec((COND_ROWS, ADA_TN), lambda j: (0, j)),
        compiler_params=_params("parallel"),
        name="ada_params",
    )(cond, w_mod, b_mod.reshape(1, n))


def _inproj_kernel(x_ref, sh_ref, sc_ref, w_ref, o_ref, *rest, kv_block):
    h_scr = rest[-1]
    j = pl.program_id(1)

    @pl.when(j == 0)
    def _():
        h = x_ref[...] * (1.0 + sc_ref[0]) + sh_ref[0]
        h_scr[...] = h.astype(BF16)

    acc = jnp.dot(h_scr[...], w_ref[...], preferred_element_type=F32)
    o_ref[...] = acc.astype(o_ref.dtype)
    if kv_block is not None:
        for which, cache_ref in enumerate(rest[:2]):
            @pl.when(j == kv_block + which)
            def _(cache_ref=cache_ref):
                n_batch, n_heads, seq, head_dim = cache_ref.shape
                for bi in range(n_batch):
                    for hd in range(n_heads):
                        cache_ref[bi, hd] = acc[bi * seq:(bi + 1) * seq, hd * head_dim:(hd + 1) * head_dim]


def in_projection(x2d, mod3, w_bf16, rows_per_cond, kv_block=None, cache_seq=None):
    m = x2d.shape[0]
    n = w_bf16.shape[1]
    if rows_per_cond is None:
        cond_row = lambda i: CTX_ROW
    else:
        blocks_per_cond = rows_per_cond // PROJ_TM
        cond_row = lambda i: i // blocks_per_cond
    tn = PROJ_TN
    if kv_block is None:
        tn = next((c for c in PROJ_TN_WIDE if n % c == 0), PROJ_TN)
    out_shape = [jax.ShapeDtypeStruct((m, n), BF16)]
    out_specs = [pl.BlockSpec((PROJ_TM, tn), lambda i, j: (i, j))]
    if kv_block is not None:
        assert PROJ_TN == HEADS * HEAD_DIM and PROJ_TM % cache_seq == 0
        per_block = PROJ_TM // cache_seq
        cache_spec = pl.BlockSpec((per_block, HEADS, cache_seq, HEAD_DIM), lambda i, j: (i, 0, 0, 0))
        out_shape += [jax.ShapeDtypeStruct((m // cache_seq, HEADS, cache_seq, HEAD_DIM), F32)] * 2
        out_specs += [cache_spec, cache_spec]
    outs = pl.pallas_call(
        functools.partial(_inproj_kernel, kv_block=kv_block),
        out_shape=out_shape,
        grid=(m // PROJ_TM, n // tn),
        in_specs=[pl.BlockSpec((PROJ_TM, D_MODEL), lambda i, j: (i, 0)),
                  pl.BlockSpec((1, 1, D_MODEL), lambda i, j: (cond_row(i), 0, 0)),
                  pl.BlockSpec((1, 1, D_MODEL), lambda i, j: (cond_row(i), 0, 1)),
                  pl.BlockSpec((D_MODEL, tn), lambda i, j: (0, j))],
        out_specs=out_specs,
        scratch_shapes=[pltpu.VMEM((PROJ_TM, D_MODEL), BF16)],
        compiler_params=_params("parallel", "arbitrary"),
        name="in_projection",
    )(x2d, mod3, mod3, w_bf16)
    return outs[0] if kv_block is None else outs


def _outproj_kernel(ya_ref, yb_ref, x_ref, g_ref, w_ref, lng_ref, lnb_ref, o_ref):
    half = ya_ref.shape[1]
    for r in range(OUT_TM // OUT_ROW_GROUP):
        rows = slice(r * OUT_ROW_GROUP, (r + 1) * OUT_ROW_GROUP)
        acc = jnp.dot(ya_ref[rows, :], w_ref[0:half, :], preferred_element_type=F32)
        acc = acc + jnp.dot(yb_ref[rows, :], w_ref[half:, :], preferred_element_type=F32)
        z = DEEPNORM_ALPHA * x_ref[rows, :] + g_ref[0] * acc
        mu = jnp.mean(z, axis=-1, keepdims=True)
        zc = z - mu
        var = jnp.mean(zc * zc, axis=-1, keepdims=True)
        o_ref[rows, :] = zc * lax.rsqrt(var + LN_EPS) * lng_ref[...] + lnb_ref[...]


def out_projection(ya, yb, x2d, mod3, w_bf16, ln_g, ln_b, rows_per_cond):
    m = x2d.shape[0]
    half = ya.shape[1]
    if rows_per_cond is None:
        cond_row = lambda i: CTX_ROW
    else:
        blocks_per_cond = rows_per_cond // OUT_TM
        cond_row = lambda i: i // blocks_per_cond
    return pl.pallas_call(
        _outproj_kernel,
        out_shape=jax.ShapeDtypeStruct((m, D_MODEL), F32),
        grid=(m // OUT_TM,),
        in_specs=[pl.BlockSpec((OUT_TM, half), lambda i: (i, 0)),
                  pl.BlockSpec((OUT_TM, half), lambda i: (i, 0)),
                  pl.BlockSpec((OUT_TM, D_MODEL), lambda i: (i, 0)),
                  pl.BlockSpec((1, 1, D_MODEL), lambda i: (cond_row(i), 0, 2)),
                  pl.BlockSpec((2 * half, D_MODEL), lambda i: (0, 0), pipeline_mode=pl.Buffered(1)),
                  pl.BlockSpec((1, D_MODEL), lambda i: (0, 0)),
                  pl.BlockSpec((1, D_MODEL), lambda i: (0, 0))],
        out_specs=pl.BlockSpec((OUT_TM, D_MODEL), lambda i: (i, 0)),
        compiler_params=_params("parallel"),
        name="out_projection",
    )(ya, yb, x2d, mod3, w_bf16, ln_g.reshape(1, D_MODEL), ln_b.reshape(1, D_MODEL))


def _pool_bands():
    assert max(POOL_WINDOWS) // 2 <= POOL_HALO
    i = np.arange(POOL_ROWS)[:, None]
    own = np.arange(POOL_ROWS)[None, :]
    mains = [(own >= i - w // 2) & (own < i + w // 2) for w in POOL_WINDOWS]
    top_i = np.arange(POOL_HALO)[:, None]
    bot_i = POOL_ROWS - POOL_HALO + top_i
    before = np.arange(POOL_HALO)[None, :] - POOL_HALO
    after = POOL_ROWS + np.arange(POOL_HALO)[None, :]
    halos = [np.stack([before >= top_i - w // 2, after < bot_i + w // 2]) for w in POOL_WINDOWS]
    return np.stack(mains).astype(np.float32), np.stack(halos).astype(np.float32)


def _pool_kernel(a_ref, g_ref, w_ref, s_ref, bm_ref, bh_ref, o_ref, *, seq):
    n_chunks = seq // POOL_ROWS
    group = pl.program_id(1)
    half = jnp.int32(0)
    for gi, window in enumerate(POOL_WINDOWS):
        half = jnp.where(group == gi, window // 2, half)

    def chunk(bi, ci):
        r0 = pl.multiple_of(ci * POOL_ROWS, POOL_ROWS)
        main = a_ref[bi, pl.ds(r0, POOL_ROWS), :]
        win = jnp.dot(bm_ref[0], main.astype(BF16), preferred_element_type=F32)
        if n_chunks > 1:
            p0 = pl.multiple_of(jnp.maximum(r0 - POOL_HALO, 0), POOL_HALO)
            n0 = pl.multiple_of(jnp.minimum(r0 + POOL_ROWS, seq - POOL_HALO), POOL_HALO)
            prev = jnp.where(ci > 0, a_ref[bi, pl.ds(p0, POOL_HALO), :].astype(F32), 0.0).astype(BF16)
            nxt = jnp.where(ci < n_chunks - 1, a_ref[bi, pl.ds(n0, POOL_HALO), :].astype(F32), 0.0).astype(BF16)
            top = win[0:POOL_HALO] + jnp.dot(bh_ref[0, 0], prev, preferred_element_type=F32)
            bot = win[POOL_ROWS - POOL_HALO:] + jnp.dot(bh_ref[0, 1], nxt, preferred_element_type=F32)
            win = jnp.concatenate([top, win[POOL_HALO:POOL_ROWS - POOL_HALO], bot], axis=0)
        t = r0 + lax.broadcasted_iota(jnp.int32, (POOL_ROWS, 1), 0)
        cnt = (jnp.minimum(t + half, seq) - jnp.maximum(t - half, 0)).astype(F32)
        pooled = win / cnt - main.astype(F32)
        mixed = jnp.dot(pooled.astype(BF16), w_ref[0], preferred_element_type=F32)
        gate = g_ref[bi, pl.ds(r0, POOL_ROWS), :].astype(F32)
        y = mixed * s_ref[...] * _silu(gate)
        o_ref[bi, pl.ds(r0, POOL_ROWS), :] = y.astype(o_ref.dtype)

    def group_of_chunks(it, carry):
        for u in range(POOL_CHUNKS_PER_ITER):
            idx = it * POOL_CHUNKS_PER_ITER + u
            chunk(idx // n_chunks, idx % n_chunks)
        return carry

    lax.fori_loop(0, a_ref.shape[0] * n_chunks // POOL_CHUNKS_PER_ITER, group_of_chunks, 0)


def pool_mixer(p3, w_pool_bf16, pool_scale):
    b, seq, _ = p3.shape
    groups = len(POOL_WINDOWS)
    band_main, band_halo = (jnp.asarray(m, BF16) for m in _pool_bands())
    bb = max(1, POOL_CHUNKS_PER_ITER * POOL_ROWS // seq)
    return pl.pallas_call(
        functools.partial(_pool_kernel, seq=seq),
        out_shape=jax.ShapeDtypeStruct((b, seq, groups * POOL_GROUP_DIM), BF16),
        grid=(b // bb, groups),
        in_specs=[pl.BlockSpec((bb, seq, POOL_GROUP_DIM), lambda i, g: (i, 0, g)),
                  pl.BlockSpec((bb, seq, POOL_GROUP_DIM), lambda i, g: (i, 0, groups + g)),
                  pl.BlockSpec((1, POOL_GROUP_DIM, POOL_GROUP_DIM), lambda i, g: (g, 0, 0)),
                  pl.BlockSpec((1, POOL_GROUP_DIM), lambda i, g: (0, g)),
                  pl.BlockSpec((1, POOL_ROWS, POOL_ROWS), lambda i, g: (g, 0, 0)),
                  pl.BlockSpec((1, 2, POOL_HALO, POOL_HALO), lambda i, g: (g, 0, 0, 0))],
        out_specs=pl.BlockSpec((bb, seq, POOL_GROUP_DIM), lambda i, g: (i, 0, g)),
        compiler_params=_params("parallel", "parallel"),
        name="pool_mixer",
    )(p3, p3, w_pool_bf16, pool_scale.reshape(1, groups * POOL_GROUP_DIM), band_main, band_halo)


def _rope_tables():
    t = np.arange(GRID_ROWS * GRID_W)
    row = (t // GRID_W).astype(np.float64)[:, None]
    col = (t % GRID_W).astype(np.float64)[:, None]
    axis_dim = QK_HALF // 2
    inv = 1.0 / (ROPE_BASE ** (np.arange(0, axis_dim, 2, dtype=np.float64) / axis_dim))
    lane = np.arange(HEAD_DIM)
    within = lane % QK_HALF
    pos = np.where((within < axis_dim)[None, :], row, col)
    ang = pos * inv[within % (axis_dim // 2)][None, :]
    is_a = ((within % axis_dim) < axis_dim // 2)[None, :]
    cos = np.cos(ang)
    sin = np.sin(ang)
    return (cos.astype(np.float32),
            np.where(is_a, -sin, 0.0).astype(np.float32),
            np.where(is_a, 0.0, sin).astype(np.float32))


def _rotate_lanes(x, cos, sin_a, sin_b):
    shift = QK_HALF // 4
    return (x * cos + pltpu.roll(x, HEAD_DIM - shift, 1) * sin_a
            + pltpu.roll(x, shift, 1) * sin_b)


def _diff_lambda(lam_ref, lam_init):
    dl = lam_ref[...]
    return (jnp.exp(jnp.sum(dl[0:1] * dl[1:2], axis=-1, keepdims=True))
            - jnp.exp(jnp.sum(dl[2:3] * dl[3:4], axis=-1, keepdims=True)) + lam_init)


def _split_components(q):
    lane = lax.broadcasted_iota(jnp.int32, (1, HEAD_DIM), 1)
    return (jnp.where(lane < QK_HALF, q, 0.0).astype(BF16),
            jnp.where(lane >= QK_HALF, q, 0.0).astype(BF16))


def _diff_output(r1, r2, lam, sub, lam_init, gate):
    o = r1[:, 0:HEAD_DIM] * (1.0 / r1[:, HEAD_DIM:]) - r2[:, 0:HEAD_DIM] * (lam / r2[:, HEAD_DIM:])
    o = o * lax.rsqrt(jnp.mean(o * o, axis=-1, keepdims=True) + LN_EPS)
    o = o * sub * (1.0 - lam_init)
    return o * _silu(gate)


def _diff_latent_kernel(q_ref, k_ref, v_ref, g_ref, ck_ref, cv_ref, cos_ref, sa_ref, sb_ref,
                        lam_ref, sub_ref, o_ref, k_scr, v_scr, *, seq, tq, lam_init):
    ones_col = jnp.ones((PREP_ROWS, HEAD_DIM), BF16)

    def stage(ci, carry):
        rows = pl.ds(pl.multiple_of(ci * PREP_ROWS, PREP_ROWS), PREP_ROWS)
        k = _rotate_lanes(k_ref[0, rows, :].astype(F32), cos_ref[rows, :], sa_ref[rows, :], sb_ref[rows, :])
        k_scr[rows, :] = k.astype(BF16)
        v_scr[rows, 0:HEAD_DIM] = v_ref[0, rows, :].astype(BF16)
        v_scr[rows, HEAD_DIM:] = ones_col
        return carry

    lax.fori_loop(0, seq // PREP_ROWS, stage, 0)
    k_scr[seq:, :] = ck_ref[0, 0].astype(BF16)
    v_scr[seq:, 0:HEAD_DIM] = cv_ref[0, 0].astype(BF16)
    v_scr[seq:, HEAD_DIM:] = ones_col[0:v_scr.shape[0] - seq]
    lam = _diff_lambda(lam_ref, lam_init)

    def tiles(it, carry):
        keys_all = k_scr[...]
        values = v_scr[...]
        def rows_of(u):
            return pl.ds(pl.multiple_of((it * DIFF_TILES_PER_ITER + u) * tq, tq), tq)

        def scores(u):
            r = rows_of(u)
            pair = _split_components(
                _rotate_lanes(q_ref[0, r, :].astype(F32), cos_ref[r, :], sa_ref[r, :], sb_ref[r, :])
                * (QK_HALF ** -0.5 * LOG2E))
            return [_dot_nt(qc, keys_all) for qc in pair]

        def finish(u, s):
            r = rows_of(u)
            e = [jnp.exp2(sc - jnp.max(sc, axis=-1, keepdims=True)).astype(BF16) for sc in s]
            r1, r2 = [jnp.dot(ec, values, preferred_element_type=F32) for ec in e]
            y = _diff_output(r1, r2, lam, sub_ref[...], lam_init, g_ref[0, r, :].astype(F32))
            o_ref[0, r, :] = y.astype(o_ref.dtype)

        pending = scores(0)
        for u in range(DIFF_TILES_PER_ITER):
            upcoming = scores(u + 1) if u + 1 < DIFF_TILES_PER_ITER else None
            finish(u, pending)
            pending = upcoming
        return carry

    lax.fori_loop(0, seq // (tq * DIFF_TILES_PER_ITER), tiles, 0)


def diff_attention_latent(p3, diff_lam, diff_subln, layer_idx, ctx_k, ctx_v):
    b, seq, _ = p3.shape
    ctx_len = ctx_k.shape[2]
    tq = DIFF_TQ
    nk = seq + ctx_len
    lam_init = 0.8 - 0.6 * math.exp(-0.3 * layer_idx)
    q0, k0, v0, g0 = 2 * HEADS, 3 * HEADS, 4 * HEADS, 5 * HEADS
    const = lambda i, h: (0, 0)
    head_blk = lambda off: pl.BlockSpec((1, seq, HEAD_DIM), lambda i, h: (i, 0, off + h))
    ctx_spec = pl.BlockSpec((1, 1, ctx_len, HEAD_DIM), lambda i, h: (i, h, 0, 0))
    tab_spec = pl.BlockSpec((seq, HEAD_DIM), const, pipeline_mode=pl.Buffered(1))
    return pl.pallas_call(
        functools.partial(_diff_latent_kernel, seq=seq, tq=tq, lam_init=lam_init),
        out_shape=jax.ShapeDtypeStruct((b, seq, HEADS * HEAD_DIM), BF16),
        grid=(b, HEADS),
        in_specs=[head_blk(q0), head_blk(k0), head_blk(v0), head_blk(g0),
                  ctx_spec, ctx_spec, tab_spec, tab_spec, tab_spec,
                  pl.BlockSpec(diff_lam.shape, const),
                  pl.BlockSpec((1, HEAD_DIM), const)],
        out_specs=pl.BlockSpec((1, seq, HEAD_DIM), lambda i, h: (i, 0, h)),
        scratch_shapes=[pltpu.VMEM((nk, HEAD_DIM), BF16), pltpu.VMEM((nk, 2 * HEAD_DIM), BF16)],
        compiler_params=_params("parallel", "parallel"),
        name="diff_attention_latent",
    )(p3, p3, p3, p3, ctx_k, ctx_v, *[jnp.asarray(tab) for tab in _rope_tables()],
      diff_lam, diff_subln.reshape(1, HEAD_DIM))


def _diff_context_kernel(q_ref, k_ref, v_ref, g_ref, lam_ref, sub_ref, o_ref, *, lam_init):
    lam = _diff_lambda(lam_ref, lam_init)
    heads = [slice(h * HEAD_DIM, (h + 1) * HEAD_DIM) for h in range(HEADS)]
    seq = q_ref.shape[1]
    ones_col = jnp.ones((seq, HEAD_DIM), BF16)

    def one(bi, carry):
        def scores(group):
            qs = [_split_components(q_ref[bi, :, c].astype(F32) * (QK_HALF ** -0.5 * LOG2E)) for c in group]
            return [[_dot_nt(qc, k_ref[bi, :, c]) for qc in pair] for pair, c in zip(qs, group)]

        def finish(group, s):
            e = [[jnp.exp2(sc - jnp.max(sc, axis=-1, keepdims=True)).astype(BF16) for sc in sh] for sh in s]
            v_aug = [jnp.concatenate([v_ref[bi, :, c], ones_col], axis=-1) for c in group]
            r = [[jnp.dot(ec, va, preferred_element_type=F32) for ec in eh] for eh, va in zip(e, v_aug)]
            for (r1, r2), c in zip(r, group):
                y = _diff_output(r1, r2, lam, sub_ref[...], lam_init, g_ref[bi, :, c].astype(F32))
                o_ref[bi, :, c] = y.astype(o_ref.dtype)

        groups = [heads[i:i + CTX_HEAD_GROUP] for i in range(0, HEADS, CTX_HEAD_GROUP)]
        pending = scores(groups[0])
        for n, group in enumerate(groups):
            upcoming = scores(groups[n + 1]) if n + 1 < len(groups) else None
            finish(group, pending)
            pending = upcoming
        return carry

    lax.fori_loop(0, q_ref.shape[0], one, 0)


def diff_attention_context(p3, diff_lam, diff_subln, layer_idx):
    b, seq, _ = p3.shape
    width = HEADS * HEAD_DIM
    lam_init = 0.8 - 0.6 * math.exp(-0.3 * layer_idx)
    bb = CTX_BATCH_PER_STEP
    blk = lambda c: pl.BlockSpec((bb, seq, width), lambda i: (i, 0, c))
    return pl.pallas_call(
        functools.partial(_diff_context_kernel, lam_init=lam_init),
        out_shape=jax.ShapeDtypeStruct((b, seq, width), BF16),
        grid=(b // bb,),
        in_specs=[blk(2), blk(3), blk(4), blk(5),
                  pl.BlockSpec(diff_lam.shape, lambda i: (0, 0)),
                  pl.BlockSpec((1, HEAD_DIM), lambda i: (0, 0))],
        out_specs=pl.BlockSpec((bb, seq, width), lambda i: (i, 0, 0)),
        compiler_params=_params("parallel"),
        name="diff_attention_context",
    )(p3, p3, p3, p3, diff_lam, diff_subln.reshape(1, HEAD_DIM))


def _ctx_attn_kernel(q_ref, k_ref, v_ref, g_ref, o_ref):
    heads = [slice(h * HEAD_DIM, (h + 1) * HEAD_DIM) for h in range(HEADS)]
    seq = q_ref.shape[1]
    ones_col = jnp.ones((seq, HEAD_DIM), BF16)

    def one(bi, carry):
        q = [(q_ref[bi, :, c].astype(F32) * (HEAD_DIM ** -0.5 * LOG2E)).astype(BF16) for c in heads]
        s = [_dot_nt(q[h], k_ref[bi, :, c]) for h, c in enumerate(heads)]
        e = [jnp.exp2(sh - jnp.max(sh, axis=-1, keepdims=True)).astype(BF16) for sh in s]
        r = [jnp.dot(e[h], jnp.concatenate([v_ref[bi, :, c], ones_col], axis=-1), preferred_element_type=F32)
             for h, c in enumerate(heads)]
        for h, c in enumerate(heads):
            o = r[h][:, 0:HEAD_DIM] / r[h][:, HEAD_DIM:]
            o_ref[bi, :, c] = (o * _silu(g_ref[bi, :, c].astype(F32))).astype(o_ref.dtype)
        return carry

    lax.fori_loop(0, q_ref.shape[0], one, 0)


def context_attention(p3):
    b, seq, _ = p3.shape
    width = HEADS * HEAD_DIM
    bb = CTX_BATCH_PER_STEP
    blk = lambda c: pl.BlockSpec((bb, seq, width), lambda i: (i, 0, c))
    return pl.pallas_call(
        _ctx_attn_kernel,
        out_shape=jax.ShapeDtypeStruct((b, seq, width), BF16),
        grid=(b // bb,),
        in_specs=[blk(0), blk(1), blk(2), blk(3)],
        out_specs=pl.BlockSpec((bb, seq, width), lambda i: (i, 0, 0)),
        compiler_params=_params("parallel"),
        name="context_attention",
    )(p3, p3, p3, p3)


NA_TILE_Q = NA_Q_ROWS * GRID_W
NA_TILE_K = NA_K_ROWS * GRID_W
NA_TILES = GRID_ROWS // NA_Q_ROWS
NA_TILES_PER_ITER = 8
NA_CASES = 3
NA_KEY_ALIGN = (NA_WIN_R // 2) * GRID_W
NA_DR = 2 * NA_WIN_R - 1


def _na_key_row_start(tile):
    return int(np.clip(NA_Q_ROWS * tile - NA_WIN_R // 2, 0, GRID_ROWS - NA_K_ROWS))


def _na_bias_layout():
    dr = np.full((NA_CASES, NA_Q_ROWS, NA_K_ROWS), NA_DR, np.int32)
    for case, tile in enumerate((0, 1, NA_TILES - 1)):
        ks = _na_key_row_start(tile)
        for qr_local in range(NA_Q_ROWS):
            qr = NA_Q_ROWS * tile + qr_local
            rs = int(np.clip(qr - NA_WIN_R // 2, 0, GRID_ROWS - NA_WIN_R))
            for kr_local in range(NA_K_ROWS):
                kr = ks + kr_local
                if rs <= kr < rs + NA_WIN_R:
                    dr[case, qr_local, kr_local] = kr - qr + NA_WIN_R - 1
    pairs = sorted({(int(a), int(b)) for a, b in zip(dr[..., 0::2].ravel(), dr[..., 1::2].ravel())})
    index = {p: i for i, p in enumerate(pairs)}
    block = np.array([[[index[(int(dr[c, q, 2 * p]), int(dr[c, q, 2 * p + 1]))]
                        for p in range(NA_K_ROWS // 2)] for q in range(NA_Q_ROWS)] for c in range(NA_CASES)])
    return pairs, block


def _na_bias_blocks(rpb):
    qc = np.arange(GRID_W)[:, None]
    kc = np.arange(GRID_W)[None, :]
    cstart = np.clip(qc - NA_WIN_C // 2, 0, GRID_W - NA_WIN_C)
    col_valid = (kc >= cstart) & (kc < cstart + NA_WIN_C)
    dc = np.clip(kc - qc + NA_WIN_C - 1, 0, 2 * NA_WIN_C - 2)
    onehot = (dc.reshape(-1)[None, :] == np.arange(2 * NA_WIN_C - 1)[:, None]).astype(np.float32)
    toeplitz = jnp.einsum('hrj,jn->hrn', rpb, jnp.asarray(onehot), precision=lax.Precision.HIGHEST)
    per_dr = jnp.where(jnp.asarray(col_valid), toeplitz.reshape(rpb.shape[:2] + dc.shape), NEG_INF)
    per_dr = jnp.concatenate([per_dr, jnp.full_like(per_dr[:, :1], NEG_INF)], axis=1)
    pairs, _ = _na_bias_layout()
    left = np.array([p[0] for p in pairs])
    right = np.array([p[1] for p in pairs])
    return jnp.concatenate([per_dr[:, left], per_dr[:, right]], axis=-1)


def _na_kernel(q_ref, k_ref, v_ref, g_ref, ck_ref, cv_ref, tab_ref, o_ref,
               bias_scr, ck_scr, cv_scr, v_scr):
    b = pl.program_id(1)
    seq = v_ref.shape[1]

    @pl.when(b == 0)
    def _build_bias():
        _, block = _na_bias_layout()
        for case in range(NA_CASES):
            for qr in range(NA_Q_ROWS):
                for p in range(NA_K_ROWS // 2):
                    bias_scr[case, qr * GRID_W:(qr + 1) * GRID_W, p * 2 * GRID_W:(p + 1) * 2 * GRID_W] = (
                        tab_ref[0, int(block[case, qr, p])] * LOG2E)

    ones_col = jnp.ones((PREP_ROWS, HEAD_DIM), BF16)
    ck_scr[...] = ck_ref[0, 0].astype(BF16)
    cv_scr[:, 0:HEAD_DIM] = cv_ref[0, 0].astype(BF16)
    cv_scr[:, HEAD_DIM:] = ones_col[0:cv_scr.shape[0]]

    def stage(ci, carry):
        rows = pl.ds(pl.multiple_of(ci * PREP_ROWS, PREP_ROWS), PREP_ROWS)
        v_scr[rows, 0:HEAD_DIM] = v_ref[0, rows, :]
        v_scr[rows, HEAD_DIM:] = ones_col
        return carry

    lax.fori_loop(0, seq // PREP_ROWS, stage, 0)
    half_rows = NA_TILE_Q // 2

    def tiles(it, carry):
        k_ctx = ck_scr[...]
        v_ctx = cv_scr[...]

        def window(u):
            t = it * NA_TILES_PER_ITER + u
            case = jnp.where(t == 0, 0, jnp.where(t == NA_TILES - 1, 2, 1))
            key_row = jnp.clip(NA_Q_ROWS * t - NA_WIN_R // 2, 0, GRID_ROWS - NA_K_ROWS)
            keys = pl.ds(pl.multiple_of(key_row * GRID_W, NA_KEY_ALIGN), NA_TILE_K)
            rows = [pl.ds(pl.multiple_of(t * NA_TILE_Q + i * half_rows, half_rows), half_rows) for i in range(2)]
            return case, keys, rows

        def scores(u):
            case, keys, rows = window(u)
            k_loc = k_ref[0, keys, :]
            q = [(q_ref[0, r, :].astype(F32) * (HEAD_DIM ** -0.5 * LOG2E)).astype(BF16) for r in rows]
            s_loc = [_dot_nt(qh, k_loc) + bias_scr[case, i * half_rows:(i + 1) * half_rows, :]
                     for i, qh in enumerate(q)]
            s_ctx = [_dot_nt(qh, k_ctx) for qh in q]
            return s_loc, s_ctx

        def finish(u, s):
            _, keys, rows = window(u)
            s_loc, s_ctx = s
            m = [jnp.maximum(jnp.max(sl, axis=-1, keepdims=True), jnp.max(sc, axis=-1, keepdims=True))
                 for sl, sc in zip(s_loc, s_ctx)]
            e_loc = [jnp.exp2(sl - mh).astype(BF16) for sl, mh in zip(s_loc, m)]
            e_ctx = [jnp.exp2(sc - mh).astype(BF16) for sc, mh in zip(s_ctx, m)]
            v_loc = v_scr[keys, :]
            res = [jnp.dot(el, v_loc, preferred_element_type=F32) + jnp.dot(ec, v_ctx, preferred_element_type=F32)
                   for el, ec in zip(e_loc, e_ctx)]
            for r, rh in zip(rows, res):
                o = rh[:, 0:HEAD_DIM] / rh[:, HEAD_DIM:]
                o_ref[0, r, :] = (o * _silu(g_ref[0, r, :].astype(F32))).astype(o_ref.dtype)

        pending = scores(0)
        for u in range(NA_TILES_PER_ITER):
            upcoming = scores(u + 1) if u + 1 < NA_TILES_PER_ITER else None
            finish(u, pending)
            pending = upcoming
        return carry

    lax.fori_loop(0, NA_TILES // NA_TILES_PER_ITER, tiles, 0)


def neighborhood_attention(p3, ctx_k, ctx_v, rpb):
    b, seq, _ = p3.shape
    ctx_len = ctx_k.shape[2]
    tab = _na_bias_blocks(rpb)
    n_pairs = tab.shape[1]
    ctx_spec = pl.BlockSpec((1, 1, ctx_len, HEAD_DIM), lambda h, i: (i, h, 0, 0))
    head_blk = lambda off: pl.BlockSpec((1, seq, HEAD_DIM), lambda h, i: (i, 0, off + h))
    return pl.pallas_call(
        _na_kernel,
        out_shape=jax.ShapeDtypeStruct((b, seq, HEADS * HEAD_DIM), BF16),
        grid=(HEADS, b),
        in_specs=[head_blk(0), head_blk(HEADS), head_blk(2 * HEADS), head_blk(3 * HEADS),
                  ctx_spec, ctx_spec,
                  pl.BlockSpec((1, n_pairs, GRID_W, 2 * GRID_W), lambda h, i: (h, 0, 0, 0))],
        out_specs=pl.BlockSpec((1, seq, HEAD_DIM), lambda h, i: (i, 0, h)),
        scratch_shapes=[pltpu.VMEM((NA_CASES, NA_TILE_Q, NA_TILE_K), F32),
                        pltpu.VMEM((ctx_len, HEAD_DIM), BF16), pltpu.VMEM((ctx_len, 2 * HEAD_DIM), BF16),
                        pltpu.VMEM((seq, 2 * HEAD_DIM), BF16)],
        compiler_params=_params("arbitrary", "arbitrary"),
        name="neighborhood_attention",
    )(p3, p3, p3, p3, ctx_k, ctx_v, tab)


def _sgu_kernel(u_ref, v_ref, g_ref, ln_ref, w_ref, b_ref, o_ref):
    v = v_ref[...].astype(F32)
    mu = jnp.mean(v, axis=-1, keepdims=True)
    vc = v - mu
    var = jnp.mean(vc * vc, axis=-1, keepdims=True)
    vn = (vc * lax.rsqrt(var + LN_EPS) * ln_ref[...]).astype(BF16)
    groups = w_ref.shape[0]
    for n in range(v.shape[0] // SGU_CHUNK):
        rows = slice(n * SGU_CHUNK, (n + 1) * SGU_CHUNK)
        for g in range(groups):
            cols = slice(g * SGU_GROUP_DIM, (g + 1) * SGU_GROUP_DIM)
            s = jnp.dot(w_ref[g], vn[rows, cols], preferred_element_type=F32) + b_ref[:, g:g + 1]
            y = u_ref[rows, cols].astype(F32) * s * _silu(g_ref[rows, cols].astype(F32))
            o_ref[rows, cols] = y.astype(o_ref.dtype)


def spatial_gating(p2, sgu_ln, w_s_bf16, b_s):
    m = p2.shape[0]
    groups = w_s_bf16.shape[0]
    width = groups * SGU_GROUP_DIM
    blk = lambda c: pl.BlockSpec((SGU_TM, width), lambda i: (i, c))
    return pl.pallas_call(
        _sgu_kernel,
        out_shape=jax.ShapeDtypeStruct((m, width), BF16),
        grid=(m // SGU_TM,),
        in_specs=[blk(4), blk(5), blk(6),
                  pl.BlockSpec((1, width), lambda i: (0, 0)),
                  pl.BlockSpec((groups, SGU_CHUNK, SGU_CHUNK), lambda i: (0, 0, 0)),
                  pl.BlockSpec((SGU_CHUNK, groups), lambda i: (0, 0))],
        out_specs=pl.BlockSpec((SGU_TM, width), lambda i: (i, 0)),
        compiler_params=_params("parallel"),
        name="spatial_gating",
    )(p2, p2, p2, sgu_ln.reshape(1, width), w_s_bf16, jnp.transpose(b_s))


def kernel(x_prompt, x_sample, cache_k_l0, cache_v_l0, cache_k_l1, cache_v_l1, cache_k_l2, cache_v_l2, cache_k_l3, cache_v_l3, c, c_ctx, w_mod_0, b_mod_0, w_in_0, w_out_0, ln_g_0, ln_b_0, pool_w_0, pool_scale_0, diff_lam_0, diff_subln_0, w_mod_1, b_mod_1, w_in_1, w_out_1, ln_g_1, ln_b_1, rpb_1, sgu_ln_1, sgu_w_1, sgu_b_1, w_mod_2, b_mod_2, w_in_2, w_out_2, ln_g_2, ln_b_2, pool_w_2, pool_scale_2, diff_lam_2, diff_subln_2, w_mod_3, b_mod_3, w_in_3, w_out_3, ln_g_3, ln_b_3, rpb_3, sgu_ln_3, sgu_w_3, sgu_b_3):
    cache_k = [cache_k_l0, cache_k_l1, cache_k_l2, cache_k_l3]
    cache_v = [cache_v_l0, cache_v_l1, cache_v_l2, cache_v_l3]
    w_mod = [w_mod_0, w_mod_1, w_mod_2, w_mod_3]
    b_mod = [b_mod_0, b_mod_1, b_mod_2, b_mod_3]
    w_in = [w_in_0, w_in_1, w_in_2, w_in_3]
    w_out = [w_out_0, w_out_1, w_out_2, w_out_3]
    ln_g = [ln_g_0, ln_g_1, ln_g_2, ln_g_3]
    ln_b = [ln_b_0, ln_b_1, ln_b_2, ln_b_3]
    even_p = {0: (pool_w_0, pool_scale_0, diff_lam_0, diff_subln_0),
              2: (pool_w_2, pool_scale_2, diff_lam_2, diff_subln_2)}
    odd_p = {1: (rpb_1, sgu_ln_1, sgu_w_1, sgu_b_1),
             3: (rpb_3, sgu_ln_3, sgu_w_3, sgu_b_3)}

    bp, lp, _ = x_prompt.shape
    bs, ls, _ = x_sample.shape
    xp = x_prompt.reshape(bp * lp, D_MODEL)
    xs = x_sample.reshape(bs * ls, D_MODEL)
    cond = jnp.concatenate([c, c_ctx[None, :], jnp.zeros((COND_ROWS - bs - 1, D_MODEL), F32)], axis=0)

    new_k, new_v = [], []
    for l in range(DEPTH):
        mod3 = ada_params(cond, w_mod[l], b_mod[l]).reshape(COND_ROWS, 1, 3 * D_MODEL)
        w_in_l = w_in[l].astype(BF16)
        n_in = w_in_l.shape[1]
        kv_block = (3 if l % 2 == 0 else 1) * HEADS * HEAD_DIM // PROJ_TN
        pp, nk, nv = in_projection(xp, mod3, w_in_l, None, kv_block=kv_block, cache_seq=lp)
        ps = in_projection(xs, mod3, w_in_l, ls)
        pp3 = pp.reshape(bp, lp, n_in)
        ps3 = ps.reshape(bs, ls, n_in)
        if l % 2 == 0:
            pool_w, pool_scale, diff_lam, diff_subln = even_p[l]
            pool_w = pool_w.astype(BF16)
            ya_p = pool_mixer(pp3, pool_w, pool_scale)
            ya_s = pool_mixer(ps3, pool_w, pool_scale)
            yb_p = diff_attention_context(pp3, diff_lam, diff_subln, l)
            yb_s = diff_attention_latent(ps3, diff_lam, diff_subln, l, cache_k[l], cache_v[l])
        else:
            rpb, sgu_ln, sgu_w, sgu_b = odd_p[l]
            sgu_w = sgu_w.astype(BF16)
            ya_p = context_attention(pp3)
            ya_s = neighborhood_attention(ps3, cache_k[l], cache_v[l], rpb)
            yb_p = spatial_gating(pp, sgu_ln, sgu_w, sgu_b)
            yb_s = spatial_gating(ps, sgu_ln, sgu_w, sgu_b)
        half = ya_p.shape[-1]
        w_out_l = w_out[l].astype(BF16)
        xp = out_projection(ya_p.reshape(bp * lp, half), yb_p.reshape(bp * lp, half), xp, mod3,
                            w_out_l, ln_g[l], ln_b[l], None)
        xs = out_projection(ya_s.reshape(bs * ls, half), yb_s.reshape(bs * ls, half), xs, mod3,
                            w_out_l, ln_g[l], ln_b[l], ls)
        new_k.append(nk)
        new_v.append(nv)

    return (xp.reshape(bp, lp, D_MODEL), xs.reshape(bs, ls, D_MODEL),
            new_k[0], new_v[0], new_k[1], new_v[1], new_k[2], new_v[2], new_k[3], new_v[3])
```

```python
import functools
import math

import numpy as np
import jax
import jax.numpy as jnp
from jax import lax
from jax.experimental import pallas as pl
from jax.experimental.pallas import tpu as pltpu

F32 = jnp.float32
BF16 = jnp.bfloat16

D_MODEL = 2048
DEPTH = 4
GRID_W = 64
GRID_ROWS = 64
HEADS = 8
HEAD_DIM = 128
QK_HALF = 64
POOL_WINDOWS = (2, 4, 8, 16)
POOL_GROUP_DIM = 256
SGU_CHUNK = 128
SGU_GROUP_DIM = 256
NA_WIN_R = 8
NA_WIN_C = 16
NA_Q_ROWS = 8
NA_K_ROWS = 16
ROPE_BASE = 10000.0
LN_EPS = 1e-5
NEG_INF = -1e30
DEEPNORM_ALPHA = (2 * DEPTH) ** 0.25
LOG2E = math.log2(math.e)

VMEM_LIMIT_BYTES = 56 * 1024 * 1024
COND_ROWS = 8
CTX_ROW = 4

PROJ_TM = 1024
PROJ_TN = 1024
PROJ_TN_WIDE = (2048, 1792)
OUT_TM = 1024
OUT_ROW_GROUP = 256
ADA_TN = 1536
POOL_ROWS = 256
POOL_HALO = 16
POOL_CHUNKS_PER_ITER = 8
SGU_TM = 1024
DIFF_TQ = 256
DIFF_TILES_PER_ITER = 16
PREP_ROWS = 512
CTX_BATCH_PER_STEP = 4
CTX_HEAD_GROUP = 2


def _params(*sem):
    return pltpu.CompilerParams(dimension_semantics=sem, vmem_limit_bytes=VMEM_LIMIT_BYTES)


def _silu(x):
    half_x = 0.5 * x
    return half_x * jnp.tanh(half_x) + half_x


def _dot_nt(a, b):
    return lax.dot_general(a, b, (((1,), (1,)), ((), ())), preferred_element_type=F32)


def _ada_kernel(c_ref, w_ref, b_ref, o_ref):
    a = _silu(c_ref[...]).astype(BF16)
    o_ref[...] = jnp.dot(a, w_ref[...].astype(BF16), preferred_element_type=F32) + b_ref[...]


def ada_params(cond, w_mod, b_mod):
    n = w_mod.shape[1]
    return pl.pallas_call(
        _ada_kernel,
        out_shape=jax.ShapeDtypeStruct((COND_ROWS, n), F32),
        grid=(n // ADA_TN,),
        in_specs=[pl.BlockSpec((COND_ROWS, D_MODEL), lambda j: (0, 0)),
                  pl.BlockSpec((D_MODEL, ADA_TN), lambda j: (0, j)),
                  pl.BlockSpec((1, ADA_TN), lambda j: (0, j))],
        out_specs=pl.BlockSpec((COND_ROWS, ADA_TN), lambda j: (0, j)),
        compiler_params=_params("parallel"),
        name="ada_params",
    )(cond, w_mod, b_mod.reshape(1, n))


def _inproj_kernel(x_ref, sh_ref, sc_ref, w_ref, o_ref, *rest, kv_block):
    h_scr = rest[-1]
    j = pl.program_id(1)

    @pl.when(j == 0)
    def _():
        h = x_ref[...] * (1.0 + sc_ref[0]) + sh_ref[0]
        h_scr[...] = h.astype(BF16)

    acc = jnp.dot(h_scr[...], w_ref[...], preferred_element_type=F32)
    o_ref[...] = acc.astype(o_ref.dtype)
    if kv_block is not None:
        for which, cache_ref in enumerate(rest[:2]):
            @pl.when(j == kv_block + which)
            def _(cache_ref=cache_ref):
                n_batch, n_heads, seq, head_dim = cache_ref.shape
                for bi in range(n_batch):
                    for hd in range(n_heads):
                        cache_ref[bi, hd] = acc[bi * seq:(bi + 1) * seq, hd * head_dim:(hd + 1) * head_dim]


def in_projection(x2d, mod3, w_bf16, rows_per_cond, kv_block=None, cache_seq=None):
    m = x2d.shape[0]
    n = w_bf16.shape[1]
    if rows_per_cond is None:
        cond_row = lambda i: CTX_ROW
    else:
        blocks_per_cond = rows_per_cond // PROJ_TM
        cond_row = lambda i: i // blocks_per_cond
    tn = PROJ_TN
    if kv_block is None:
        tn = next((c for c in PROJ_TN_WIDE if n % c == 0), PROJ_TN)
    out_shape = [jax.ShapeDtypeStruct((m, n), BF16)]
    out_specs = [pl.BlockSpec((PROJ_TM, tn), lambda i, j: (i, j))]
    if kv_block is not None:
        assert PROJ_TN == HEADS * HEAD_DIM and PROJ_TM % cache_seq == 0
        per_block = PROJ_TM // cache_seq
        cache_spec = pl.BlockSpec((per_block, HEADS, cache_seq, HEAD_DIM), lambda i, j: (i, 0, 0, 0))
        out_shape += [jax.ShapeDtypeStruct((m // cache_seq, HEADS, cache_seq, HEAD_DIM), F32)] * 2
        out_specs += [cache_spec, cache_spec]
    outs = pl.pallas_call(
        functools.partial(_inproj_kernel, kv_block=kv_block),
        out_shape=out_shape,
        grid=(m // PROJ_TM, n // tn),
        in_specs=[pl.BlockSpec((PROJ_TM, D_MODEL), lambda i, j: (i, 0)),
                  pl.BlockSpec((1, 1, D_MODEL), lambda i, j: (cond_row(i), 0, 0)),
                  pl.BlockSpec((1, 1, D_MODEL), lambda i, j: (cond_row(i), 0, 1)),
                  pl.BlockSpec((D_MODEL, tn), lambda i, j: (0, j))],
        out_specs=out_specs,
        scratch_shapes=[pltpu.VMEM((PROJ_TM, D_MODEL), BF16)],
        compiler_params=_params("parallel", "arbitrary"),
        name="in_projection",
    )(x2d, mod3, mod3, w_bf16)
    return outs[0] if kv_block is None else outs


def _outproj_kernel(ya_ref, yb_ref, x_ref, g_ref, w_ref, lng_ref, lnb_ref, o_ref):
    half = ya_ref.shape[1]
    for r in range(OUT_TM // OUT_ROW_GROUP):
        rows = slice(r * OUT_ROW_GROUP, (r + 1) * OUT_ROW_GROUP)
        acc = jnp.dot(ya_ref[rows, :], w_ref[0:half, :], preferred_element_type=F32)
        acc = acc + jnp.dot(yb_ref[rows, :], w_ref[half:, :], preferred_element_type=F32)
        z = DEEPNORM_ALPHA * x_ref[rows, :] + g_ref[0] * acc
        mu = jnp.mean(z, axis=-1, keepdims=True)
        zc = z - mu
        var = jnp.mean(zc * zc, axis=-1, keepdims=True)
        o_ref[rows, :] = zc * lax.rsqrt(var + LN_EPS) * lng_ref[...] + lnb_ref[...]


def out_projection(ya, yb, x2d, mod3, w_bf16, ln_g, ln_b, rows_per_cond):
    m = x2d.shape[0]
    half = ya.shape[1]
    if rows_per_cond is None:
        cond_row = lambda i: CTX_ROW
    else:
        blocks_per_cond = rows_per_cond // OUT_TM
        cond_row = lambda i: i // blocks_per_cond
    return pl.pallas_call(
        _outproj_kernel,
        out_shape=jax.ShapeDtypeStruct((m, D_MODEL), F32),
        grid=(m // OUT_TM,),
        in_specs=[pl.BlockSpec((OUT_TM, half), lambda i: (i, 0)),
                  pl.BlockSpec((OUT_TM, half), lambda i: (i, 0)),
                  pl.BlockSpec((OUT_TM, D_MODEL), lambda i: (i, 0)),
                  pl.BlockSpec((1, 1, D_MODEL), lambda i: (cond_row(i), 0, 2)),
                  pl.BlockSpec((2 * half, D_MODEL), lambda i: (0, 0), pipeline_mode=pl.Buffered(1)),
                  pl.BlockSpec((1, D_MODEL), lambda i: (0, 0)),
                  pl.BlockSpec((1, D_MODEL), lambda i: (0, 0))],
        out_specs=pl.BlockSpec((OUT_TM, D_MODEL), lambda i: (i, 0)),
        compiler_params=_params("parallel"),
        name="out_projection",
    )(ya, yb, x2d, mod3, w_bf16, ln_g.reshape(1, D_MODEL), ln_b.reshape(1, D_MODEL))


def _pool_bands():
    assert max(POOL_WINDOWS) // 2 <= POOL_HALO
    i = np.arange(POOL_ROWS)[:, None]
    own = np.arange(POOL_ROWS)[None, :]
    mains = [(own >= i - w // 2) & (own < i + w // 2) for w in POOL_WINDOWS]
    top_i = np.arange(POOL_HALO)[:, None]
    bot_i = POOL_ROWS - POOL_HALO + top_i
    before = np.arange(POOL_HALO)[None, :] - POOL_HALO
    after = POOL_ROWS + np.arange(POOL_HALO)[None, :]
    halos = [np.stack([before >= top_i - w // 2, after < bot_i + w // 2]) for w in POOL_WINDOWS]
    return np.stack(mains).astype(np.float32), np.stack(halos).astype(np.float32)


def _pool_kernel(a_ref, g_ref, w_ref, s_ref, bm_ref, bh_ref, o_ref, *, seq):
    n_chunks = seq // POOL_ROWS
    group = pl.program_id(1)
    half = jnp.int32(0)
    for gi, window in enumerate(POOL_WINDOWS):
        half = jnp.where(group == gi, window // 2, half)

    def chunk(bi, ci):
        r0 = pl.multiple_of(ci * POOL_ROWS, POOL_ROWS)
        main = a_ref[bi, pl.ds(r0, POOL_ROWS), :]
        win = jnp.dot(bm_ref[0], main.astype(BF16), preferred_element_type=F32)
        if n_chunks > 1:
            p0 = pl.multiple_of(jnp.maximum(r0 - POOL_HALO, 0), POOL_HALO)
            n0 = pl.multiple_of(jnp.minimum(r0 + POOL_ROWS, seq - POOL_HALO), POOL_HALO)
            prev = jnp.where(ci > 0, a_ref[bi, pl.ds(p0, POOL_HALO), :].astype(F32), 0.0).astype(BF16)
            nxt = jnp.where(ci < n_chunks - 1, a_ref[bi, pl.ds(n0, POOL_HALO), :].astype(F32), 0.0).astype(BF16)
            top = win[0:POOL_HALO] + jnp.dot(bh_ref[0, 0], prev, preferred_element_type=F32)
            bot = win[POOL_ROWS - POOL_HALO:] + jnp.dot(bh_ref[0, 1], nxt, preferred_element_type=F32)
            win = jnp.concatenate([top, win[POOL_HALO:POOL_ROWS - POOL_HALO], bot], axis=0)
        t = r0 + lax.broadcasted_iota(jnp.int32, (POOL_ROWS, 1), 0)
        cnt = (jnp.minimum(t + half, seq) - jnp.maximum(t - half, 0)).astype(F32)
        pooled = win / cnt - main.astype(F32)
        mixed = jnp.dot(pooled.astype(BF16), w_ref[0], preferred_element_type=F32)
        gate = g_ref[bi, pl.ds(r0, POOL_ROWS), :].astype(F32)
        y = mixed * s_ref[...] * _silu(gate)
        o_ref[bi, pl.ds(r0, POOL_ROWS), :] = y.astype(o_ref.dtype)

    def group_of_chunks(it, carry):
        for u in range(POOL_CHUNKS_PER_ITER):
            idx = it * POOL_CHUNKS_PER_ITER + u
            chunk(idx // n_chunks, idx % n_chunks)
        return carry

    lax.fori_loop(0, a_ref.shape[0] * n_chunks // POOL_CHUNKS_PER_ITER, group_of_chunks, 0)


def pool_mixer(p3, w_pool_bf16, pool_scale):
    b, seq, _ = p3.shape
    groups = len(POOL_WINDOWS)
    band_main, band_halo = (jnp.asarray(m, BF16) for m in _pool_bands())
    bb = max(1, POOL_CHUNKS_PER_ITER * POOL_ROWS // seq)
    return pl.pallas_call(
        functools.partial(_pool_kernel, seq=seq),
        out_shape=jax.ShapeDtypeStruct((b, seq, groups * POOL_GROUP_DIM), BF16),
        grid=(b // bb, groups),
        in_specs=[pl.BlockSpec((bb, seq, POOL_GROUP_DIM), lambda i, g: (i, 0, g)),
                  pl.BlockSpec((bb, seq, POOL_GROUP_DIM), lambda i, g: (i, 0, groups + g)),
                  pl.BlockSpec((1, POOL_GROUP_DIM, POOL_GROUP_DIM), lambda i, g: (g, 0, 0)),
                  pl.BlockSpec((1, POOL_GROUP_DIM), lambda i, g: (0, g)),
                  pl.BlockSpec((1, POOL_ROWS, POOL_ROWS), lambda i, g: (g, 0, 0)),
                  pl.BlockSpec((1, 2, POOL_HALO, POOL_HALO), lambda i, g: (g, 0, 0, 0))],
        out_specs=pl.BlockSpec((bb, seq, POOL_GROUP_DIM), lambda i, g: (i, 0, g)),
        compiler_params=_params("parallel", "parallel"),
        name="pool_mixer",
    )(p3, p3, w_pool_bf16, pool_scale.reshape(1, groups * POOL_GROUP_DIM), band_main, band_halo)


def _rope_tables():
    t = np.arange(GRID_ROWS * GRID_W)
    row = (t // GRID_W).astype(np.float64)[:, None]
    col = (t % GRID_W).astype(np.float64)[:, None]
    axis_dim = QK_HALF // 2
    inv = 1.0 / (ROPE_BASE ** (np.arange(0, axis_dim, 2, dtype=np.float64) / axis_dim))
    lane = np.arange(HEAD_DIM)
    within = lane % QK_HALF
    pos = np.where((within < axis_dim)[None, :], row, col)
    ang = pos * inv[within % (axis_dim // 2)][None, :]
    is_a = ((within % axis_dim) < axis_dim // 2)[None, :]
    cos = np.cos(ang)
    sin = np.sin(ang)
    return (cos.astype(np.float32),
            np.where(is_a, -sin, 0.0).astype(np.float32),
            np.where(is_a, 0.0, sin).astype(np.float32))


def _rotate_lanes(x, cos, sin_a, sin_b):
    shift = QK_HALF // 4
    return (x * cos + pltpu.roll(x, HEAD_DIM - shift, 1) * sin_a
            + pltpu.roll(x, shift, 1) * sin_b)


def _diff_lambda(lam_ref, lam_init):
    dl = lam_ref[...]
    return (jnp.exp(jnp.sum(dl[0:1] * dl[1:2], axis=-1, keepdims=True))
            - jnp.exp(jnp.sum(dl[2:3] * dl[3:4], axis=-1, keepdims=True)) + lam_init)


def _split_components(q):
    lane = lax.broadcasted_iota(jnp.int32, (1, HEAD_DIM), 1)
    return (jnp.where(lane < QK_HALF, q, 0.0).astype(BF16),
            jnp.where(lane >= QK_HALF, q, 0.0).astype(BF16))


def _diff_output(r1, r2, lam, sub, lam_init, gate):
    o = r1[:, 0:HEAD_DIM] * (1.0 / r1[:, HEAD_DIM:]) - r2[:, 0:HEAD_DIM] * (lam / r2[:, HEAD_DIM:])
    o = o * lax.rsqrt(jnp.mean(o * o, axis=-1, keepdims=True) + LN_EPS)
    o = o * sub * (1.0 - lam_init)
    return o * _silu(gate)


def _diff_latent_kernel(q_ref, k_ref, v_ref, g_ref, ck_ref, cv_ref, cos_ref, sa_ref, sb_ref,
                        lam_ref, sub_ref, o_ref, k_scr, v_scr, *, seq, tq, lam_init):
    ones_col = jnp.ones((PREP_ROWS, HEAD_DIM), BF16)

    def stage(ci, carry):
        rows = pl.ds(pl.multiple_of(ci * PREP_ROWS, PREP_ROWS), PREP_ROWS)
        k = _rotate_lanes(k_ref[0, rows, :].astype(F32), cos_ref[rows, :], sa_ref[rows, :], sb_ref[rows, :])
        k_scr[rows, :] = k.astype(BF16)
        v_scr[rows, 0:HEAD_DIM] = v_ref[0, rows, :].astype(BF16)
        v_scr[rows, HEAD_DIM:] = ones_col
        return carry

    lax.fori_loop(0, seq // PREP_ROWS, stage, 0)
    k_scr[seq:, :] = ck_ref[0, 0].astype(BF16)
    v_scr[seq:, 0:HEAD_DIM] = cv_ref[0, 0].astype(BF16)
    v_scr[seq:, HEAD_DIM:] = ones_col[0:v_scr.shape[0] - seq]
    lam = _diff_lambda(lam_ref, lam_init)

    def tiles(it, carry):
        keys_all = k_scr[...]
        values = v_scr[...]
        def rows_of(u):
            return pl.ds(pl.multiple_of((it * DIFF_TILES_PER_ITER + u) * tq, tq), tq)

        def scores(u):
            r = rows_of(u)
            pair = _split_components(
                _rotate_lanes(q_ref[0, r, :].astype(F32), cos_ref[r, :], sa_ref[r, :], sb_ref[r, :])
                * (QK_HALF ** -0.5 * LOG2E))
            return [_dot_nt(qc, keys_all) for qc in pair]

        def finish(u, s):
            r = rows_of(u)
            e = [jnp.exp2(sc - jnp.max(sc, axis=-1, keepdims=True)).astype(BF16) for sc in s]
            r1, r2 = [jnp.dot(ec, values, preferred_element_type=F32) for ec in e]
            y = _diff_output(r1, r2, lam, sub_ref[...], lam_init, g_ref[0, r, :].astype(F32))
            o_ref[0, r, :] = y.astype(o_ref.dtype)

        pending = scores(0)
        for u in range(DIFF_TILES_PER_ITER):
            upcoming = scores(u + 1) if u + 1 < DIFF_TILES_PER_ITER else None
            finish(u, pending)
            pending = upcoming
        return carry

    lax.fori_loop(0, seq // (tq * DIFF_TILES_PER_ITER), tiles, 0)


def diff_attention_latent(p3, diff_lam, diff_subln, layer_idx, ctx_k, ctx_v):
    b, seq, _ = p3.shape
    ctx_len = ctx_k.shape[2]
    tq = DIFF_TQ
    nk = seq + ctx_len
    lam_init = 0.8 - 0.6 * math.exp(-0.3 * layer_idx)
    q0, k0, v0, g0 = 2 * HEADS, 3 * HEADS, 4 * HEADS, 5 * HEADS
    const = lambda i, h: (0, 0)
    head_blk = lambda off: pl.BlockSpec((1, seq, HEAD_DIM), lambda i, h: (i, 0, off + h))
    ctx_spec = pl.BlockSpec((1, 1, ctx_len, HEAD_DIM), lambda i, h: (i, h, 0, 0))
    tab_spec = pl.BlockSpec((seq, HEAD_DIM), const, pipeline_mode=pl.Buffered(1))
    return pl.pallas_call(
        functools.partial(_diff_latent_kernel, seq=seq, tq=tq, lam_init=lam_init),
        out_shape=jax.ShapeDtypeStruct((b, seq, HEADS * HEAD_DIM), BF16),
        grid=(b, HEADS),
        in_specs=[head_blk(q0), head_blk(k0), head_blk(v0), head_blk(g0),
                  ctx_spec, ctx_spec, tab_spec, tab_spec, tab_spec,
                  pl.BlockSpec(diff_lam.shape, const),
                  pl.BlockSpec((1, HEAD_DIM), const)],
        out_specs=pl.BlockSpec((1, seq, HEAD_DIM), lambda i, h: (i, 0, h)),
        scratch_shapes=[pltpu.VMEM((nk, HEAD_DIM), BF16), pltpu.VMEM((nk, 2 * HEAD_DIM), BF16)],
        compiler_params=_params("parallel", "parallel"),
        name="diff_attention_latent",
    )(p3, p3, p3, p3, ctx_k, ctx_v, *[jnp.asarray(tab) for tab in _rope_tables()],
      diff_lam, diff_subln.reshape(1, HEAD_DIM))


def _diff_context_kernel(q_ref, k_ref, v_ref, g_ref, lam_ref, sub_ref, o_ref, *, lam_init):
    lam = _diff_lambda(lam_ref, lam_init)
    heads = [slice(h * HEAD_DIM, (h + 1) * HEAD_DIM) for h in range(HEADS)]
    seq = q_ref.shape[1]
    ones_col = jnp.ones((seq, HEAD_DIM), BF16)

    def one(bi, carry):
        def scores(group):
            qs = [_split_components(q_ref[bi, :, c].astype(F32) * (QK_HALF ** -0.5 * LOG2E)) for c in group]
            return [[_dot_nt(qc, k_ref[bi, :, c]) for qc in pair] for pair, c in zip(qs, group)]

        def finish(group, s):
            e = [[jnp.exp2(sc - jnp.max(sc, axis=-1, keepdims=True)).astype(BF16) for sc in sh] for sh in s]
            v_aug = [jnp.concatenate([v_ref[bi, :, c], ones_col], axis=-1) for c in group]
            r = [[jnp.dot(ec, va, preferred_element_type=F32) for ec in eh] for eh, va in zip(e, v_aug)]
            for (r1, r2), c in zip(r, group):
                y = _diff_output(r1, r2, lam, sub_ref[...], lam_init, g_ref[bi, :, c].astype(F32))
                o_ref[bi, :, c] = y.astype(o_ref.dtype)

        groups = [heads[i:i + CTX_HEAD_GROUP] for i in range(0, HEADS, CTX_HEAD_GROUP)]
        pending = scores(groups[0])
        for n, group in enumerate(groups):
            upcoming = scores(groups[n + 1]) if n + 1 < len(groups) else None
            finish(group, pending)
            pending = upcoming
        return carry

    lax.fori_loop(0, q_ref.shape[0], one, 0)


def diff_attention_context(p3, diff_lam, diff_subln, layer_idx):
    b, seq, _ = p3.shape
    width = HEADS * HEAD_DIM
    lam_init = 0.8 - 0.6 * math.exp(-0.3 * layer_idx)
    bb = CTX_BATCH_PER_STEP
    blk = lambda c: pl.BlockSpec((bb, seq, width), lambda i: (i, 0, c))
    return pl.pallas_call(
        functools.partial(_diff_context_kernel, lam_init=lam_init),
        out_shape=jax.ShapeDtypeStruct((b, seq, width), BF16),
        grid=(b // bb,),
        in_specs=[blk(2), blk(3), blk(4), blk(5),
                  pl.BlockSpec(diff_lam.shape, lambda i: (0, 0)),
                  pl.BlockSpec((1, HEAD_DIM), lambda i: (0, 0))],
        out_specs=pl.BlockSpec((bb, seq, width), lambda i: (i, 0, 0)),
        compiler_params=_params("parallel"),
        name="diff_attention_context",
    )(p3, p3, p3, p3, diff_lam, diff_subln.reshape(1, HEAD_DIM))


def _ctx_attn_kernel(q_ref, k_ref, v_ref, g_ref, o_ref):
    heads = [slice(h * HEAD_DIM, (h + 1) * HEAD_DIM) for h in range(HEADS)]
    seq = q_ref.shape[1]
    ones_col = jnp.ones((seq, HEAD_DIM), BF16)

    def one(bi, carry):
        q = [(q_ref[bi, :, c].astype(F32) * (HEAD_DIM ** -0.5 * LOG2E)).astype(BF16) for c in heads]
        s = [_dot_nt(q[h], k_ref[bi, :, c]) for h, c in enumerate(heads)]
        e = [jnp.exp2(sh - jnp.max(sh, axis=-1, keepdims=True)).astype(BF16) for sh in s]
        r = [jnp.dot(e[h], jnp.concatenate([v_ref[bi, :, c], ones_col], axis=-1), preferred_element_type=F32)
             for h, c in enumerate(heads)]
        for h, c in enumerate(heads):
            o = r[h][:, 0:HEAD_DIM] / r[h][:, HEAD_DIM:]
            o_ref[bi, :, c] = (o * _silu(g_ref[bi, :, c].astype(F32))).astype(o_ref.dtype)
        return carry

    lax.fori_loop(0, q_ref.shape[0], one, 0)


def context_attention(p3):
    b, seq, _ = p3.shape
    width = HEADS * HEAD_DIM
    bb = CTX_BATCH_PER_STEP
    blk = lambda c: pl.BlockSpec((bb, seq, width), lambda i: (i, 0, c))
    return pl.pallas_call(
        _ctx_attn_kernel,
        out_shape=jax.ShapeDtypeStruct((b, seq, width), BF16),
        grid=(b // bb,),
        in_specs=[blk(0), blk(1), blk(2), blk(3)],
        out_specs=pl.BlockSpec((bb, seq, width), lambda i: (i, 0, 0)),
        compiler_params=_params("parallel"),
        name="context_attention",
    )(p3, p3, p3, p3)


NA_TILE_Q = NA_Q_ROWS * GRID_W
NA_TILE_K = NA_K_ROWS * GRID_W
NA_TILES = GRID_ROWS // NA_Q_ROWS
NA_TILES_PER_ITER = 8
NA_CASES = 3
NA_KEY_ALIGN = (NA_WIN_R // 2) * GRID_W
NA_DR = 2 * NA_WIN_R - 1


def _na_key_row_start(tile):
    return int(np.clip(NA_Q_ROWS * tile - NA_WIN_R // 2, 0, GRID_ROWS - NA_K_ROWS))


def _na_bias_layout():
    dr = np.full((NA_CASES, NA_Q_ROWS, NA_K_ROWS), NA_DR, np.int32)
    for case, tile in enumerate((0, 1, NA_TILES - 1)):
        ks = _na_key_row_start(tile)
        for qr_local in range(NA_Q_ROWS):
            qr = NA_Q_ROWS * tile + qr_local
            rs = int(np.clip(qr - NA_WIN_R // 2, 0, GRID_ROWS - NA_WIN_R))
            for kr_local in range(NA_K_ROWS):
                kr = ks + kr_local
                if rs <= kr < rs + NA_WIN_R:
                    dr[case, qr_local, kr_local] = kr - qr + NA_WIN_R - 1
    pairs = sorted({(int(a), int(b)) for a, b in zip(dr[..., 0::2].ravel(), dr[..., 1::2].ravel())})
    index = {p: i for i, p in enumerate(pairs)}
    block = np.array([[[index[(int(dr[c, q, 2 * p]), int(dr[c, q, 2 * p + 1]))]
                        for p in range(NA_K_ROWS // 2)] for q in range(NA_Q_ROWS)] for c in range(NA_CASES)])
    return pairs, block


def _na_bias_blocks(rpb):
    qc = np.arange(GRID_W)[:, None]
    kc = np.arange(GRID_W)[None, :]
    cstart = np.clip(qc - NA_WIN_C // 2, 0, GRID_W - NA_WIN_C)
    col_valid = (kc >= cstart) & (kc < cstart + NA_WIN_C)
    dc = np.clip(kc - qc + NA_WIN_C - 1, 0, 2 * NA_WIN_C - 2)
    onehot = (dc.reshape(-1)[None, :] == np.arange(2 * NA_WIN_C - 1)[:, None]).astype(np.float32)
    toeplitz = jnp.einsum('hrj,jn->hrn', rpb, jnp.asarray(onehot), precision=lax.Precision.HIGHEST)
    per_dr = jnp.where(jnp.asarray(col_valid), toeplitz.reshape(rpb.shape[:2] + dc.shape), NEG_INF)
    per_dr = jnp.concatenate([per_dr, jnp.full_like(per_dr[:, :1], NEG_INF)], axis=1)
    pairs, _ = _na_bias_layout()
    left = np.array([p[0] for p in pairs])
    right = np.array([p[1] for p in pairs])
    return jnp.concatenate([per_dr[:, left], per_dr[:, right]], axis=-1)


def _na_kernel(q_ref, k_ref, v_ref, g_ref, ck_ref, cv_ref, tab_ref, o_ref,
               bias_scr, ck_scr, cv_scr, v_scr):
    b = pl.program_id(1)
    seq = v_ref.shape[1]

    @pl.when(b == 0)
    def _build_bias():
        _, block = _na_bias_layout()
        for case in range(NA_CASES):
            for qr in range(NA_Q_ROWS):
                for p in range(NA_K_ROWS // 2):
                    bias_scr[case, qr * GRID_W:(qr + 1) * GRID_W, p * 2 * GRID_W:(p + 1) * 2 * GRID_W] = (
                        tab_ref[0, int(block[case, qr, p])] * LOG2E)

    ones_col = jnp.ones((PREP_ROWS, HEAD_DIM), BF16)
    ck_scr[...] = ck_ref[0, 0].astype(BF16)
    cv_scr[:, 0:HEAD_DIM] = cv_ref[0, 0].astype(BF16)
    cv_scr[:, HEAD_DIM:] = ones_col[0:cv_scr.shape[0]]

    def stage(ci, carry):
        rows = pl.ds(pl.multiple_of(ci * PREP_ROWS, PREP_ROWS), PREP_ROWS)
        v_scr[rows, 0:HEAD_DIM] = v_ref[0, rows, :]
        v_scr[rows, HEAD_DIM:] = ones_col
        return carry

    lax.fori_loop(0, seq // PREP_ROWS, stage, 0)
    half_rows = NA_TILE_Q // 2

    def tiles(it, carry):
        k_ctx = ck_scr[...]
        v_ctx = cv_scr[...]

        def window(u):
            t = it * NA_TILES_PER_ITER + u
            case = jnp.where(t == 0, 0, jnp.where(t == NA_TILES - 1, 2, 1))
            key_row = jnp.clip(NA_Q_ROWS * t - NA_WIN_R // 2, 0, GRID_ROWS - NA_K_ROWS)
            keys = pl.ds(pl.multiple_of(key_row * GRID_W, NA_KEY_ALIGN), NA_TILE_K)
            rows = [pl.ds(pl.multiple_of(t * NA_TILE_Q + i * half_rows, half_rows), half_rows) for i in range(2)]
            return case, keys, rows

        def scores(u):
            case, keys, rows = window(u)
            k_loc = k_ref[0, keys, :]
            q = [(q_ref[0, r, :].astype(F32) * (HEAD_DIM ** -0.5 * LOG2E)).astype(BF16) for r in rows]
            s_loc = [_dot_nt(qh, k_loc) + bias_scr[case, i * half_rows:(i + 1) * half_rows, :]
                     for i, qh in enumerate(q)]
            s_ctx = [_dot_nt(qh, k_ctx) for qh in q]
            return s_loc, s_ctx

        def finish(u, s):
            _, keys, rows = window(u)
            s_loc, s_ctx = s
            m = [jnp.maximum(jnp.max(sl, axis=-1, keepdims=True), jnp.max(sc, axis=-1, keepdims=True))
                 for sl, sc in zip(s_loc, s_ctx)]
            e_loc = [jnp.exp2(sl - mh).astype(BF16) for sl, mh in zip(s_loc, m)]
            e_ctx = [jnp.exp2(sc - mh).astype(BF16) for sc, mh in zip(s_ctx, m)]
            v_loc = v_scr[keys, :]
            res = [jnp.dot(el, v_loc, preferred_element_type=F32) + jnp.dot(ec, v_ctx, preferred_element_type=F32)
                   for el, ec in zip(e_loc, e_ctx)]
            for r, rh in zip(rows, res):
                o = rh[:, 0:HEAD_DIM] / rh[:, HEAD_DIM:]
                o_ref[0, r, :] = (o * _silu(g_ref[0, r, :].astype(F32))).astype(o_ref.dtype)

        pending = scores(0)
        for u in range(NA_TILES_PER_ITER):
            upcoming = scores(u + 1) if u + 1 < NA_TILES_PER_ITER else None
            finish(u, pending)
            pending = upcoming
        return carry

    lax.fori_loop(0, NA_TILES // NA_TILES_PER_ITER, tiles, 0)


def neighborhood_attention(p3, ctx_k, ctx_v, rpb):
    b, seq, _ = p3.shape
    ctx_len = ctx_k.shape[2]
    tab = _na_bias_blocks(rpb)
    n_pairs = tab.shape[1]
    ctx_spec = pl.BlockSpec((1, 1, ctx_len, HEAD_DIM), lambda h, i: (i, h, 0, 0))
    head_blk = lambda off: pl.BlockSpec((1, seq, HEAD_DIM), lambda h, i: (i, 0, off + h))
    return pl.pallas_call(
        _na_kernel,
        out_shape=jax.ShapeDtypeStruct((b, seq, HEADS * HEAD_DIM), BF16),
        grid=(HEADS, b),
        in_specs=[head_blk(0), head_blk(HEADS), head_blk(2 * HEADS), head_blk(3 * HEADS),
                  ctx_spec, ctx_spec,
                  pl.BlockSpec((1, n_pairs, GRID_W, 2 * GRID_W), lambda h, i: (h, 0, 0, 0))],
        out_specs=pl.BlockSpec((1, seq, HEAD_DIM), lambda h, i: (i, 0, h)),
        scratch_shapes=[pltpu.VMEM((NA_CASES, NA_TILE_Q, NA_TILE_K), F32),
                        pltpu.VMEM((ctx_len, HEAD_DIM), BF16), pltpu.VMEM((ctx_len, 2 * HEAD_DIM), BF16),
                        pltpu.VMEM((seq, 2 * HEAD_DIM), BF16)],
        compiler_params=_params("arbitrary", "arbitrary"),
        name="neighborhood_attention",
    )(p3, p3, p3, p3, ctx_k, ctx_v, tab)


def _sgu_kernel(u_ref, v_ref, g_ref, ln_ref, w_ref, b_ref, o_ref):
    v = v_ref[...].astype(F32)
    mu = jnp.mean(v, axis=-1, keepdims=True)
    vc = v - mu
    var = jnp.mean(vc * vc, axis=-1, keepdims=True)
    vn = (vc * lax.rsqrt(var + LN_EPS) * ln_ref[...]).astype(BF16)
    groups = w_ref.shape[0]
    for n in range(v.shape[0] // SGU_CHUNK):
        rows = slice(n * SGU_CHUNK, (n + 1) * SGU_CHUNK)
        for g in range(groups):
            cols = slice(g * SGU_GROUP_DIM, (g + 1) * SGU_GROUP_DIM)
            s = jnp.dot(w_ref[g], vn[rows, cols], preferred_element_type=F32) + b_ref[:, g:g + 1]
            y = u_ref[rows, cols].astype(F32) * s * _silu(g_ref[rows, cols].astype(F32))
            o_ref[rows, cols] = y.astype(o_ref.dtype)


def spatial_gating(p2, sgu_ln, w_s_bf16, b_s):
    m = p2.shape[0]
    groups = w_s_bf16.shape[0]
    width = groups * SGU_GROUP_DIM
    blk = lambda c: pl.BlockSpec((SGU_TM, width), lambda i: (i, c))
    return pl.pallas_call(
        _sgu_kernel,
        out_shape=jax.ShapeDtypeStruct((m, width), BF16),
        grid=(m // SGU_TM,),
        in_specs=[blk(4), blk(5), blk(6),
                  pl.BlockSpec((1, width), lambda i: (0, 0)),
                  pl.BlockSpec((groups, SGU_CHUNK, SGU_CHUNK), lambda i: (0, 0, 0)),
                  pl.BlockSpec((SGU_CHUNK, groups), lambda i: (0, 0))],
        out_specs=pl.BlockSpec((SGU_TM, width), lambda i: (i, 0)),
        compiler_params=_params("parallel"),
        name="spatial_gating",
    )(p2, p2, p2, sgu_ln.reshape(1, width), w_s_bf16, jnp.transpose(b_s))


def kernel(x_prompt, x_sample, cache_k_l0, cache_v_l0, cache_k_l1, cache_v_l1, cache_k_l2, cache_v_l2, cache_k_l3, cache_v_l3, c, c_ctx, w_mod_0, b_mod_0, w_in_0, w_out_0, ln_g_0, ln_b_0, pool_w_0, pool_scale_0, diff_lam_0, diff_subln_0, w_mod_1, b_mod_1, w_in_1, w_out_1, ln_g_1, ln_b_1, rpb_1, sgu_ln_1, sgu_w_1, sgu_b_1, w_mod_2, b_mod_2, w_in_2, w_out_2, ln_g_2, ln_b_2, pool_w_2, pool_scale_2, diff_lam_2, diff_subln_2, w_mod_3, b_mod_3, w_in_3, w_out_3, ln_g_3, ln_b_3, rpb_3, sgu_ln_3, sgu_w_3, sgu_b_3):
    cache_k = [cache_k_l0, cache_k_l1, cache_k_l2, cache_k_l3]
    cache_v = [cache_v_l0, cache_v_l1, cache_v_l2, cache_v_l3]
    w_mod = [w_mod_0, w_mod_1, w_mod_2, w_mod_3]
    b_mod = [b_mod_0, b_mod_1, b_mod_2, b_mod_3]
    w_in = [w_in_0, w_in_1, w_in_2, w_in_3]
    w_out = [w_out_0, w_out_1, w_out_2, w_out_3]
    ln_g = [ln_g_0, ln_g_1, ln_g_2, ln_g_3]
    ln_b = [ln_b_0, ln_b_1, ln_b_2, ln_b_3]
    even_p = {0: (pool_w_0, pool_scale_0, diff_lam_0, diff_subln_0),
              2: (pool_w_2, pool_scale_2, diff_lam_2, diff_subln_2)}
    odd_p = {1: (rpb_1, sgu_ln_1, sgu_w_1, sgu_b_1),
             3: (rpb_3, sgu_ln_3, sgu_w_3, sgu_b_3)}

    bp, lp, _ = x_prompt.shape
    bs, ls, _ = x_sample.shape
    xp = x_prompt.reshape(bp * lp, D_MODEL)
    xs = x_sample.reshape(bs * ls, D_MODEL)
    cond = jnp.concatenate([c, c_ctx[None, :], jnp.zeros((COND_ROWS - bs - 1, D_MODEL), F32)], axis=0)

    new_k, new_v = [], []
    for l in range(DEPTH):
        mod3 = ada_params(cond, w_mod[l], b_mod[l]).reshape(COND_ROWS, 1, 3 * D_MODEL)
        w_in_l = w_in[l].astype(BF16)
        n_in = w_in_l.shape[1]
        kv_block = (3 if l % 2 == 0 else 1) * HEADS * HEAD_DIM // PROJ_TN
        pp, nk, nv = in_projection(xp, mod3, w_in_l, None, kv_block=kv_block, cache_seq=lp)
        ps = in_projection(xs, mod3, w_in_l, ls)
        pp3 = pp.reshape(bp, lp, n_in)
        ps3 = ps.reshape(bs, ls, n_in)
        if l % 2 == 0:
            pool_w, pool_scale, diff_lam, diff_subln = even_p[l]
            pool_w = pool_w.astype(BF16)
            ya_p = pool_mixer(pp3, pool_w, pool_scale)
            ya_s = pool_mixer(ps3, pool_w, pool_scale)
            yb_p = diff_attention_context(pp3, diff_lam, diff_subln, l)
            yb_s = diff_attention_latent(ps3, diff_lam, diff_subln, l, cache_k[l], cache_v[l])
        else:
            rpb, sgu_ln, sgu_w, sgu_b = odd_p[l]
            sgu_w = sgu_w.astype(BF16)
            ya_p = context_attention(pp3)
            ya_s = neighborhood_attention(ps3, cache_k[l], cache_v[l], rpb)
            yb_p = spatial_gating(pp, sgu_ln, sgu_w, sgu_b)
            yb_s = spatial_gating(ps, sgu_ln, sgu_w, sgu_b)
        half = ya_p.shape[-1]
        w_out_l = w_out[l].astype(BF16)
        xp = out_projection(ya_p.reshape(bp * lp, half), yb_p.reshape(bp * lp, half), xp, mod3,
                            w_out_l, ln_g[l], ln_b[l], None)
        xs = out_projection(ya_s.reshape(bs * ls, half), yb_s.reshape(bs * ls, half), xs, mod3,
                            w_out_l, ln_g[l], ln_b[l], ls)
        new_k.append(nk)
        new_v.append(nv)

    return (xp.reshape(bp, lp, D_MODEL), xs.reshape(bs, ls, D_MODEL),
            new_k[0], new_v[0], new_k[1], new_v[1], new_k[2], new_v[2], new_k[3], new_v[3])
```

```python
import functools
import math

import numpy as np
import jax
import jax.numpy as jnp
from jax import lax
from jax.experimental import pallas as pl
from jax.experimental.pallas import tpu as pltpu

F32 = jnp.float32
BF16 = jnp.bfloat16

D_MODEL = 2048
DEPTH = 4
GRID_W = 64
GRID_ROWS = 64
HEADS = 8
HEAD_DIM = 128
QK_HALF = 64
POOL_WINDOWS = (2, 4, 8, 16)
POOL_GROUP_DIM = 256
SGU_CHUNK = 128
SGU_GROUP_DIM = 256
NA_WIN_R = 8
NA_WIN_C = 16
NA_Q_ROWS = 8
NA_K_ROWS = 16
ROPE_BASE = 10000.0
LN_EPS = 1e-5
NEG_INF = -1e30
DEEPNORM_ALPHA = (2 * DEPTH) ** 0.25
LOG2E = math.log2(math.e)

VMEM_LIMIT_BYTES = 56 * 1024 * 1024
COND_ROWS = 8
CTX_ROW = 4

PROJ_TM = 1024
PROJ_TN = 1024
PROJ_TN_WIDE = (2048, 1792)
OUT_TM = 1024
OUT_ROW_GROUP = 256
ADA_TN = 1536
POOL_ROWS = 256
POOL_HALO = 16
POOL_CHUNKS_PER_ITER = 8
SGU_TM = 1024
DIFF_TQ = 256
DIFF_TILES_PER_ITER = 8
PREP_ROWS = 512
CTX_BATCH_PER_STEP = 4
CTX_HEAD_GROUP = 2


def _params(*sem):
    return pltpu.CompilerParams(dimension_semantics=sem, vmem_limit_bytes=VMEM_LIMIT_BYTES)


def _silu(x):
    half_x = 0.5 * x
    return half_x * jnp.tanh(half_x) + half_x


def _dot_nt(a, b):
    return lax.dot_general(a, b, (((1,), (1,)), ((), ())), preferred_element_type=F32)


def _ada_kernel(c_ref, w_ref, b_ref, o_ref):
    a = _silu(c_ref[...]).astype(BF16)
    o_ref[...] = jnp.dot(a, w_ref[...].astype(BF16), preferred_element_type=F32) + b_ref[...]


def ada_params(cond, w_mod, b_mod):
    n = w_mod.shape[1]
    return pl.pallas_call(
        _ada_kernel,
        out_shape=jax.ShapeDtypeStruct((COND_ROWS, n), F32),
        grid=(n // ADA_TN,),
        in_specs=[pl.BlockSpec((COND_ROWS, D_MODEL), lambda j: (0, 0)),
                  pl.BlockSpec((D_MODEL, ADA_TN), lambda j: (0, j)),
                  pl.BlockSpec((1, ADA_TN), lambda j: (0, j))],
        out_specs=pl.BlockSpec((COND_ROWS, ADA_TN), lambda j: (0, j)),
        compiler_params=_params("parallel"),
        name="ada_params",
    )(cond, w_mod, b_mod.reshape(1, n))


def _inproj_kernel(x_ref, sh_ref, sc_ref, w_ref, o_ref, *rest, kv_block):
    h_scr = rest[-1]
    j = pl.program_id(1)

    @pl.when(j == 0)
    def _():
        h = x_ref[...] * (1.0 + sc_ref[0]) + sh_ref[0]
        h_scr[...] = h.astype(BF16)

    acc = jnp.dot(h_scr[...], w_ref[...], preferred_element_type=F32)
    o_ref[...] = acc.astype(o_ref.dtype)
    if kv_block is not None:
        for which, cache_ref in enumerate(rest[:2]):
            @pl.when(j == kv_block + which)
            def _(cache_ref=cache_ref):
                n_batch, n_heads, seq, head_dim = cache_ref.shape
                for bi in range(n_batch):
                    for hd in range(n_heads):
                        cache_ref[bi, hd] = acc[bi * seq:(bi + 1) * seq, hd * head_dim:(hd + 1) * head_dim]


def in_projection(x2d, mod3, w_bf16, rows_per_cond, kv_block=None, cache_seq=None):
    m = x2d.shape[0]
    n = w_bf16.shape[1]
    if rows_per_cond is None:
        cond_row = lambda i: CTX_ROW
    else:
        blocks_per_cond = rows_per_cond // PROJ_TM
        cond_row = lambda i: i // blocks_per_cond
    tn = PROJ_TN
    if kv_block is None:
        tn = next((c for c in PROJ_TN_WIDE if n % c == 0), PROJ_TN)
    out_shape = [jax.ShapeDtypeStruct((m, n), BF16)]
    out_specs = [pl.BlockSpec((PROJ_TM, tn), lambda i, j: (i, j))]
    if kv_block is not None:
        assert PROJ_TN == HEADS * HEAD_DIM and PROJ_TM % cache_seq == 0
        per_block = PROJ_TM // cache_seq
        cache_spec = pl.BlockSpec((per_block, HEADS, cache_seq, HEAD_DIM), lambda i, j: (i, 0, 0, 0))
        out_shape += [jax.ShapeDtypeStruct((m // cache_seq, HEADS, cache_seq, HEAD_DIM), F32)] * 2
        out_specs += [cache_spec, cache_spec]
    outs = pl.pallas_call(
        functools.partial(_inproj_kernel, kv_block=kv_block),
        out_shape=out_shape,
        grid=(m // PROJ_TM, n // tn),
        in_specs=[pl.BlockSpec((PROJ_TM, D_MODEL), lambda i, j: (i, 0)),
                  pl.BlockSpec((1, 1, D_MODEL), lambda i, j: (cond_row(i), 0, 0)),
                  pl.BlockSpec((1, 1, D_MODEL), lambda i, j: (cond_row(i), 0, 1)),
                  pl.BlockSpec((D_MODEL, tn), lambda i, j: (0, j))],
        out_specs=out_specs,
        scratch_shapes=[pltpu.VMEM((PROJ_TM, D_MODEL), BF16)],
        compiler_params=_params("parallel", "arbitrary"),
        name="in_projection",
    )(x2d, mod3, mod3, w_bf16)
    return outs[0] if kv_block is None else outs


def _outproj_kernel(ya_ref, yb_ref, x_ref, g_ref, w_ref, lng_ref, lnb_ref, o_ref):
    half = ya_ref.shape[1]
    for r in range(OUT_TM // OUT_ROW_GROUP):
        rows = slice(r * OUT_ROW_GROUP, (r + 1) * OUT_ROW_GROUP)
        acc = jnp.dot(ya_ref[rows, :], w_ref[0:half, :], preferred_element_type=F32)
        acc = acc + jnp.dot(yb_ref[rows, :], w_ref[half:, :], preferred_element_type=F32)
        z = DEEPNORM_ALPHA * x_ref[rows, :] + g_ref[0] * acc
        mu = jnp.mean(z, axis=-1, keepdims=True)
        zc = z - mu
        var = jnp.mean(zc * zc, axis=-1, keepdims=True)
        o_ref[rows, :] = zc * lax.rsqrt(var + LN_EPS) * lng_ref[...] + lnb_ref[...]


def out_projection(ya, yb, x2d, mod3, w_bf16, ln_g, ln_b, rows_per_cond):
    m = x2d.shape[0]
    half = ya.shape[1]
    if rows_per_cond is None:
        cond_row = lambda i: CTX_ROW
    else:
        blocks_per_cond = rows_per_cond // OUT_TM
        cond_row = lambda i: i // blocks_per_cond
    return pl.pallas_call(
        _outproj_kernel,
        out_shape=jax.ShapeDtypeStruct((m, D_MODEL), F32),
        grid=(m // OUT_TM,),
        in_specs=[pl.BlockSpec((OUT_TM, half), lambda i: (i, 0)),
                  pl.BlockSpec((OUT_TM, half), lambda i: (i, 0)),
                  pl.BlockSpec((OUT_TM, D_MODEL), lambda i: (i, 0)),
                  pl.BlockSpec((1, 1, D_MODEL), lambda i: (cond_row(i), 0, 2)),
                  pl.BlockSpec((2 * half, D_MODEL), lambda i: (0, 0), pipeline_mode=pl.Buffered(1)),
                  pl.BlockSpec((1, D_MODEL), lambda i: (0, 0)),
                  pl.BlockSpec((1, D_MODEL), lambda i: (0, 0))],
        out_specs=pl.BlockSpec((OUT_TM, D_MODEL), lambda i: (i, 0)),
        compiler_params=_params("parallel"),
        name="out_projection",
    )(ya, yb, x2d, mod3, w_bf16, ln_g.reshape(1, D_MODEL), ln_b.reshape(1, D_MODEL))


def _pool_bands():
    assert max(POOL_WINDOWS) // 2 <= POOL_HALO
    i = np.arange(POOL_ROWS)[:, None]
    own = np.arange(POOL_ROWS)[None, :]
    mains = [(own >= i - w // 2) & (own < i + w // 2) for w in POOL_WINDOWS]
    top_i = np.arange(POOL_HALO)[:, None]
    bot_i = POOL_ROWS - POOL_HALO + top_i
    before = np.arange(POOL_HALO)[None, :] - POOL_HALO
    after = POOL_ROWS + np.arange(POOL_HALO)[None, :]
    halos = [np.stack([before >= top_i - w // 2, after < bot_i + w // 2]) for w in POOL_WINDOWS]
    return np.stack(mains).astype(np.float32), np.stack(halos).astype(np.float32)


def _pool_kernel(a_ref, g_ref, w_ref, s_ref, bm_ref, bh_ref, o_ref, *, seq):
    n_chunks = seq // POOL_ROWS
    group = pl.program_id(1)
    half = jnp.int32(0)
    for gi, window in enumerate(POOL_WINDOWS):
        half = jnp.where(group == gi, window // 2, half)

    def chunk(bi, ci):
        r0 = pl.multiple_of(ci * POOL_ROWS, POOL_ROWS)
        main = a_ref[bi, pl.ds(r0, POOL_ROWS), :]
        win = jnp.dot(bm_ref[0], main.astype(BF16), preferred_element_type=F32)
        if n_chunks > 1:
            p0 = pl.multiple_of(jnp.maximum(r0 - POOL_HALO, 0), POOL_HALO)
            n0 = pl.multiple_of(jnp.minimum(r0 + POOL_ROWS, seq - POOL_HALO), POOL_HALO)
            prev = jnp.where(ci > 0, a_ref[bi, pl.ds(p0, POOL_HALO), :].astype(F32), 0.0).astype(BF16)
            nxt = jnp.where(ci < n_chunks - 1, a_ref[bi, pl.ds(n0, POOL_HALO), :].astype(F32), 0.0).astype(BF16)
            top = win[0:POOL_HALO] + jnp.dot(bh_ref[0, 0], prev, preferred_element_type=F32)
            bot = win[POOL_ROWS - POOL_HALO:] + jnp.dot(bh_ref[0, 1], nxt, preferred_element_type=F32)
            win = jnp.concatenate([top, win[POOL_HALO:POOL_ROWS - POOL_HALO], bot], axis=0)
        t = r0 + lax.broadcasted_iota(jnp.int32, (POOL_ROWS, 1), 0)
        cnt = (jnp.minimum(t + half, seq) - jnp.maximum(t - half, 0)).astype(F32)
        pooled = win / cnt - main.astype(F32)
        mixed = jnp.dot(pooled.astype(BF16), w_ref[0], preferred_element_type=F32)
        gate = g_ref[bi, pl.ds(r0, POOL_ROWS), :].astype(F32)
        y = mixed * s_ref[...] * _silu(gate)
        o_ref[bi, pl.ds(r0, POOL_ROWS), :] = y.astype(o_ref.dtype)

    def group_of_chunks(it, carry):
        for u in range(POOL_CHUNKS_PER_ITER):
            idx = it * POOL_CHUNKS_PER_ITER + u
            chunk(idx // n_chunks, idx % n_chunks)
        return carry

    lax.fori_loop(0, a_ref.shape[0] * n_chunks // POOL_CHUNKS_PER_ITER, group_of_chunks, 0)


def pool_mixer(p3, w_pool_bf16, pool_scale):
    b, seq, _ = p3.shape
    groups = len(POOL_WINDOWS)
    band_main, band_halo = (jnp.asarray(m, BF16) for m in _pool_bands())
    bb = max(1, POOL_CHUNKS_PER_ITER * POOL_ROWS // seq)
    return pl.pallas_call(
        functools.partial(_pool_kernel, seq=seq),
        out_shape=jax.ShapeDtypeStruct((b, seq, groups * POOL_GROUP_DIM), BF16),
        grid=(b // bb, groups),
        in_specs=[pl.BlockSpec((bb, seq, POOL_GROUP_DIM), lambda i, g: (i, 0, g)),
                  pl.BlockSpec((bb, seq, POOL_GROUP_DIM), lambda i, g: (i, 0, groups + g)),
                  pl.BlockSpec((1, POOL_GROUP_DIM, POOL_GROUP_DIM), lambda i, g: (g, 0, 0)),
                  pl.BlockSpec((1, POOL_GROUP_DIM), lambda i, g: (0, g)),
                  pl.BlockSpec((1, POOL_ROWS, POOL_ROWS), lambda i, g: (g, 0, 0)),
                  pl.BlockSpec((1, 2, POOL_HALO, POOL_HALO), lambda i, g: (g, 0, 0, 0))],
        out_specs=pl.BlockSpec((bb, seq, POOL_GROUP_DIM), lambda i, g: (i, 0, g)),
        compiler_params=_params("parallel", "parallel"),
        name="pool_mixer",
    )(p3, p3, w_pool_bf16, pool_scale.reshape(1, groups * POOL_GROUP_DIM), band_main, band_halo)


def _rope_tables():
    t = np.arange(GRID_ROWS * GRID_W)
    row = (t // GRID_W).astype(np.float64)[:, None]
    col = (t % GRID_W).astype(np.float64)[:, None]
    axis_dim = QK_HALF // 2
    inv = 1.0 / (ROPE_BASE ** (np.arange(0, axis_dim, 2, dtype=np.float64) / axis_dim))
    lane = np.arange(HEAD_DIM)
    within = lane % QK_HALF
    pos = np.where((within < axis_dim)[None, :], row, col)
    ang = pos * inv[within % (axis_dim // 2)][None, :]
    is_a = ((within % axis_dim) < axis_dim // 2)[None, :]
    cos = np.cos(ang)
    sin = np.sin(ang)
    return (cos.astype(np.float32),
            np.where(is_a, -sin, 0.0).astype(np.float32),
            np.where(is_a, 0.0, sin).astype(np.float32))


def _rotate_lanes(x, cos, sin_a, sin_b):
    shift = QK_HALF // 4
    return (x * cos + pltpu.roll(x, HEAD_DIM - shift, 1) * sin_a
            + pltpu.roll(x, shift, 1) * sin_b)


def _diff_lambda(lam_ref, lam_init):
    dl = lam_ref[...]
    return (jnp.exp(jnp.sum(dl[0:1] * dl[1:2], axis=-1, keepdims=True))
            - jnp.exp(jnp.sum(dl[2:3] * dl[3:4], axis=-1, keepdims=True)) + lam_init)


def _split_components(q):
    lane = lax.broadcasted_iota(jnp.int32, (1, HEAD_DIM), 1)
    return (jnp.where(lane < QK_HALF, q, 0.0).astype(BF16),
            jnp.where(lane >= QK_HALF, q, 0.0).astype(BF16))


def _diff_output(r1, r2, lam, sub, lam_init, gate):
    o = r1[:, 0:HEAD_DIM] * (1.0 / r1[:, HEAD_DIM:]) - r2[:, 0:HEAD_DIM] * (lam / r2[:, HEAD_DIM:])
    o = o * lax.rsqrt(jnp.mean(o * o, axis=-1, keepdims=True) + LN_EPS)
    o = o * sub * (1.0 - lam_init)
    return o * _silu(gate)


def _diff_latent_kernel(q_ref, k_ref, v_ref, g_ref, ck_ref, cv_ref, cos_ref, sa_ref, sb_ref,
                        lam_ref, sub_ref, o_ref, k_scr, v_scr, *, seq, tq, lam_init):
    ones_col = jnp.ones((PREP_ROWS, HEAD_DIM), BF16)

    def stage(ci, carry):
        rows = pl.ds(pl.multiple_of(ci * PREP_ROWS, PREP_ROWS), PREP_ROWS)
        k = _rotate_lanes(k_ref[0, rows, :].astype(F32), cos_ref[rows, :], sa_ref[rows, :], sb_ref[rows, :])
        k_scr[rows, :] = k.astype(BF16)
        v_scr[rows, 0:HEAD_DIM] = v_ref[0, rows, :].astype(BF16)
        v_scr[rows, HEAD_DIM:] = ones_col
        return carry

    lax.fori_loop(0, seq // PREP_ROWS, stage, 0)
    k_scr[seq:, :] = ck_ref[0, 0].astype(BF16)
    v_scr[seq:, 0:HEAD_DIM] = cv_ref[0, 0].astype(BF16)
    v_scr[seq:, HEAD_DIM:] = ones_col[0:v_scr.shape[0] - seq]
    lam = _diff_lambda(lam_ref, lam_init)

    def tiles(it, carry):
        keys_all = k_scr[...]
        values = v_scr[...]
        def rows_of(u):
            return pl.ds(pl.multiple_of((it * DIFF_TILES_PER_ITER + u) * tq, tq), tq)

        def scores(u):
            r = rows_of(u)
            pair = _split_components(
                _rotate_lanes(q_ref[0, r, :].astype(F32), cos_ref[r, :], sa_ref[r, :], sb_ref[r, :])
                * (QK_HALF ** -0.5 * LOG2E))
            return [_dot_nt(qc, keys_all) for qc in pair]

        def finish(u, s):
            r = rows_of(u)
            e = [jnp.exp2(sc - jnp.max(sc, axis=-1, keepdims=True)).astype(BF16) for sc in s]
            r1, r2 = [jnp.dot(ec, values, preferred_element_type=F32) for ec in e]
            y = _diff_output(r1, r2, lam, sub_ref[...], lam_init, g_ref[0, r, :].astype(F32))
            o_ref[0, r, :] = y.astype(o_ref.dtype)

        pending = scores(0)
        for u in range(DIFF_TILES_PER_ITER):
            upcoming = scores(u + 1) if u + 1 < DIFF_TILES_PER_ITER else None
            finish(u, pending)
            pending = upcoming
        return carry

    lax.fori_loop(0, seq // (tq * DIFF_TILES_PER_ITER), tiles, 0)


def diff_attention_latent(p3, diff_lam, diff_subln, layer_idx, ctx_k, ctx_v):
    b, seq, _ = p3.shape
    ctx_len = ctx_k.shape[2]
    tq = DIFF_TQ
    nk = seq + ctx_len
    lam_init = 0.8 - 0.6 * math.exp(-0.3 * layer_idx)
    q0, k0, v0, g0 = 2 * HEADS, 3 * HEADS, 4 * HEADS, 5 * HEADS
    const = lambda i, h: (0, 0)
    head_blk = lambda off: pl.BlockSpec((1, seq, HEAD_DIM), lambda i, h: (i, 0, off + h))
    ctx_spec = pl.BlockSpec((1, 1, ctx_len, HEAD_DIM), lambda i, h: (i, h, 0, 0))
    tab_spec = pl.BlockSpec((seq, HEAD_DIM), const, pipeline_mode=pl.Buffered(1))
    return pl.pallas_call(
        functools.partial(_diff_latent_kernel, seq=seq, tq=tq, lam_init=lam_init),
        out_shape=jax.ShapeDtypeStruct((b, seq, HEADS * HEAD_DIM), BF16),
        grid=(b, HEADS),
        in_specs=[head_blk(q0), head_blk(k0), head_blk(v0), head_blk(g0),
                  ctx_spec, ctx_spec, tab_spec, tab_spec, tab_spec,
                  pl.BlockSpec(diff_lam.shape, const),
                  pl.BlockSpec((1, HEAD_DIM), const)],
        out_specs=pl.BlockSpec((1, seq, HEAD_DIM), lambda i, h: (i, 0, h)),
        scratch_shapes=[pltpu.VMEM((nk, HEAD_DIM), BF16), pltpu.VMEM((nk, 2 * HEAD_DIM), BF16)],
        compiler_params=_params("parallel", "parallel"),
        name="diff_attention_latent",
    )(p3, p3, p3, p3, ctx_k, ctx_v, *[jnp.asarray(tab) for tab in _rope_tables()],
      diff_lam, diff_subln.reshape(1, HEAD_DIM))


def _diff_context_kernel(q_ref, k_ref, v_ref, g_ref, lam_ref, sub_ref, o_ref, *, lam_init):
    lam = _diff_lambda(lam_ref, lam_init)
    heads = [slice(h * HEAD_DIM, (h + 1) * HEAD_DIM) for h in range(HEADS)]
    seq = q_ref.shape[1]
    ones_col = jnp.ones((seq, HEAD_DIM), BF16)

    def one(bi, carry):
        def scores(group):
            qs = [_split_components(q_ref[bi, :, c].astype(F32) * (QK_HALF ** -0.5 * LOG2E)) for c in group]
            return [[_dot_nt(qc, k_ref[bi, :, c]) for qc in pair] for pair, c in zip(qs, group)]

        def finish(group, s):
            e = [[jnp.exp2(sc - jnp.max(sc, axis=-1, keepdims=True)).astype(BF16) for sc in sh] for sh in s]
            v_aug = [jnp.concatenate([v_ref[bi, :, c], ones_col], axis=-1) for c in group]
            r = [[jnp.dot(ec, va, preferred_element_type=F32) for ec in eh] for eh, va in zip(e, v_aug)]
            for (r1, r2), c in zip(r, group):
                y = _diff_output(r1, r2, lam, sub_ref[...], lam_init, g_ref[bi, :, c].astype(F32))
                o_ref[bi, :, c] = y.astype(o_ref.dtype)

        groups = [heads[i:i + CTX_HEAD_GROUP] for i in range(0, HEADS, CTX_HEAD_GROUP)]
        pending = scores(groups[0])
        for n, group in enumerate(groups):
            upcoming = scores(groups[n + 1]) if n + 1 < len(groups) else None
            finish(group, pending)
            pending = upcoming
        return carry

    lax.fori_loop(0, q_ref.shape[0], one, 0)


def diff_attention_context(p3, diff_lam, diff_subln, layer_idx):
    b, seq, _ = p3.shape
    width = HEADS * HEAD_DIM
    lam_init = 0.8 - 0.6 * math.exp(-0.3 * layer_idx)
    bb = CTX_BATCH_PER_STEP
    blk = lambda c: pl.BlockSpec((bb, seq, width), lambda i: (i, 0, c))
    return pl.pallas_call(
        functools.partial(_diff_context_kernel, lam_init=lam_init),
        out_shape=jax.ShapeDtypeStruct((b, seq, width), BF16),
        grid=(b // bb,),
        in_specs=[blk(2), blk(3), blk(4), blk(5),
                  pl.BlockSpec(diff_lam.shape, lambda i: (0, 0)),
                  pl.BlockSpec((1, HEAD_DIM), lambda i: (0, 0))],
        out_specs=pl.BlockSpec((bb, seq, width), lambda i: (i, 0, 0)),
        compiler_params=_params("parallel"),
        name="diff_attention_context",
    )(p3, p3, p3, p3, diff_lam, diff_subln.reshape(1, HEAD_DIM))


def _ctx_attn_kernel(q_ref, k_ref, v_ref, g_ref, o_ref):
    heads = [slice(h * HEAD_DIM, (h + 1) * HEAD_DIM) for h in range(HEADS)]
    seq = q_ref.shape[1]
    ones_col = jnp.ones((seq, HEAD_DIM), BF16)

    def one(bi, carry):
        q = [(q_ref[bi, :, c].astype(F32) * (HEAD_DIM ** -0.5 * LOG2E)).astype(BF16) for c in heads]
        s = [_dot_nt(q[h], k_ref[bi, :, c]) for h, c in enumerate(heads)]
        e = [jnp.exp2(sh - jnp.max(sh, axis=-1, keepdims=True)).astype(BF16) for sh in s]
        r = [jnp.dot(e[h], jnp.concatenate([v_ref[bi, :, c], ones_col], axis=-1), preferred_element_type=F32)
             for h, c in enumerate(heads)]
        for h, c in enumerate(heads):
            o = r[h][:, 0:HEAD_DIM] / r[h][:, HEAD_DIM:]
            o_ref[bi, :, c] = (o * _silu(g_ref[bi, :, c].astype(F32))).astype(o_ref.dtype)
        return carry

    lax.fori_loop(0, q_ref.shape[0], one, 0)


def context_attention(p3):
    b, seq, _ = p3.shape
    width = HEADS * HEAD_DIM
    bb = CTX_BATCH_PER_STEP
    blk = lambda c: pl.BlockSpec((bb, seq, width), lambda i: (i, 0, c))
    return pl.pallas_call(
        _ctx_attn_kernel,
        out_shape=jax.ShapeDtypeStruct((b, seq, width), BF16),
        grid=(b // bb,),
        in_specs=[blk(0), blk(1), blk(2), blk(3)],
        out_specs=pl.BlockSpec((bb, seq, width), lambda i: (i, 0, 0)),
        compiler_params=_params("parallel"),
        name="context_attention",
    )(p3, p3, p3, p3)


NA_TILE_Q = NA_Q_ROWS * GRID_W
NA_TILE_K = NA_K_ROWS * GRID_W
NA_TILES = GRID_ROWS // NA_Q_ROWS
NA_TILES_PER_ITER = 8
NA_CASES = 3
NA_KEY_ALIGN = (NA_WIN_R // 2) * GRID_W
NA_DR = 2 * NA_WIN_R - 1


def _na_key_row_start(tile):
    return int(np.clip(NA_Q_ROWS * tile - NA_WIN_R // 2, 0, GRID_ROWS - NA_K_ROWS))


def _na_bias_layout():
    dr = np.full((NA_CASES, NA_Q_ROWS, NA_K_ROWS), NA_DR, np.int32)
    for case, tile in enumerate((0, 1, NA_TILES - 1)):
        ks = _na_key_row_start(tile)
        for qr_local in range(NA_Q_ROWS):
            qr = NA_Q_ROWS * tile + qr_local
            rs = int(np.clip(qr - NA_WIN_R // 2, 0, GRID_ROWS - NA_WIN_R))
            for kr_local in range(NA_K_ROWS):
                kr = ks + kr_local
                if rs <= kr < rs + NA_WIN_R:
                    dr[case, qr_local, kr_local] = kr - qr + NA_WIN_R - 1
    pairs = sorted({(int(a), int(b)) for a, b in zip(dr[..., 0::2].ravel(), dr[..., 1::2].ravel())})
    index = {p: i for i, p in enumerate(pairs)}
    block = np.array([[[index[(int(dr[c, q, 2 * p]), int(dr[c, q, 2 * p + 1]))]
                        for p in range(NA_K_ROWS // 2)] for q in range(NA_Q_ROWS)] for c in range(NA_CASES)])
    return pairs, block


def _na_bias_blocks(rpb):
    qc = np.arange(GRID_W)[:, None]
    kc = np.arange(GRID_W)[None, :]
    cstart = np.clip(qc - NA_WIN_C // 2, 0, GRID_W - NA_WIN_C)
    col_valid = (kc >= cstart) & (kc < cstart + NA_WIN_C)
    dc = np.clip(kc - qc + NA_WIN_C - 1, 0, 2 * NA_WIN_C - 2)
    onehot = (dc.reshape(-1)[None, :] == np.arange(2 * NA_WIN_C - 1)[:, None]).astype(np.float32)
    toeplitz = jnp.einsum('hrj,jn->hrn', rpb, jnp.asarray(onehot), precision=lax.Precision.HIGHEST)
    per_dr = jnp.where(jnp.asarray(col_valid), toeplitz.reshape(rpb.shape[:2] + dc.shape), NEG_INF)
    per_dr = jnp.concatenate([per_dr, jnp.full_like(per_dr[:, :1], NEG_INF)], axis=1)
    pairs, _ = _na_bias_layout()
    left = np.array([p[0] for p in pairs])
    right = np.array([p[1] for p in pairs])
    return jnp.concatenate([per_dr[:, left], per_dr[:, right]], axis=-1)


def _na_kernel(q_ref, k_ref, v_ref, g_ref, ck_ref, cv_ref, tab_ref, o_ref,
               bias_scr, ck_scr, cv_scr, v_scr):
    b = pl.program_id(1)
    seq = v_ref.shape[1]

    @pl.when(b == 0)
    def _build_bias():
        _, block = _na_bias_layout()
        for case in range(NA_CASES):
            for qr in range(NA_Q_ROWS):
                for p in range(NA_K_ROWS // 2):
                    bias_scr[case, qr * GRID_W:(qr + 1) * GRID_W, p * 2 * GRID_W:(p + 1) * 2 * GRID_W] = (
                        tab_ref[0, int(block[case, qr, p])] * LOG2E)

    ones_col = jnp.ones((PREP_ROWS, HEAD_DIM), BF16)
    ck_scr[...] = ck_ref[0, 0].astype(BF16)
    cv_scr[:, 0:HEAD_DIM] = cv_ref[0, 0].astype(BF16)
    cv_scr[:, HEAD_DIM:] = ones_col[0:cv_scr.shape[0]]

    def stage(ci, carry):
        rows = pl.ds(pl.multiple_of(ci * PREP_ROWS, PREP_ROWS), PREP_ROWS)
        v_scr[rows, 0:HEAD_DIM] = v_ref[0, rows, :]
        v_scr[rows, HEAD_DIM:] = ones_col
        return carry

    lax.fori_loop(0, seq // PREP_ROWS, stage, 0)
    half_rows = NA_TILE_Q // 2

    def tiles(it, carry):
        k_ctx = ck_scr[...]
        v_ctx = cv_scr[...]

        def window(u):
            t = it * NA_TILES_PER_ITER + u
            case = jnp.where(t == 0, 0, jnp.where(t == NA_TILES - 1, 2, 1))
            key_row = jnp.clip(NA_Q_ROWS * t - NA_WIN_R // 2, 0, GRID_ROWS - NA_K_ROWS)
            keys = pl.ds(pl.multiple_of(key_row * GRID_W, NA_KEY_ALIGN), NA_TILE_K)
            rows = [pl.ds(pl.multiple_of(t * NA_TILE_Q + i * half_rows, half_rows), half_rows) for i in range(2)]
            return case, keys, rows

        def scores(u):
            case, keys, rows = window(u)
            k_all = jnp.concatenate([k_ref[0, keys, :], k_ctx], axis=0)
            q = [(q_ref[0, r, :].astype(F32) * (HEAD_DIM ** -0.5 * LOG2E)).astype(BF16) for r in rows]
            s = [_dot_nt(qh, k_all) for qh in q]
            return [jnp.concatenate([sh[:, 0:NA_TILE_K] + bias_scr[case, i * half_rows:(i + 1) * half_rows, :],
                                     sh[:, NA_TILE_K:]], axis=-1) for i, sh in enumerate(s)]

        def finish(u, s):
            _, keys, rows = window(u)
            e = [jnp.exp2(sh - jnp.max(sh, axis=-1, keepdims=True)).astype(BF16) for sh in s]
            v_all = jnp.concatenate([v_scr[keys, :], v_ctx], axis=0)
            res = [jnp.dot(eh, v_all, preferred_element_type=F32) for eh in e]
            for r, rh in zip(rows, res):
                o = rh[:, 0:HEAD_DIM] / rh[:, HEAD_DIM:]
                o_ref[0, r, :] = (o * _silu(g_ref[0, r, :].astype(F32))).astype(o_ref.dtype)

        pending = scores(0)
        for u in range(NA_TILES_PER_ITER):
            upcoming = scores(u + 1) if u + 1 < NA_TILES_PER_ITER else None
            finish(u, pending)
            pending = upcoming
        return carry

    lax.fori_loop(0, NA_TILES // NA_TILES_PER_ITER, tiles, 0)


def neighborhood_attention(p3, ctx_k, ctx_v, rpb):
    b, seq, _ = p3.shape
    ctx_len = ctx_k.shape[2]
    tab = _na_bias_blocks(rpb)
    n_pairs = tab.shape[1]
    ctx_spec = pl.BlockSpec((1, 1, ctx_len, HEAD_DIM), lambda h, i: (i, h, 0, 0))
    head_blk = lambda off: pl.BlockSpec((1, seq, HEAD_DIM), lambda h, i: (i, 0, off + h))
    return pl.pallas_call(
        _na_kernel,
        out_shape=jax.ShapeDtypeStruct((b, seq, HEADS * HEAD_DIM), BF16),
        grid=(HEADS, b),
        in_specs=[head_blk(0), head_blk(HEADS), head_blk(2 * HEADS), head_blk(3 * HEADS),
                  ctx_spec, ctx_spec,
                  pl.BlockSpec((1, n_pairs, GRID_W, 2 * GRID_W), lambda h, i: (h, 0, 0, 0))],
        out_specs=pl.BlockSpec((1, seq, HEAD_DIM), lambda h, i: (i, 0, h)),
        scratch_shapes=[pltpu.VMEM((NA_CASES, NA_TILE_Q, NA_TILE_K), F32),
                        pltpu.VMEM((ctx_len, HEAD_DIM), BF16), pltpu.VMEM((ctx_len, 2 * HEAD_DIM), BF16),
                        pltpu.VMEM((seq, 2 * HEAD_DIM), BF16)],
        compiler_params=_params("arbitrary", "arbitrary"),
        name="neighborhood_attention",
    )(p3, p3, p3, p3, ctx_k, ctx_v, tab)


def _sgu_kernel(u_ref, v_ref, g_ref, ln_ref, w_ref, b_ref, o_ref):
    v = v_ref[...].astype(F32)
    mu = jnp.mean(v, axis=-1, keepdims=True)
    vc = v - mu
    var = jnp.mean(vc * vc, axis=-1, keepdims=True)
    vn = (vc * lax.rsqrt(var + LN_EPS) * ln_ref[...]).astype(BF16)
    groups = w_ref.shape[0]
    for n in range(v.shape[0] // SGU_CHUNK):
        rows = slice(n * SGU_CHUNK, (n + 1) * SGU_CHUNK)
        for g in range(groups):
            cols = slice(g * SGU_GROUP_DIM, (g + 1) * SGU_GROUP_DIM)
            s = jnp.dot(w_ref[g], vn[rows, cols], preferred_element_type=F32) + b_ref[:, g:g + 1]
            y = u_ref[rows, cols].astype(F32) * s * _silu(g_ref[rows, cols].astype(F32))
            o_ref[rows, cols] = y.astype(o_ref.dtype)


def spatial_gating(p2, sgu_ln, w_s_bf16, b_s):
    m = p2.shape[0]
    groups = w_s_bf16.shape[0]
    width = groups * SGU_GROUP_DIM
    blk = lambda c: pl.BlockSpec((SGU_TM, width), lambda i: (i, c))
    return pl.pallas_call(
        _sgu_kernel,
        out_shape=jax.ShapeDtypeStruct((m, width), BF16),
        grid=(m // SGU_TM,),
        in_specs=[blk(4), blk(5), blk(6),
                  pl.BlockSpec((1, width), lambda i: (0, 0)),
                  pl.BlockSpec((groups, SGU_CHUNK, SGU_CHUNK), lambda i: (0, 0, 0)),
                  pl.BlockSpec((SGU_CHUNK, groups), lambda i: (0, 0))],
        out_specs=pl.BlockSpec((SGU_TM, width), lambda i: (i, 0)),
        compiler_params=_params("parallel"),
        name="spatial_gating",
    )(p2, p2, p2, sgu_ln.reshape(1, width), w_s_bf16, jnp.transpose(b_s))


def kernel(x_prompt, x_sample, cache_k_l0, cache_v_l0, cache_k_l1, cache_v_l1, cache_k_l2, cache_v_l2, cache_k_l3, cache_v_l3, c, c_ctx, w_mod_0, b_mod_0, w_in_0, w_out_0, ln_g_0, ln_b_0, pool_w_0, pool_scale_0, diff_lam_0, diff_subln_0, w_mod_1, b_mod_1, w_in_1, w_out_1, ln_g_1, ln_b_1, rpb_1, sgu_ln_1, sgu_w_1, sgu_b_1, w_mod_2, b_mod_2, w_in_2, w_out_2, ln_g_2, ln_b_2, pool_w_2, pool_scale_2, diff_lam_2, diff_subln_2, w_mod_3, b_mod_3, w_in_3, w_out_3, ln_g_3, ln_b_3, rpb_3, sgu_ln_3, sgu_w_3, sgu_b_3):
    cache_k = [cache_k_l0, cache_k_l1, cache_k_l2, cache_k_l3]
    cache_v = [cache_v_l0, cache_v_l1, cache_v_l2, cache_v_l3]
    w_mod = [w_mod_0, w_mod_1, w_mod_2, w_mod_3]
    b_mod = [b_mod_0, b_mod_1, b_mod_2, b_mod_3]
    w_in = [w_in_0, w_in_1, w_in_2, w_in_3]
    w_out = [w_out_0, w_out_1, w_out_2, w_out_3]
    ln_g = [ln_g_0, ln_g_1, ln_g_2, ln_g_3]
    ln_b = [ln_b_0, ln_b_1, ln_b_2, ln_b_3]
    even_p = {0: (pool_w_0, pool_scale_0, diff_lam_0, diff_subln_0),
              2: (pool_w_2, pool_scale_2, diff_lam_2, diff_subln_2)}
    odd_p = {1: (rpb_1, sgu_ln_1, sgu_w_1, sgu_b_1),
             3: (rpb_3, sgu_ln_3, sgu_w_3, sgu_b_3)}

    bp, lp, _ = x_prompt.shape
    bs, ls, _ = x_sample.shape
    xp = x_prompt.reshape(bp * lp, D_MODEL)
    xs = x_sample.reshape(bs * ls, D_MODEL)
    cond = jnp.concatenate([c, c_ctx[None, :], jnp.zeros((COND_ROWS - bs - 1, D_MODEL), F32)], axis=0)

    new_k, new_v = [], []
    for l in range(DEPTH):
        mod3 = ada_params(cond, w_mod[l], b_mod[l]).reshape(COND_ROWS, 1, 3 * D_MODEL)
        w_in_l = w_in[l].astype(BF16)
        n_in = w_in_l.shape[1]
        kv_block = (3 if l % 2 == 0 else 1) * HEADS * HEAD_DIM // PROJ_TN
        pp, nk, nv = in_projection(xp, mod3, w_in_l, None, kv_block=kv_block, cache_seq=lp)
        ps = in_projection(xs, mod3, w_in_l, ls)
        pp3 = pp.reshape(bp, lp, n_in)
        ps3 = ps.reshape(bs, ls, n_in)
        if l % 2 == 0:
            pool_w, pool_scale, diff_lam, diff_subln = even_p[l]
            pool_w = pool_w.astype(BF16)
            ya_p = pool_mixer(pp3, pool_w, pool_scale)
            ya_s = pool_mixer(ps3, pool_w, pool_scale)
            yb_p = diff_attention_context(pp3, diff_lam, diff_subln, l)
            yb_s = diff_attention_latent(ps3, diff_lam, diff_subln, l, cache_k[l], cache_v[l])
        else:
            rpb, sgu_ln, sgu_w, sgu_b = odd_p[l]
            sgu_w = sgu_w.astype(BF16)
            ya_p = context_attention(pp3)
            ya_s = neighborhood_attention(ps3, cache_k[l], cache_v[l], rpb)
            yb_p = spatial_gating(pp, sgu_ln, sgu_w, sgu_b)
            yb_s = spatial_gating(ps, sgu_ln, sgu_w, sgu_b)
        half = ya_p.shape[-1]
        w_out_l = w_out[l].astype(BF16)
        xp = out_projection(ya_p.reshape(bp * lp, half), yb_p.reshape(bp * lp, half), xp, mod3,
                            w_out_l, ln_g[l], ln_b[l], None)
        xs = out_projection(ya_s.reshape(bs * ls, half), yb_s.reshape(bs * ls, half), xs, mod3,
                            w_out_l, ln_g[l], ln_b[l], ls)
        new_k.append(nk)
        new_v.append(nv)

    return (xp.reshape(bp, lp, D_MODEL), xs.reshape(bs, ls, D_MODEL),
            new_k[0], new_v[0], new_k[1], new_v[1], new_k[2], new_v[2], new_k[3], new_v[3])
```
